```python
import jax
import jax.numpy as jnp
from jax import lax
import numpy as np

D_MODEL = 4096
BATCH = 4
SEQ = 2048
DEPTH = 2

CHUNK = 64
N_A_LAYERS = DEPTH // 2
N_B_LAYERS = DEPTH - N_A_LAYERS

GLA_HEADS = 8
GLA_DK = D_MODEL // 16
GLA_DV = D_MODEL // 8
GLA_GATE_RANK = 16
GLA_TAU = 16.0
GLA_QK = GLA_HEADS * GLA_DK
GLA_V = GLA_HEADS * GLA_DV
GLA_SPLITS = [GLA_QK, 2 * GLA_QK, 2 * GLA_QK + GLA_V, 2 * GLA_QK + 2 * GLA_V]
GLA_IN_COLS = 2 * GLA_QK + 2 * GLA_V + GLA_GATE_RANK

SB_HEADS = 32
SB_DH = D_MODEL // SB_HEADS
SB_QBLOCK = 128

N_EXPERTS = 32
TOP_K = 4
EXPERT_FF = 768
SWIGLU_LIMIT = 7.0
SWIGLU_ALPHA = 1.702
MOE_ROWS = 128

DN_ALPHA = (2.0 * DEPTH) ** 0.25
DN_BETA = (8.0 * DEPTH) ** -0.25
LN_EPS = 1e-5
RMS_EPS = 1e-6

kernel_name = 'yoco_gla_stickbreaking_moe_deepnorm'


def layer_norm(x, g, b):
    xf = x.astype(jnp.float32)
    xc = xf - jnp.mean(xf, axis=-1, keepdims=True)
    var = jnp.mean(xc * xc, axis=-1, keepdims=True)
    return (xc * lax.rsqrt(var + LN_EPS) * g + b).astype(x.dtype)


def gla_mixer(x, w_in, w_gate2, b_gate2, norm_g, w_out):
    bsz, seq, _ = x.shape
    n_chunks = seq // CHUNK
    q, k, v, r, g_low = jnp.split(x @ w_in, GLA_SPLITS, axis=-1)
    log_a = jax.nn.log_sigmoid((g_low @ w_gate2 + b_gate2).astype(jnp.float32)) / GLA_TAU

    def to_chunks(t, hd):
        t = t.astype(jnp.float32).reshape(bsz, n_chunks, CHUNK, GLA_HEADS, hd)
        return t.transpose(1, 0, 3, 2, 4)

    qc = to_chunks(q, GLA_DK) * (GLA_DK ** -0.5)
    kc = to_chunks(k, GLA_DK)
    vc = to_chunks(v, GLA_DV)
    cum_a = jnp.cumsum(to_chunks(log_a, GLA_DK), axis=3)
    end_a = cum_a[:, :, :, -1:, :]
    k_dec = kc * jnp.exp(end_a - cum_a)
    chunk_decay = jnp.exp(end_a[:, :, :, 0, :])

    def chunk_step(state, inp):
        k_c, v_c, q_c, dec_c = inp
        state = dec_c[..., None] * state + jnp.einsum('bhck,bhcv->bhkv', k_c, v_c)
        return state, jnp.einsum('bhck,bhkv->bhcv', q_c, state)

    state0 = jnp.zeros((bsz, GLA_HEADS, GLA_DK, GLA_DV), jnp.float32)
    _, o = lax.scan(chunk_step, state0, (k_dec, vc, qc, chunk_decay))
    o = o.transpose(1, 0, 3, 2, 4).reshape(bsz, seq, GLA_HEADS, GLA_DV)
    o = o * lax.rsqrt(jnp.mean(o * o, axis=-1, keepdims=True) + RMS_EPS) * norm_g
    o = o.reshape(bsz, seq, GLA_V) * jax.nn.silu(r.astype(jnp.float32))
    return o.astype(x.dtype) @ w_out


def shared_kv(x, w_kv):
    bsz, seq, _ = x.shape
    kv = (x @ w_kv).reshape(bsz, seq, 2, SB_HEADS, SB_DH)
    return kv[:, :, 0].transpose(0, 2, 1, 3), kv[:, :, 1].transpose(0, 2, 1, 3)


def stick_breaking_mixer(x, k, v, w_q, w_out):
    bsz, seq, _ = x.shape
    q = (x @ w_q).reshape(bsz, seq, SB_HEADS, SB_DH).transpose(0, 2, 1, 3)
    scale = SB_DH ** -0.5
    outs = []
    for blk in range(seq // SB_QBLOCK):
        lo, hi = blk * SB_QBLOCK, (blk + 1) * SB_QBLOCK
        kp, vp = k[:, :, :hi], v[:, :, :hi]
        z = jnp.einsum('bhqd,bhkd->bhqk', q[:, :, lo:hi], kp).astype(jnp.float32) * scale
        t_idx = lo + jnp.arange(SB_QBLOCK)[:, None]
        s_idx = jnp.arange(hi)[None, :]
        mask = s_idx < t_idx
        log_keep = jnp.where(mask, -jax.nn.softplus(z), 0.0)
        cum = jnp.cumsum(log_keep, axis=-1)
        log_w = jax.nn.log_sigmoid(z) + cum[..., -1:] - cum
        w = jnp.where(mask, jnp.exp(log_w), 0.0)
        outs.append(jnp.einsum('bhqk,bhkd->bhqd', w.astype(vp.dtype), vp))
    o = jnp.concatenate(outs, axis=2).transpose(0, 2, 1, 3).reshape(bsz, seq, SB_HEADS * SB_DH)
    return o @ w_out


def moe_ffn(x, w_r, b_r, w_gu, b_gu, w_dn, b_dn):
    bsz, seq, dm = x.shape
    n_tok = bsz * seq
    xf = x.reshape(n_tok, dm)
    logits = (xf @ w_r).astype(jnp.float32) + b_r.astype(jnp.float32)
    top_val, top_idx = lax.top_k(logits, TOP_K)
    gates = jax.nn.softmax(top_val, axis=-1)

    e_flat = top_idx.reshape(-1).astype(jnp.int32)
    tok_flat = jnp.repeat(jnp.arange(n_tok, dtype=jnp.int32), TOP_K)
    g_flat = gates.reshape(-1)
    order = jnp.argsort(e_flat)
    e_sorted, tok_sorted, g_sorted = e_flat[order], tok_flat[order], g_flat[order]

    counts = jnp.bincount(e_flat, length=N_EXPERTS).astype(jnp.int32)
    padded = ((counts + MOE_ROWS - 1) // MOE_ROWS) * MOE_ROWS
    start = jnp.cumsum(counts) - counts
    pend = jnp.cumsum(padded)
    pstart = pend - padded
    n_assign = n_tok * TOP_K
    rank = jnp.arange(n_assign, dtype=jnp.int32) - start[e_sorted]
    dest = pstart[e_sorted] + rank
    n_rows = n_assign + N_EXPERTS * MOE_ROWS
    n_blocks = n_rows // MOE_ROWS
    row_tok = jnp.full((n_rows,), n_tok, jnp.int32).at[dest].set(tok_sorted)
    row_gate = jnp.zeros((n_rows,), jnp.float32).at[dest].set(g_sorted)
    blk_expert = jnp.clip(jnp.searchsorted(pend, jnp.arange(n_blocks, dtype=jnp.int32) * MOE_ROWS,
                                           side='right'), 0, N_EXPERTS - 1)
    x_pad = jnp.concatenate([xf, jnp.zeros((1, dm), xf.dtype)], axis=0)

    def expert_block(args):
        e, toks, g = args
        h = x_pad[toks] @ w_gu[e] + b_gu[e]
        h_glu = jnp.minimum(h[:, :EXPERT_FF], SWIGLU_LIMIT)
        h_lin = jnp.clip(h[:, EXPERT_FF:], -SWIGLU_LIMIT, SWIGLU_LIMIT)
        act = h_glu * jax.nn.sigmoid(SWIGLU_ALPHA * h_glu) * (h_lin + 1.0)
        y = act @ w_dn[e] + b_dn[e]
        return y * g[:, None].astype(y.dtype)

    ys = lax.map(expert_block, (blk_expert, row_tok.reshape(n_blocks, MOE_ROWS),
                                row_gate.reshape(n_blocks, MOE_ROWS)))
    out = jnp.zeros((n_tok + 1, dm), ys.dtype).at[row_tok].add(ys.reshape(n_rows, dm))
    return out[:n_tok].reshape(bsz, seq, dm).astype(x.dtype)


def setup_inputs(seed: int = 0) -> dict:
    key = jax.random.key(seed)
    ks = jax.random.split(key, 24)
    f32 = jnp.float32
    dm = D_MODEL

    def nrm(k, shape, scale):
        return jax.random.normal(k, shape, f32) * scale

    gla_col_scale = jnp.concatenate([jnp.ones((2 * GLA_QK,), f32), jnp.full((GLA_V,), DN_BETA, f32),
                                     jnp.ones((GLA_V + GLA_GATE_RANK,), f32)])
    kv_col_scale = jnp.concatenate([jnp.ones((dm,), f32), jnp.full((dm,), DN_BETA, f32)])
    return {
        'x': nrm(ks[0], (BATCH, SEQ, dm), 1.0),
        'gla_w_in': nrm(ks[1], (N_A_LAYERS, dm, GLA_IN_COLS), dm ** -0.5) * gla_col_scale,
        'gla_w_gate2': nrm(ks[2], (N_A_LAYERS, GLA_GATE_RANK, GLA_QK), GLA_GATE_RANK ** -0.5),
        'gla_b_gate2': nrm(ks[3], (N_A_LAYERS, GLA_QK), 0.1),
        'gla_norm_g': 1.0 + nrm(ks[4], (N_A_LAYERS, GLA_DV), 0.02),
        'gla_w_out': nrm(ks[5], (N_A_LAYERS, GLA_V, dm), GLA_V ** -0.5 * DN_BETA),
        'sb_w_q': nrm(ks[6], (N_B_LAYERS, dm, SB_HEADS * SB_DH), dm ** -0.5),
        'sb_w_out': nrm(ks[7], (N_B_LAYERS, SB_HEADS * SB_DH, dm), (SB_HEADS * SB_DH) ** -0.5 * DN_BETA),
        'shared_w_kv': nrm(ks[8], (dm, 2 * dm), dm ** -0.5) * kv_col_scale,
        'router_w': nrm(ks[9], (DEPTH, dm, N_EXPERTS), dm ** -0.5),
        'router_b': nrm(ks[10], (DEPTH, N_EXPERTS), 0.01),
        'moe_w_gate_up': nrm(ks[11], (DEPTH, N_EXPERTS, dm, 2 * EXPERT_FF), dm ** -0.5),
        'moe_b_gate_up': nrm(ks[12], (DEPTH, N_EXPERTS, 2 * EXPERT_FF), 0.01),
        'moe_w_down': nrm(ks[13], (DEPTH, N_EXPERTS, EXPERT_FF, dm), EXPERT_FF ** -0.5 * DN_BETA),
        'moe_b_down': nrm(ks[14], (DEPTH, N_EXPERTS, dm), 0.01),
        'ln1_g': 1.0 + nrm(ks[15], (DEPTH, dm), 0.02),
        'ln1_b': nrm(ks[16], (DEPTH, dm), 0.02),
        'ln2_g': 1.0 + nrm(ks[17], (DEPTH, dm), 0.02),
        'ln2_b': nrm(ks[18], (DEPTH, dm), 0.02),
    }


def reference(x, gla_w_in, gla_w_gate2, gla_b_gate2, gla_norm_g, gla_w_out, sb_w_q, sb_w_out,
              shared_w_kv, router_w, router_b, moe_w_gate_up, moe_b_gate_up, moe_w_down, moe_b_down,
              ln1_g, ln1_b, ln2_g, ln2_b):
    k_sh, v_sh = None, None
    for layer in range(DEPTH):
        if layer < N_A_LAYERS:
            mix = gla_mixer(x, gla_w_in[layer], gla_w_gate2[layer], gla_b_gate2[layer],
                            gla_norm_g[layer], gla_w_out[layer])
        else:
            if layer == N_A_LAYERS:
                k_sh, v_sh = shared_kv(x, shared_w_kv)
            j = layer - N_A_LAYERS
            mix = stick_breaking_mixer(x, k_sh, v_sh, sb_w_q[j], sb_w_out[j])
        x = layer_norm(DN_ALPHA * x + mix, ln1_g[layer], ln1_b[layer])
        ffn = moe_ffn(x, router_w[layer], router_b[layer], moe_w_gate_up[layer], moe_b_gate_up[layer],
                      moe_w_down[layer], moe_b_down[layer])
        x = layer_norm(DN_ALPHA * x + ffn, ln2_g[layer], ln2_b[layer])
    return x
```

```python
import functools

import jax
import jax.numpy as jnp
from jax import lax
from jax.experimental import pallas as pl
from jax.experimental.pallas import tpu as pltpu

F32 = jnp.float32
BF16 = jnp.bfloat16

DEPTH = 2
CHUNK = 64
GLA_HEADS = 8
GLA_GATE_RANK = 16
GLA_TAU = 16.0
SB_HEADS = 32
N_EXPERTS = 32
TOP_K = 4
EXPERT_FF = 768
SWIGLU_LIMIT = 7.0
SWIGLU_ALPHA = 1.702
DN_ALPHA = (2.0 * DEPTH) ** 0.25
LN_EPS = 1e-5
RMS_EPS = 1e-6

V7X_LANES = 128
V7X_VMEM_LIMIT_BYTES = 56 * 1024 * 1024

MM_BM = 1024
MM_BN = 1024
LN_BM = 512
LN_BK = 512
GLA_TS = 256
SB_TQ = 256
ROUTER_BM = 512
MOE_TM = 256
MOE_FC = 384
COMBINE_TC = 128


def _params(semantics):
    return pltpu.CompilerParams(dimension_semantics=semantics,
                                vmem_limit_bytes=V7X_VMEM_LIMIT_BYTES)


def _mm_kernel(a_ref, w_ref, o_ref):
    o_ref[...] = jnp.dot(a_ref[...], w_ref[...],
                         preferred_element_type=F32).astype(o_ref.dtype)


def _matmul(a, w, out_dtype, bn=None):
    m, k = a.shape
    n = w.shape[1]
    bm = min(MM_BM, m)
    bn = min(bn or MM_BN, n)
    assert m % bm == 0 and n % bn == 0, (m, n, bm, bn)
    return pl.pallas_call(
        _mm_kernel,
        out_shape=jax.ShapeDtypeStruct((m, n), out_dtype),
        grid=(m // bm, n // bn),
        in_specs=[pl.BlockSpec((bm, k), lambda i, j: (i, 0)),
                  pl.BlockSpec((k, bn), lambda i, j: (0, j))],
        out_specs=pl.BlockSpec((bm, bn), lambda i, j: (i, j)),
        compiler_params=_params(("parallel", "parallel")),
        name="dense_matmul",
    )(a, w)


def _layer_norm_rows(y, g, b):
    mu = jnp.mean(y, axis=-1, keepdims=True)
    yc = y - mu
    var = jnp.mean(yc * yc, axis=-1, keepdims=True)
    return yc * lax.rsqrt(var + LN_EPS) * g + b


def _mm_ln_kernel(a_ref, w_ref, x_ref, g_ref, b_ref, o_ref, *, nk):
    k = pl.program_id(1)

    @pl.when(k == 0)
    def _():
        o_ref[...] = DN_ALPHA * x_ref[...]

    o_ref[...] += jnp.dot(a_ref[...], w_ref[...], preferred_element_type=F32)

    @pl.when(k == nk - 1)
    def _():
        o_ref[...] = _layer_norm_rows(o_ref[...], g_ref[...], b_ref[...])


def _matmul_ln(a, w, xres, g, b):
    m, k = a.shape
    n = w.shape[1]
    bm = min(LN_BM, m)
    bk = min(LN_BK, k)
    nk = k // bk
    return pl.pallas_call(
        functools.partial(_mm_ln_kernel, nk=nk),
        out_shape=jax.ShapeDtypeStruct((m, n), F32),
        grid=(m // bm, nk),
        in_specs=[pl.BlockSpec((bm, bk), lambda i, kk: (i, kk)),
                  pl.BlockSpec((bk, n), lambda i, kk: (kk, 0)),
                  pl.BlockSpec((bm, n), lambda i, kk: (i, 0)),
                  pl.BlockSpec((1, n), lambda i, kk: (0, 0)),
                  pl.BlockSpec((1, n), lambda i, kk: (0, 0))],
        out_specs=pl.BlockSpec((bm, n), lambda i, kk: (i, 0)),
        compiler_params=_params(("parallel", "arbitrary")),
        name="matmul_deepnorm",
    )(a, w, xres, g.reshape(1, n), b.reshape(1, n))


def _split_bf16(v):
    hi = v.astype(BF16)
    lo = (v - hi.astype(F32)).astype(BF16)
    return hi, lo


def _gla_kernel(q_ref, k_ref, v_ref, r_ref, gl_ref, wg_ref, bg_ref, ng_ref, o_ref, state_ref,
                *, ts, dk, dv):
    s = pl.program_id(2)

    @pl.when(s == 0)
    def _():
        state_ref[...] = jnp.zeros_like(state_ref)

    z = jnp.dot(gl_ref[...].astype(BF16), wg_ref[...], preferred_element_type=F32) + bg_ref[...]
    la = -(jnp.maximum(-z, 0.0) + jnp.log(1.0 + jnp.exp(-jnp.abs(z)))) * (1.0 / GLA_TAU)
    la_hi, la_lo = _split_bf16(la)

    row = lax.broadcasted_iota(jnp.int32, (ts, ts), 0)
    col = lax.broadcasted_iota(jnp.int32, (ts, ts), 1)
    shift = CHUNK.bit_length() - 1
    later = ((row >> shift) == (col >> shift)) & (col > row)
    later = jnp.where(later, 1.0, 0.0).astype(BF16)
    to_end = (jnp.dot(later, la_hi, preferred_element_type=F32)
              + jnp.dot(later, la_lo, preferred_element_type=F32))
    k_dec = (k_ref[...].astype(F32) * jnp.exp(to_end)).astype(BF16)

    ones = jnp.ones((CHUNK, dv), BF16)
    tn = (((0,), (0,)), ((), ()))
    for c in range(ts // CHUNK):
        sl = slice(c * CHUNK, (c + 1) * CHUNK)
        chunk_log = (lax.dot_general(la_hi[sl], ones, tn, preferred_element_type=F32)
                     + lax.dot_general(la_lo[sl], ones, tn, preferred_element_type=F32))
        kv = lax.dot_general(k_dec[sl], v_ref[sl, :], tn, preferred_element_type=F32)
        state = jnp.exp(chunk_log) * state_ref[...] + kv
        state_ref[...] = state
        o = jnp.dot(q_ref[sl, :], state.astype(BF16), preferred_element_type=F32) * (dk ** -0.5)
        o = o * lax.rsqrt(jnp.mean(o * o, axis=-1, keepdims=True) + RMS_EPS) * ng_ref[...]
        r = r_ref[sl, :].astype(F32)
        o_ref[sl, :] = (o * (r * jax.nn.sigmoid(r))).astype(o_ref.dtype)


def _gla_core(proj, glow, wg2, bg2, norm_g, bsz, seq):
    t = proj.shape[0]
    qk = wg2.shape[1]
    dk = qk // GLA_HEADS
    dv = norm_g.shape[0]
    ts = min(GLA_TS, seq)
    ns = seq // ts
    nh = GLA_HEADS
    v_blk0 = 2 * qk // dv
    r_blk0 = v_blk0 + nh
    gw = glow.shape[1]
    return pl.pallas_call(
        functools.partial(_gla_kernel, ts=ts, dk=dk, dv=dv),
        out_shape=jax.ShapeDtypeStruct((t, nh * dv), BF16),
        grid=(bsz, nh, ns),
        in_specs=[pl.BlockSpec((ts, dk), lambda b, h, s: (b * ns + s, h)),
                  pl.BlockSpec((ts, dk), lambda b, h, s: (b * ns + s, nh + h)),
                  pl.BlockSpec((ts, dv), lambda b, h, s: (b * ns + s, v_blk0 + h)),
                  pl.BlockSpec((ts, dv), lambda b, h, s: (b * ns + s, r_blk0 + h)),
                  pl.BlockSpec((ts, gw), lambda b, h, s: (b * ns + s, 0)),
                  pl.BlockSpec((gw, dk), lambda b, h, s: (0, h)),
                  pl.BlockSpec((1, dk), lambda b, h, s: (0, h)),
                  pl.BlockSpec((1, dv), lambda b, h, s: (0, 0))],
        out_specs=pl.BlockSpec((ts, dv), lambda b, h, s: (b * ns + s, h)),
        scratch_shapes=[pltpu.VMEM((dk, dv), F32)],
        compiler_params=_params(("parallel", "parallel", "arbitrary")),
        name="gla_chunk_scan",
    )(proj, proj, proj, proj, glow, wg2, bg2.reshape(1, qk), norm_g.reshape(1, dv))


def _gla_mixer(x_bf, w_in, w_gate2, b_gate2, norm_g, bsz, seq):
    qk = w_gate2.shape[1]
    n_main = w_in.shape[1] - GLA_GATE_RANK
    w_main = w_in[:, :n_main].astype(BF16)
    w_code = jnp.pad(w_in[:, n_main:], ((0, 0), (0, V7X_LANES - GLA_GATE_RANK))).astype(BF16)
    wg2 = jnp.pad(w_gate2, ((0, V7X_LANES - GLA_GATE_RANK), (0, 0))).astype(BF16)
    proj = _matmul(x_bf, w_main, BF16)
    glow = _matmul(x_bf, w_code, F32, bn=V7X_LANES)
    return _gla_core(proj, glow, wg2, b_gate2, norm_g, bsz, seq)


def _sb_kernel(q_ref, k_ref, v_ref, o_ref, *, tq, dh):
    i = pl.program_id(2)
    q = q_ref[...]
    scale = dh ** -0.5
    row = lax.broadcasted_iota(jnp.int32, (tq, tq), 0)
    col = lax.broadcasted_iota(jnp.int32, (tq, tq), 1)
    suffix = jnp.where(row >= col, 1.0, 0.0).astype(BF16)
    causal = col < row
    nt = (((1,), (1,)), ((), ()))

    def block(jb, carry, masked):
        right, acc = carry
        start = pl.multiple_of(jb * tq, tq)
        kb = k_ref[pl.ds(start, tq), :]
        vb = v_ref[pl.ds(start, tq), :]
        z = lax.dot_general(q, kb, nt, preferred_element_type=F32) * scale
        sp = jnp.maximum(z, 0.0) + jnp.log(1.0 + jnp.exp(-jnp.abs(z)))
        if masked:
            sp = jnp.where(causal, sp, 0.0)
        hi, lo = _split_bf16(sp)
        tail = (jnp.dot(hi, suffix, preferred_element_type=F32)
                + jnp.dot(lo, suffix, preferred_element_type=F32))
        w = jnp.exp(z - tail - right)
        if masked:
            w = jnp.where(causal, w, 0.0)
        acc = acc + jnp.dot(w.astype(BF16), vb, preferred_element_type=F32)
        right = right + jnp.sum(sp, axis=-1, keepdims=True)
        return right, acc

    carry = (jnp.zeros((tq, 1), F32), jnp.zeros((tq, dh), F32))
    carry = block(i, carry, True)
    carry = lax.fori_loop(0, i, lambda n, c: block(i - 1 - n, c, False), carry)
    o_ref[...] = carry[1].astype(o_ref.dtype)


def _sb_attention(qp, kvp, bsz, seq):
    t, d = qp.shape
    nh = SB_HEADS
    dh = d // nh
    tq = min(SB_TQ, seq)
    nq = seq // tq
    return pl.pallas_call(
        functools.partial(_sb_kernel, tq=tq, dh=dh),
        out_shape=jax.ShapeDtypeStruct((t, d), BF16),
        grid=(bsz, nh, nq),
        in_specs=[pl.BlockSpec((tq, dh), lambda b, h, i: (b * nq + i, h)),
                  pl.BlockSpec((seq, dh), lambda b, h, i: (b, h)),
                  pl.BlockSpec((seq, dh), lambda b, h, i: (b, nh + h))],
        out_specs=pl.BlockSpec((tq, dh), lambda b, h, i: (b * nq + i, h)),
        compiler_params=_params(("parallel", "parallel", "arbitrary")),
        name="stick_breaking_attention",
    )(qp, kvp, kvp)


def _router_kernel(x_ref, w_ref, b_ref, idx_ref, gate_ref):
    logits = jnp.dot(x_ref[...], w_ref[...], precision=lax.Precision.HIGHEST,
                     preferred_element_type=F32) + b_ref[...]
    n_e = logits.shape[-1]
    lane = lax.broadcasted_iota(jnp.int32, logits.shape, 1)
    vals = logits
    top_v, top_i = [], []
    for _ in range(TOP_K):
        m = jnp.max(vals, axis=-1, keepdims=True)
        sel = jnp.min(jnp.where(vals == m, lane, n_e), axis=-1, keepdims=True)
        top_v.append(m)
        top_i.append(sel)
        vals = jnp.where(lane == sel, -jnp.inf, vals)
    ex = [jnp.exp(v - top_v[0]) for v in top_v]
    denom = ex[0] + ex[1] + ex[2] + ex[3]
    for kk in range(TOP_K):
        idx_ref[:, kk:kk + 1] = top_i[kk]
        gate_ref[:, kk:kk + 1] = ex[kk] / denom


def _router(x, w_r, b_r):
    t, d = x.shape
    n_e = w_r.shape[1]
    bm = min(ROUTER_BM, t)
    return pl.pallas_call(
        _router_kernel,
        out_shape=(jax.ShapeDtypeStruct((t, TOP_K), jnp.int32),
                   jax.ShapeDtypeStruct((t, TOP_K), F32)),
        grid=(t // bm,),
        in_specs=[pl.BlockSpec((bm, d), lambda i: (i, 0)),
                  pl.BlockSpec((d, n_e), lambda i: (0, 0)),
                  pl.BlockSpec((1, n_e), lambda i: (0, 0))],
        out_specs=(pl.BlockSpec((bm, TOP_K), lambda i: (i, 0)),
                   pl.BlockSpec((bm, TOP_K), lambda i: (i, 0))),
        compiler_params=_params(("parallel",)),
        name="router_topk",
    )(x, w_r, b_r.reshape(1, n_e))


def _route_tables(idx, tm):
    t = idx.shape[0]
    n_e = N_EXPERTS
    hit = (idx[:, :, None] == jnp.arange(n_e, dtype=jnp.int32)[None, None, :]).any(axis=1)
    hit = hit.astype(jnp.int32)
    csum = jnp.cumsum(hit, axis=0)
    counts = csum[-1]
    padded = ((counts + tm - 1) // tm) * tm
    pend = jnp.cumsum(padded)
    pstart = pend - padded
    rank = jnp.take_along_axis(csum - hit, idx, axis=1)
    dest = (pstart[idx] + rank).astype(jnp.int32)
    n_rows = t * TOP_K + n_e * tm
    n_tiles = n_rows // tm
    tok = jnp.repeat(jnp.arange(t, dtype=jnp.int32), TOP_K)
    row_tok = jnp.zeros((n_rows,), jnp.int32).at[dest.reshape(-1)].set(tok)
    tile_expert = jnp.clip(jnp.searchsorted(pend, jnp.arange(n_tiles, dtype=jnp.int32) * tm,
                                            side='right'), 0, n_e - 1).astype(jnp.int32)
    n_used = (pend[-1:] // tm).astype(jnp.int32)
    return dest, row_tok, tile_expert, n_used, n_rows, n_tiles


def _dispatch_kernel(rt_ref, x_hbm, o_ref, buf, sem, *, tm, n_tiles):
    t = pl.program_id(0)

    def start(tile, slot):
        base = tile * tm

        def body(r, c):
            tok = rt_ref[base + r]
            pltpu.make_async_copy(x_hbm.at[pl.ds(tok, 1)], buf.at[slot, pl.ds(r, 1)],
                                  sem.at[slot]).start()
            return c

        lax.fori_loop(0, tm, body, 0)

    @pl.when(t == 0)
    def _():
        start(0, 0)

    @pl.when(t + 1 < n_tiles)
    def _():
        start(t + 1, (t + 1) % 2)

    slot = t % 2
    pltpu.make_async_copy(x_hbm.at[pl.ds(0, tm)], buf.at[slot], sem.at[slot]).wait()
    o_ref[...] = buf[slot].astype(o_ref.dtype)


def _dispatch(x, row_tok, tm, n_rows, n_tiles):
    d = x.shape[1]
    return pl.pallas_call(
        functools.partial(_dispatch_kernel, tm=tm, n_tiles=n_tiles),
        out_shape=jax.ShapeDtypeStruct((n_rows, d), BF16),
        grid_spec=pltpu.PrefetchScalarGridSpec(
            num_scalar_prefetch=1,
            grid=(n_tiles,),
            in_specs=[pl.BlockSpec(memory_space=pl.ANY)],
            out_specs=pl.BlockSpec((tm, d), lambda t, rt: (t, 0)),
            scratch_shapes=[pltpu.VMEM((2, tm, d), F32), pltpu.SemaphoreType.DMA((2,))]),
        compiler_params=_params(("arbitrary",)),
        name="moe_dispatch_gather",
    )(row_tok, x)


def _expert_changed(te_ref, t):
    return (t == 0) | (te_ref[t] != te_ref[jnp.maximum(t - 1, 0)])


def _moe_up_kernel(te_ref, nu_ref, x_ref, wg_ref, wu_ref, bg_ref, bu_ref, o_ref, wg_bf, wu_bf):
    t = pl.program_id(1)

    @pl.when(_expert_changed(te_ref, t))
    def _():
        wg_bf[...] = wg_ref[...].astype(BF16)
        wu_bf[...] = wu_ref[...].astype(BF16)

    @pl.when(t < nu_ref[0])
    def _():
        x = x_ref[...]
        h_glu = jnp.dot(x, wg_bf[...], preferred_element_type=F32) + bg_ref[...]
        h_lin = jnp.dot(x, wu_bf[...], preferred_element_type=F32) + bu_ref[...]
        h_glu = jnp.minimum(h_glu, SWIGLU_LIMIT)
        h_lin = jnp.clip(h_lin, -SWIGLU_LIMIT, SWIGLU_LIMIT)
        act = h_glu * jax.nn.sigmoid(SWIGLU_ALPHA * h_glu) * (h_lin + 1.0)
        o_ref[...] = act.astype(o_ref.dtype)

    @pl.when(t >= nu_ref[0])
    def _():
        o_ref[...] = jnp.zeros_like(o_ref)


def _moe_up(xs, w_gu, b_gu, tile_expert, n_used, tm, n_tiles):
    n_rows, d = xs.shape
    n_e, _, ff2 = w_gu.shape
    ff = ff2 // 2
    fc = min(MOE_FC, ff)
    nj = ff // fc

    def row_map(j, t, te, nu):
        return (jnp.minimum(t, nu[0] - 1), 0)

    return pl.pallas_call(
        _moe_up_kernel,
        out_shape=jax.ShapeDtypeStruct((n_rows, ff), BF16),
        grid_spec=pltpu.PrefetchScalarGridSpec(
            num_scalar_prefetch=2,
            grid=(nj, n_tiles),
            in_specs=[pl.BlockSpec((tm, d), row_map),
                      pl.BlockSpec((None, d, fc), lambda j, t, te, nu: (te[t], 0, j)),
                      pl.BlockSpec((None, d, fc), lambda j, t, te, nu: (te[t], 0, nj + j)),
                      pl.BlockSpec((None, 1, fc), lambda j, t, te, nu: (te[t], 0, j)),
                      pl.BlockSpec((None, 1, fc), lambda j, t, te, nu: (te[t], 0, nj + j))],
            out_specs=pl.BlockSpec((tm, fc), lambda j, t, te, nu: (t, j)),
            scratch_shapes=[pltpu.VMEM((d, fc), BF16), pltpu.VMEM((d, fc), BF16)]),
        compiler_params=_params(("arbitrary", "arbitrary")),
        name="moe_gate_up",
    )(tile_expert, n_used, xs, w_gu, w_gu, b_gu.reshape(n_e, 1, ff2), b_gu.reshape(n_e, 1, ff2))


def _moe_down_kernel(te_ref, nu_ref, a_ref, w_ref, b_ref, o_ref, w_bf):
    t = pl.program_id(0)

    @pl.when(_expert_changed(te_ref, t))
    def _():
        w_bf[...] = w_ref[...].astype(BF16)

    @pl.when(t < nu_ref[0])
    def _():
        o_ref[...] = jnp.dot(a_ref[...], w_bf[...], preferred_element_type=F32) + b_ref[...]

    @pl.when(t >= nu_ref[0])
    def _():
        o_ref[...] = jnp.zeros_like(o_ref)


def _moe_down(act, w_dn, b_dn, tile_expert, n_used, tm, n_tiles):
    n_rows, ff = act.shape
    n_e, _, d = w_dn.shape
    return pl.pallas_call(
        _moe_down_kernel,
        out_shape=jax.ShapeDtypeStruct((n_rows, d), F32),
        grid_spec=pltpu.PrefetchScalarGridSpec(
            num_scalar_prefetch=2,
            grid=(n_tiles,),
            in_specs=[pl.BlockSpec((tm, ff), lambda t, te, nu: (jnp.minimum(t, nu[0] - 1), 0)),
                      pl.BlockSpec((None, ff, d), lambda t, te, nu: (te[t], 0, 0)),
                      pl.BlockSpec((None, 1, d), lambda t, te, nu: (te[t], 0, 0))],
            out_specs=pl.BlockSpec((tm, d), lambda t, te, nu: (t, 0)),
            scratch_shapes=[pltpu.VMEM((ff, d), BF16)]),
        compiler_params=_params(("arbitrary",)),
        name="moe_down",
    )(tile_expert, n_used, act, w_dn, b_dn.reshape(n_e, 1, d))


def _combine_kernel(dest_ref, y_hbm, gate_ref, x_ref, g_ref, b_ref, of_ref, ob_ref, buf, sem,
                    *, tc, n_tiles):
    t = pl.program_id(0)

    def start(tile, slot):
        base = tile * tc * TOP_K

        def body(r, c):
            for kk in range(TOP_K):
                row = dest_ref[base + r * TOP_K + kk]
                pltpu.make_async_copy(y_hbm.at[pl.ds(row, 1)], buf.at[slot, kk, pl.ds(r, 1)],
                                      sem.at[slot]).start()
            return c

        lax.fori_loop(0, tc, body, 0)

    @pl.when(t == 0)
    def _():
        start(0, 0)

    @pl.when(t + 1 < n_tiles)
    def _():
        start(t + 1, (t + 1) % 2)

    slot = t % 2
    for kk in range(TOP_K):
        pltpu.make_async_copy(y_hbm.at[pl.ds(0, tc)], buf.at[slot, kk], sem.at[slot]).wait()
    gates = gate_ref[...]
    ffn = gates[:, 0:1] * buf[slot, 0]
    for kk in range(1, TOP_K):
        ffn = ffn + gates[:, kk:kk + 1] * buf[slot, kk]
    out = _layer_norm_rows(DN_ALPHA * x_ref[...] + ffn, g_ref[...], b_ref[...])
    of_ref[...] = out
    ob_ref[...] = out.astype(ob_ref.dtype)


def _combine_ln(y, dest, gates, xres, g, b):
    t, d = xres.shape
    tc = min(COMBINE_TC, t)
    n_tiles = t // tc
    return pl.pallas_call(
        functools.partial(_combine_kernel, tc=tc, n_tiles=n_tiles),
        out_shape=(jax.ShapeDtypeStruct((t, d), F32), jax.ShapeDtypeStruct((t, d), BF16)),
        grid_spec=pltpu.PrefetchScalarGridSpec(
            num_scalar_prefetch=1,
            grid=(n_tiles,),
            in_specs=[pl.BlockSpec(memory_space=pl.ANY),
                      pl.BlockSpec((tc, TOP_K), lambda i, ds: (i, 0)),
                      pl.BlockSpec((tc, d), lambda i, ds: (i, 0)),
                      pl.BlockSpec((1, d), lambda i, ds: (0, 0)),
                      pl.BlockSpec((1, d), lambda i, ds: (0, 0))],
            out_specs=(pl.BlockSpec((tc, d), lambda i, ds: (i, 0)),
                       pl.BlockSpec((tc, d), lambda i, ds: (i, 0))),
            scratch_shapes=[pltpu.VMEM((2, TOP_K, tc, d), F32), pltpu.SemaphoreType.DMA((2,))]),
        compiler_params=_params(("arbitrary",)),
        name="moe_combine_deepnorm",
    )(dest.reshape(-1), y, gates, xres, g.reshape(1, d), b.reshape(1, d))


def _moe_block(x, w_r, b_r, w_gu, b_gu, w_dn, b_dn, g, b):
    tm = MOE_TM
    idx, gates = _router(x, w_r, b_r)
    dest, row_tok, tile_expert, n_used, n_rows, n_tiles = _route_tables(idx, tm)
    xs = _dispatch(x, row_tok, tm, n_rows, n_tiles)
    act = _moe_up(xs, w_gu, b_gu, tile_expert, n_used, tm, n_tiles)
    y = _moe_down(act, w_dn, b_dn, tile_expert, n_used, tm, n_tiles)
    return _combine_ln(y, dest, gates, x, g, b)


def kernel(x, gla_w_in, gla_w_gate2, gla_b_gate2, gla_norm_g, gla_w_out, sb_w_q, sb_w_out,
           shared_w_kv, router_w, router_b, moe_w_gate_up, moe_b_gate_up, moe_w_down, moe_b_down,
           ln1_g, ln1_b, ln2_g, ln2_b):
    bsz, seq, d = x.shape
    n_a = DEPTH // 2
    xf = x.reshape(bsz * seq, d)
    xb = xf.astype(BF16)
    kvp = None
    for layer in range(DEPTH):
        if layer < n_a:
            mix = _gla_mixer(xb, gla_w_in[layer], gla_w_gate2[layer], gla_b_gate2[layer],
                             gla_norm_g[layer], bsz, seq)
            w_out = gla_w_out[layer]
        else:
            if layer == n_a:
                kvp = _matmul(xb, shared_w_kv.astype(BF16), BF16)
            j = layer - n_a
            qp = _matmul(xb, sb_w_q[j].astype(BF16), BF16)
            mix = _sb_attention(qp, kvp, bsz, seq)
            w_out = sb_w_out[j]
        xf = _matmul_ln(mix, w_out.astype(BF16), xf, ln1_g[layer], ln1_b[layer])
        xf, xb = _moe_block(xf, router_w[layer], router_b[layer], moe_w_gate_up[layer],
                            moe_b_gate_up[layer], moe_w_down[layer], moe_b_down[layer],
                            ln2_g[layer], ln2_b[layer])
    return xf.reshape(bsz, seq, d)
```

```python
import functools

import jax
import jax.numpy as jnp
from jax import lax
from jax.experimental import pallas as pl
from jax.experimental.pallas import tpu as pltpu

F32 = jnp.float32
BF16 = jnp.bfloat16

DEPTH = 2
CHUNK = 64
GLA_HEADS = 8
GLA_GATE_RANK = 16
GLA_TAU = 16.0
SB_HEADS = 32
N_EXPERTS = 32
TOP_K = 4
EXPERT_FF = 768
SWIGLU_LIMIT = 7.0
SWIGLU_ALPHA = 1.702
DN_ALPHA = (2.0 * DEPTH) ** 0.25
LN_EPS = 1e-5
RMS_EPS = 1e-6

V7X_LANES = 128
V7X_VMEM_LIMIT_BYTES = 56 * 1024 * 1024

MM_BM = 1024
MM_BN = 1024
LN_BM = 512
LN_BK = 512
GLA_TS = 256
SB_TQ = 256
SB_HB = 4
ROUTER_BM = 512
MOE_TM = 256
MOE_FC = 384
COMBINE_TC = 128


def _params(semantics):
    return pltpu.CompilerParams(dimension_semantics=semantics,
                                vmem_limit_bytes=V7X_VMEM_LIMIT_BYTES)


def _mm_kernel(a_ref, w_ref, o_ref):
    o_ref[...] = jnp.dot(a_ref[...], w_ref[...],
                         preferred_element_type=F32).astype(o_ref.dtype)


def _matmul(a, w, out_dtype, bn=None):
    m, k = a.shape
    n = w.shape[1]
    bm = min(MM_BM, m)
    bn = min(bn or MM_BN, n)
    assert m % bm == 0 and n % bn == 0, (m, n, bm, bn)
    return pl.pallas_call(
        _mm_kernel,
        out_shape=jax.ShapeDtypeStruct((m, n), out_dtype),
        grid=(m // bm, n // bn),
        in_specs=[pl.BlockSpec((bm, k), lambda i, j: (i, 0)),
                  pl.BlockSpec((k, bn), lambda i, j: (0, j))],
        out_specs=pl.BlockSpec((bm, bn), lambda i, j: (i, j)),
        compiler_params=_params(("parallel", "parallel")),
        name="dense_matmul",
    )(a, w)


def _layer_norm_rows(y, g, b):
    mu = jnp.mean(y, axis=-1, keepdims=True)
    yc = y - mu
    var = jnp.mean(yc * yc, axis=-1, keepdims=True)
    return yc * lax.rsqrt(var + LN_EPS) * g + b


def _mm_ln_kernel(a_ref, w_ref, x_ref, g_ref, b_ref, o_ref, *, nk):
    k = pl.program_id(1)

    @pl.when(k == 0)
    def _():
        o_ref[...] = DN_ALPHA * x_ref[...]

    o_ref[...] += jnp.dot(a_ref[...], w_ref[...], preferred_element_type=F32)

    @pl.when(k == nk - 1)
    def _():
        o_ref[...] = _layer_norm_rows(o_ref[...], g_ref[...], b_ref[...])


def _matmul_ln(a, w, xres, g, b):
    m, k = a.shape
    n = w.shape[1]
    bm = min(LN_BM, m)
    bk = min(LN_BK, k)
    nk = k // bk
    return pl.pallas_call(
        functools.partial(_mm_ln_kernel, nk=nk),
        out_shape=jax.ShapeDtypeStruct((m, n), F32),
        grid=(m // bm, nk),
        in_specs=[pl.BlockSpec((bm, bk), lambda i, kk: (i, kk)),
                  pl.BlockSpec((bk, n), lambda i, kk: (kk, 0)),
                  pl.BlockSpec((bm, n), lambda i, kk: (i, 0)),
                  pl.BlockSpec((1, n), lambda i, kk: (0, 0)),
                  pl.BlockSpec((1, n), lambda i, kk: (0, 0))],
        out_specs=pl.BlockSpec((bm, n), lambda i, kk: (i, 0)),
        compiler_params=_params(("parallel", "arbitrary")),
        name="matmul_deepnorm",
    )(a, w, xres, g.reshape(1, n), b.reshape(1, n))


def _split_bf16(v):
    hi = v.astype(BF16)
    lo = (v - hi.astype(F32)).astype(BF16)
    return hi, lo


def _gla_kernel(q_ref, k_ref, v_ref, r_ref, gl_ref, wg_ref, bg_ref, ng_ref, o_ref, state_ref,
                *, ts, dk, dv):
    s = pl.program_id(2)

    @pl.when(s == 0)
    def _():
        state_ref[...] = jnp.zeros_like(state_ref)

    z = jnp.dot(gl_ref[...].astype(BF16), wg_ref[...], preferred_element_type=F32) + bg_ref[...]
    la = -(jnp.maximum(-z, 0.0) + jnp.log(1.0 + jnp.exp(-jnp.abs(z)))) * (1.0 / GLA_TAU)
    la_hi, la_lo = _split_bf16(la)

    row = lax.broadcasted_iota(jnp.int32, (ts, ts), 0)
    col = lax.broadcasted_iota(jnp.int32, (ts, ts), 1)
    shift = CHUNK.bit_length() - 1
    later = ((row >> shift) == (col >> shift)) & (col > row)
    later = jnp.where(later, 1.0, 0.0).astype(BF16)
    to_end = (jnp.dot(later, la_hi, preferred_element_type=F32)
              + jnp.dot(later, la_lo, preferred_element_type=F32))
    k_dec = (k_ref[...].astype(F32) * jnp.exp(to_end)).astype(BF16)

    ones = jnp.ones((CHUNK, dv), BF16)
    tn = (((0,), (0,)), ((), ()))
    for c in range(ts // CHUNK):
        sl = slice(c * CHUNK, (c + 1) * CHUNK)
        chunk_log = (lax.dot_general(la_hi[sl], ones, tn, preferred_element_type=F32)
                     + lax.dot_general(la_lo[sl], ones, tn, preferred_element_type=F32))
        kv = lax.dot_general(k_dec[sl], v_ref[sl, :], tn, preferred_element_type=F32)
        state = jnp.exp(chunk_log) * state_ref[...] + kv
        state_ref[...] = state
        o = jnp.dot(q_ref[sl, :], state.astype(BF16), preferred_element_type=F32) * (dk ** -0.5)
        o = o * lax.rsqrt(jnp.mean(o * o, axis=-1, keepdims=True) + RMS_EPS) * ng_ref[...]
        r = r_ref[sl, :].astype(F32)
        o_ref[sl, :] = (o * (r * jax.nn.sigmoid(r))).astype(o_ref.dtype)


def _gla_core(proj, glow, wg2, bg2, norm_g, bsz, seq):
    t = proj.shape[0]
    qk = wg2.shape[1]
    dk = qk // GLA_HEADS
    dv = norm_g.shape[0]
    ts = min(GLA_TS, seq)
    ns = seq // ts
    nh = GLA_HEADS
    v_blk0 = 2 * qk // dv
    r_blk0 = v_blk0 + nh
    gw = glow.shape[1]
    return pl.pallas_call(
        functools.partial(_gla_kernel, ts=ts, dk=dk, dv=dv),
        out_shape=jax.ShapeDtypeStruct((t, nh * dv), BF16),
        grid=(bsz, nh, ns),
        in_specs=[pl.BlockSpec((ts, dk), lambda b, h, s: (b * ns + s, h)),
                  pl.BlockSpec((ts, dk), lambda b, h, s: (b * ns + s, nh + h)),
                  pl.BlockSpec((ts, dv), lambda b, h, s: (b * ns + s, v_blk0 + h)),
                  pl.BlockSpec((ts, dv), lambda b, h, s: (b * ns + s, r_blk0 + h)),
                  pl.BlockSpec((ts, gw), lambda b, h, s: (b * ns + s, 0)),
                  pl.BlockSpec((gw, dk), lambda b, h, s: (0, h)),
                  pl.BlockSpec((1, dk), lambda b, h, s: (0, h)),
                  pl.BlockSpec((1, dv), lambda b, h, s: (0, 0))],
        out_specs=pl.BlockSpec((ts, dv), lambda b, h, s: (b * ns + s, h)),
        scratch_shapes=[pltpu.VMEM((dk, dv), F32)],
        compiler_params=_params(("parallel", "parallel", "arbitrary")),
        name="gla_chunk_scan",
    )(proj, proj, proj, proj, glow, wg2, bg2.reshape(1, qk), norm_g.reshape(1, dv))


def _gla_mixer(x_bf, w_in, w_gate2, b_gate2, norm_g, bsz, seq):
    qk = w_gate2.shape[1]
    n_main = w_in.shape[1] - GLA_GATE_RANK
    w_main = w_in[:, :n_main].astype(BF16)
    w_code = jnp.pad(w_in[:, n_main:], ((0, 0), (0, V7X_LANES - GLA_GATE_RANK))).astype(BF16)
    wg2 = jnp.pad(w_gate2, ((0, V7X_LANES - GLA_GATE_RANK), (0, 0))).astype(BF16)
    proj = _matmul(x_bf, w_main, BF16)
    glow = _matmul(x_bf, w_code, F32, bn=V7X_LANES)
    return _gla_core(proj, glow, wg2, b_gate2, norm_g, bsz, seq)


def _sb_kernel(q_ref, k_ref, v_ref, o_ref, *, tq, dh, hb):
    i = pl.program_id(2)
    scale = dh ** -0.5
    row = lax.broadcasted_iota(jnp.int32, (tq, tq), 0)
    col = lax.broadcasted_iota(jnp.int32, (tq, tq), 1)
    suffix = jnp.where(row >= col, 1.0, 0.0).astype(BF16)
    causal = col < row
    nt = (((1,), (1,)), ((), ()))

    def block(jb, carry, masked):
        start = pl.multiple_of(jb * tq, tq)
        heads = [slice(h * dh, (h + 1) * dh) for h in range(hb)]
        zs = [lax.dot_general(q_ref[:, c], k_ref[pl.ds(start, tq), c], nt,
                              preferred_element_type=F32) * scale for c in heads]
        sps, tails = [], []
        for z in zs:
            sp = jnp.maximum(z, 0.0) + jnp.log(1.0 + jnp.exp(-jnp.abs(z)))
            if masked:
                sp = jnp.where(causal, sp, 0.0)
            hi, lo = _split_bf16(sp)
            sps.append(sp)
            tails.append(jnp.dot(hi, suffix, preferred_element_type=F32)
                         + jnp.dot(lo, suffix, preferred_element_type=F32))
        out = []
        for h, c in enumerate(heads):
            right, acc = carry[h]
            w = jnp.exp(zs[h] - tails[h] - right)
            if masked:
                w = jnp.where(causal, w, 0.0)
            acc = acc + jnp.dot(w.astype(BF16), v_ref[pl.ds(start, tq), c],
                                preferred_element_type=F32)
            right = right + jnp.sum(sps[h], axis=-1, keepdims=True)
            out.append((right, acc))
        return tuple(out)

    carry = tuple((jnp.zeros((tq, 1), F32), jnp.zeros((tq, dh), F32)) for _ in range(hb))
    carry = block(i, carry, True)
    carry = lax.fori_loop(0, i, lambda n, c: block(i - 1 - n, c, False), carry)
    for h in range(hb):
        o_ref[:, h * dh:(h + 1) * dh] = carry[h][1].astype(o_ref.dtype)


def _sb_attention(qp, kvp, bsz, seq):
    t, d = qp.shape
    nh = SB_HEADS
    dh = d // nh
    hb = SB_HB
    ng = nh // hb
    tq = min(SB_TQ, seq)
    nq = seq // tq
    return pl.pallas_call(
        functools.partial(_sb_kernel, tq=tq, dh=dh, hb=hb),
        out_shape=jax.ShapeDtypeStruct((t, d), BF16),
        grid=(bsz, ng, nq),
        in_specs=[pl.BlockSpec((tq, hb * dh), lambda b, g, i: (b * nq + i, g)),
                  pl.BlockSpec((seq, hb * dh), lambda b, g, i: (b, g)),
                  pl.BlockSpec((seq, hb * dh), lambda b, g, i: (b, ng + g))],
        out_specs=pl.BlockSpec((tq, hb * dh), lambda b, g, i: (b * nq + i, g)),
        compiler_params=_params(("parallel", "parallel", "arbitrary")),
        name="stick_breaking_attention",
    )(qp, kvp, kvp)


def _router_kernel(x_ref, w_ref, b_ref, idx_ref, gate_ref):
    logits = jnp.dot(x_ref[...], w_ref[...], precision=lax.Precision.HIGHEST,
                     preferred_element_type=F32) + b_ref[...]
    n_e = logits.shape[-1]
    lane = lax.broadcasted_iota(jnp.int32, logits.shape, 1)
    vals = logits
    top_v, top_i = [], []
    for _ in range(TOP_K):
        m = jnp.max(vals, axis=-1, keepdims=True)
        sel = jnp.min(jnp.where(vals == m, lane, n_e), axis=-1, keepdims=True)
        top_v.append(m)
        top_i.append(sel)
        vals = jnp.where(lane == sel, -jnp.inf, vals)
    ex = [jnp.exp(v - top_v[0]) for v in top_v]
    denom = ex[0] + ex[1] + ex[2] + ex[3]
    for kk in range(TOP_K):
        idx_ref[:, kk:kk + 1] = top_i[kk]
        gate_ref[:, kk:kk + 1] = ex[kk] / denom


def _router(x, w_r, b_r):
    t, d = x.shape
    n_e = w_r.shape[1]
    bm = min(ROUTER_BM, t)
    return pl.pallas_call(
        _router_kernel,
        out_shape=(jax.ShapeDtypeStruct((t, TOP_K), jnp.int32),
                   jax.ShapeDtypeStruct((t, TOP_K), F32)),
        grid=(t // bm,),
        in_specs=[pl.BlockSpec((bm, d), lambda i: (i, 0)),
                  pl.BlockSpec((d, n_e), lambda i: (0, 0)),
                  pl.BlockSpec((1, n_e), lambda i: (0, 0))],
        out_specs=(pl.BlockSpec((bm, TOP_K), lambda i: (i, 0)),
                   pl.BlockSpec((bm, TOP_K), lambda i: (i, 0))),
        compiler_params=_params(("parallel",)),
        name="router_topk",
    )(x, w_r, b_r.reshape(1, n_e))


def _route_tables(idx, tm):
    t = idx.shape[0]
    n_e = N_EXPERTS
    hit = (idx[:, :, None] == jnp.arange(n_e, dtype=jnp.int32)[None, None, :]).any(axis=1)
    hit = hit.astype(jnp.int32)
    csum = jnp.cumsum(hit, axis=0)
    counts = csum[-1]
    padded = ((counts + tm - 1) // tm) * tm
    pend = jnp.cumsum(padded)
    pstart = pend - padded
    rank = jnp.take_along_axis(csum - hit, idx, axis=1)
    dest = (pstart[idx] + rank).astype(jnp.int32)
    n_rows = t * TOP_K + n_e * tm
    n_tiles = n_rows // tm
    tok = jnp.repeat(jnp.arange(t, dtype=jnp.int32), TOP_K)
    row_tok = (jnp.arange(n_rows, dtype=jnp.int32) % t).at[dest.reshape(-1)].set(tok)
    tile_expert = jnp.clip(jnp.searchsorted(pend, jnp.arange(n_tiles, dtype=jnp.int32) * tm,
                                            side='right'), 0, n_e - 1).astype(jnp.int32)
    n_used = (pend[-1:] // tm).astype(jnp.int32)
    return dest, row_tok, tile_expert, n_used, n_rows, n_tiles


def _dispatch_kernel(rt_ref, nu_ref, x_hbm, o_ref, buf, sem, *, tm):
    t = pl.program_id(0)
    n_used = nu_ref[0]

    def start(tile, slot):
        base = tile * tm

        def body(r, c):
            tok = rt_ref[base + r]
            pltpu.make_async_copy(x_hbm.at[pl.ds(tok, 1)], buf.at[slot, pl.ds(r, 1)],
                                  sem.at[slot]).start()
            return c

        lax.fori_loop(0, tm, body, 0)

    @pl.when(t == 0)
    def _():
        start(0, 0)

    @pl.when(t + 1 < n_used)
    def _():
        start(t + 1, (t + 1) % 2)

    @pl.when(t < n_used)
    def _():
        slot = t % 2
        pltpu.make_async_copy(x_hbm.at[pl.ds(0, tm)], buf.at[slot], sem.at[slot]).wait()
        o_ref[...] = buf[slot].astype(o_ref.dtype)

    @pl.when(t >= n_used)
    def _():
        o_ref[...] = jnp.zeros_like(o_ref)


def _dispatch(x, row_tok, n_used, tm, n_rows, n_tiles):
    d = x.shape[1]
    return pl.pallas_call(
        functools.partial(_dispatch_kernel, tm=tm),
        out_shape=jax.ShapeDtypeStruct((n_rows, d), BF16),
        grid_spec=pltpu.PrefetchScalarGridSpec(
            num_scalar_prefetch=2,
            grid=(n_tiles,),
            in_specs=[pl.BlockSpec(memory_space=pl.ANY)],
            out_specs=pl.BlockSpec((tm, d), lambda t, rt, nu: (t, 0)),
            scratch_shapes=[pltpu.VMEM((2, tm, d), F32), pltpu.SemaphoreType.DMA((2,))]),
        compiler_params=_params(("arbitrary",)),
        name="moe_dispatch_gather",
    )(row_tok, n_used, x)


def _expert_changed(te_ref, t):
    return (t == 0) | (te_ref[t] != te_ref[jnp.maximum(t - 1, 0)])


def _moe_up_kernel(te_ref, nu_ref, x_ref, wg_ref, wu_ref, bg_ref, bu_ref, o_ref, wg_bf, wu_bf):
    t = pl.program_id(1)

    @pl.when(_expert_changed(te_ref, t))
    def _():
        wg_bf[...] = wg_ref[...].astype(BF16)
        wu_bf[...] = wu_ref[...].astype(BF16)

    @pl.when(t < nu_ref[0])
    def _():
        x = x_ref[...]
        h_glu = jnp.dot(x, wg_bf[...], preferred_element_type=F32) + bg_ref[...]
        h_lin = jnp.dot(x, wu_bf[...], preferred_element_type=F32) + bu_ref[...]
        h_glu = jnp.minimum(h_glu, SWIGLU_LIMIT)
        h_lin = jnp.clip(h_lin, -SWIGLU_LIMIT, SWIGLU_LIMIT)
        act = h_glu * jax.nn.sigmoid(SWIGLU_ALPHA * h_glu) * (h_lin + 1.0)
        o_ref[...] = act.astype(o_ref.dtype)

    @pl.when(t >= nu_ref[0])
    def _():
        o_ref[...] = jnp.zeros_like(o_ref)


def _moe_up(xs, w_gu, b_gu, layer, tile_expert, n_used, tm, n_tiles):
    n_rows, d = xs.shape
    n_l, n_e, _, ff2 = w_gu.shape
    ff = ff2 // 2
    fc = min(MOE_FC, ff)
    nj = ff // fc

    def row_map(j, t, te, nu):
        return (jnp.minimum(t, nu[0] - 1), 0)

    return pl.pallas_call(
        _moe_up_kernel,
        out_shape=jax.ShapeDtypeStruct((n_rows, ff), BF16),
        grid_spec=pltpu.PrefetchScalarGridSpec(
            num_scalar_prefetch=2,
            grid=(nj, n_tiles),
            in_specs=[pl.BlockSpec((tm, d), row_map),
                      pl.BlockSpec((None, None, d, fc), lambda j, t, te, nu: (layer, te[t], 0, j)),
                      pl.BlockSpec((None, None, d, fc),
                                   lambda j, t, te, nu: (layer, te[t], 0, nj + j)),
                      pl.BlockSpec((None, None, 1, fc), lambda j, t, te, nu: (layer, te[t], 0, j)),
                      pl.BlockSpec((None, None, 1, fc),
                                   lambda j, t, te, nu: (layer, te[t], 0, nj + j))],
            out_specs=pl.BlockSpec((tm, fc), lambda j, t, te, nu: (t, j)),
            scratch_shapes=[pltpu.VMEM((d, fc), BF16), pltpu.VMEM((d, fc), BF16)]),
        compiler_params=_params(("arbitrary", "arbitrary")),
        name="moe_gate_up",
    )(tile_expert, n_used, xs, w_gu, w_gu, b_gu.reshape(n_l, n_e, 1, ff2),
      b_gu.reshape(n_l, n_e, 1, ff2))


def _moe_down_kernel(te_ref, nu_ref, a_ref, w_ref, b_ref, o_ref, w_bf):
    t = pl.program_id(0)

    @pl.when(_expert_changed(te_ref, t))
    def _():
        w_bf[...] = w_ref[...].astype(BF16)

    @pl.when(t < nu_ref[0])
    def _():
        o_ref[...] = jnp.dot(a_ref[...], w_bf[...], preferred_element_type=F32) + b_ref[...]

    @pl.when(t >= nu_ref[0])
    def _():
        o_ref[...] = jnp.zeros_like(o_ref)


def _moe_down(act, w_dn, b_dn, layer, tile_expert, n_used, tm, n_tiles):
    n_rows, ff = act.shape
    n_l, n_e, _, d = w_dn.shape
    return pl.pallas_call(
        _moe_down_kernel,
        out_shape=jax.ShapeDtypeStruct((n_rows, d), F32),
        grid_spec=pltpu.PrefetchScalarGridSpec(
            num_scalar_prefetch=2,
            grid=(n_tiles,),
            in_specs=[pl.BlockSpec((tm, ff), lambda t, te, nu: (jnp.minimum(t, nu[0] - 1), 0)),
                      pl.BlockSpec((None, None, ff, d), lambda t, te, nu: (layer, te[t], 0, 0)),
                      pl.BlockSpec((None, None, 1, d), lambda t, te, nu: (layer, te[t], 0, 0))],
            out_specs=pl.BlockSpec((tm, d), lambda t, te, nu: (t, 0)),
            scratch_shapes=[pltpu.VMEM((ff, d), BF16)]),
        compiler_params=_params(("arbitrary",)),
        name="moe_down",
    )(tile_expert, n_used, act, w_dn, b_dn.reshape(n_l, n_e, 1, d))


def _combine_kernel(dest_ref, y_hbm, gate_ref, x_ref, g_ref, b_ref, of_ref, ob_ref, buf, sem,
                    *, tc, n_tiles):
    t = pl.program_id(0)

    def start(tile, slot):
        base = tile * tc * TOP_K

        def body(r, c):
            for kk in range(TOP_K):
                row = dest_ref[base + r * TOP_K + kk]
                pltpu.make_async_copy(y_hbm.at[pl.ds(row, 1)], buf.at[slot, kk, pl.ds(r, 1)],
                                      sem.at[slot]).start()
            return c

        lax.fori_loop(0, tc, body, 0)

    @pl.when(t == 0)
    def _():
        start(0, 0)

    @pl.when(t + 1 < n_tiles)
    def _():
        start(t + 1, (t + 1) % 2)

    slot = t % 2
    for kk in range(TOP_K):
        pltpu.make_async_copy(y_hbm.at[pl.ds(0, tc)], buf.at[slot, kk], sem.at[slot]).wait()
    gates = gate_ref[...]
    ffn = gates[:, 0:1] * buf[slot, 0]
    for kk in range(1, TOP_K):
        ffn = ffn + gates[:, kk:kk + 1] * buf[slot, kk]
    out = _layer_norm_rows(DN_ALPHA * x_ref[...] + ffn, g_ref[...], b_ref[...])
    of_ref[...] = out
    ob_ref[...] = out.astype(ob_ref.dtype)


def _combine_ln(y, dest, gates, xres, g, b):
    t, d = xres.shape
    tc = min(COMBINE_TC, t)
    n_tiles = t // tc
    return pl.pallas_call(
        functools.partial(_combine_kernel, tc=tc, n_tiles=n_tiles),
        out_shape=(jax.ShapeDtypeStruct((t, d), F32), jax.ShapeDtypeStruct((t, d), BF16)),
        grid_spec=pltpu.PrefetchScalarGridSpec(
            num_scalar_prefetch=1,
            grid=(n_tiles,),
            in_specs=[pl.BlockSpec(memory_space=pl.ANY),
                      pl.BlockSpec((tc, TOP_K), lambda i, ds: (i, 0)),
                      pl.BlockSpec((tc, d), lambda i, ds: (i, 0)),
                      pl.BlockSpec((1, d), lambda i, ds: (0, 0)),
                      pl.BlockSpec((1, d), lambda i, ds: (0, 0))],
            out_specs=(pl.BlockSpec((tc, d), lambda i, ds: (i, 0)),
                       pl.BlockSpec((tc, d), lambda i, ds: (i, 0))),
            scratch_shapes=[pltpu.VMEM((2, TOP_K, tc, d), F32), pltpu.SemaphoreType.DMA((2,))]),
        compiler_params=_params(("arbitrary",)),
        name="moe_combine_deepnorm",
    )(dest.reshape(-1), y, gates, xres, g.reshape(1, d), b.reshape(1, d))


def _moe_block(x, w_r, b_r, w_gu, b_gu, w_dn, b_dn, layer, g, b):
    tm = MOE_TM
    idx, gates = _router(x, w_r, b_r)
    dest, row_tok, tile_expert, n_used, n_rows, n_tiles = _route_tables(idx, tm)
    xs = _dispatch(x, row_tok, n_used, tm, n_rows, n_tiles)
    act = _moe_up(xs, w_gu, b_gu, layer, tile_expert, n_used, tm, n_tiles)
    y = _moe_down(act, w_dn, b_dn, layer, tile_expert, n_used, tm, n_tiles)
    return _combine_ln(y, dest, gates, x, g, b)


def kernel(x, gla_w_in, gla_w_gate2, gla_b_gate2, gla_norm_g, gla_w_out, sb_w_q, sb_w_out,
           shared_w_kv, router_w, router_b, moe_w_gate_up, moe_b_gate_up, moe_w_down, moe_b_down,
           ln1_g, ln1_b, ln2_g, ln2_b):
    bsz, seq, d = x.shape
    n_a = DEPTH // 2
    xf = x.reshape(bsz * seq, d)
    xb = xf.astype(BF16)
    kvp = None
    for layer in range(DEPTH):
        if layer < n_a:
            mix = _gla_mixer(xb, gla_w_in[layer], gla_w_gate2[layer], gla_b_gate2[layer],
                             gla_norm_g[layer], bsz, seq)
            w_out = gla_w_out[layer]
        else:
            if layer == n_a:
                kvp = _matmul(xb, shared_w_kv.astype(BF16), BF16)
            j = layer - n_a
            qp = _matmul(xb, sb_w_q[j].astype(BF16), BF16)
            mix = _sb_attention(qp, kvp, bsz, seq)
            w_out = sb_w_out[j]
        xf = _matmul_ln(mix, w_out.astype(BF16), xf, ln1_g[layer], ln1_b[layer])
        xf, xb = _moe_block(xf, router_w[layer], router_b[layer], moe_w_gate_up, moe_b_gate_up,
                            moe_w_down, moe_b_down, layer, ln2_g[layer], ln2_b[layer])
    return xf.reshape(bsz, seq, d)
```

```python
import functools

import jax
import jax.numpy as jnp
from jax import lax
from jax.experimental import pallas as pl
from jax.experimental.pallas import tpu as pltpu

F32 = jnp.float32
BF16 = jnp.bfloat16

DEPTH = 2
CHUNK = 64
GLA_HEADS = 8
GLA_GATE_RANK = 16
GLA_TAU = 16.0
SB_HEADS = 32
N_EXPERTS = 32
TOP_K = 4
EXPERT_FF = 768
SWIGLU_LIMIT = 7.0
SWIGLU_ALPHA = 1.702
DN_ALPHA = (2.0 * DEPTH) ** 0.25
LN_EPS = 1e-5
RMS_EPS = 1e-6
LOG2_E = 1.4426950408889634

V7X_LANES = 128
V7X_VMEM_LIMIT_BYTES = 56 * 1024 * 1024

MM_BM = 1024
MM_BN = 1024
MMW_BN = 512
LN_BM = 512
LN_BK = 512
GLA_TS = 256
GLA_HB = 4
SB_TQ = 256
SB_HB = 8
ROUTER_BM = 512
MOE_TM = 256
MOE_FC = 384
COMBINE_TC = 128


def _params(semantics):
    return pltpu.CompilerParams(dimension_semantics=semantics,
                                vmem_limit_bytes=V7X_VMEM_LIMIT_BYTES)


def _mm_kernel(a_ref, w_ref, o_ref):
    o_ref[...] = jnp.dot(a_ref[...], w_ref[...],
                         preferred_element_type=F32).astype(o_ref.dtype)


def _matmul(a, w, out_dtype, bn=None):
    m, k = a.shape
    n = w.shape[1]
    bm = min(MM_BM, m)
    bn = min(bn or MM_BN, n)
    assert m % bm == 0 and n % bn == 0, (m, n, bm, bn)
    return pl.pallas_call(
        _mm_kernel,
        out_shape=jax.ShapeDtypeStruct((m, n), out_dtype),
        grid=(m // bm, n // bn),
        in_specs=[pl.BlockSpec((bm, k), lambda i, j: (i, 0)),
                  pl.BlockSpec((k, bn), lambda i, j: (0, j))],
        out_specs=pl.BlockSpec((bm, bn), lambda i, j: (i, j)),
        compiler_params=_params(("parallel", "parallel")),
        name="dense_matmul",
    )(a, w)


def _mm_f32w_kernel(a_ref, w_ref, o_ref, w_bf):
    @pl.when(pl.program_id(1) == 0)
    def _():
        w_bf[...] = w_ref[...].astype(BF16)

    o_ref[...] = jnp.dot(a_ref[...], w_bf[...],
                         preferred_element_type=F32).astype(o_ref.dtype)


def _matmul_f32w(a, w, n, out_dtype):
    m, k = a.shape
    bm = min(MM_BM, m)
    bn = min(MMW_BN, n)
    assert m % bm == 0 and n % bn == 0, (m, n, bm, bn)
    return pl.pallas_call(
        _mm_f32w_kernel,
        out_shape=jax.ShapeDtypeStruct((m, n), out_dtype),
        grid=(n // bn, m // bm),
        in_specs=[pl.BlockSpec((bm, k), lambda j, i: (i, 0)),
                  pl.BlockSpec((k, bn), lambda j, i: (0, j))],
        out_specs=pl.BlockSpec((bm, bn), lambda j, i: (i, j)),
        scratch_shapes=[pltpu.VMEM((k, bn), BF16)],
        compiler_params=_params(("arbitrary", "arbitrary")),
        name="dense_matmul_f32w",
    )(a, w)


def _layer_norm_rows(y, g, b):
    mu = jnp.mean(y, axis=-1, keepdims=True)
    yc = y - mu
    var = jnp.mean(yc * yc, axis=-1, keepdims=True)
    return yc * lax.rsqrt(var + LN_EPS) * g + b


def _mm_ln_kernel(a_ref, w_ref, x_ref, g_ref, b_ref, o_ref, *, nk):
    k = pl.program_id(1)

    @pl.when(k == 0)
    def _():
        o_ref[...] = DN_ALPHA * x_ref[...]

    o_ref[...] += jnp.dot(a_ref[...], w_ref[...], preferred_element_type=F32)

    @pl.when(k == nk - 1)
    def _():
        o_ref[...] = _layer_norm_rows(o_ref[...], g_ref[...], b_ref[...])


def _matmul_ln(a, w, xres, g, b):
    m, k = a.shape
    n = w.shape[1]
    bm = min(LN_BM, m)
    bk = min(LN_BK, k)
    nk = k // bk
    return pl.pallas_call(
        functools.partial(_mm_ln_kernel, nk=nk),
        out_shape=jax.ShapeDtypeStruct((m, n), F32),
        grid=(m // bm, nk),
        in_specs=[pl.BlockSpec((bm, bk), lambda i, kk: (i, kk)),
                  pl.BlockSpec((bk, n), lambda i, kk: (kk, 0)),
                  pl.BlockSpec((bm, n), lambda i, kk: (i, 0)),
                  pl.BlockSpec((1, n), lambda i, kk: (0, 0)),
                  pl.BlockSpec((1, n), lambda i, kk: (0, 0))],
        out_specs=pl.BlockSpec((bm, n), lambda i, kk: (i, 0)),
        compiler_params=_params(("parallel", "arbitrary")),
        name="matmul_deepnorm",
    )(a, w, xres, g.reshape(1, n), b.reshape(1, n))


def _gla_kernel(q_ref, k_ref, v_ref, r_ref, gl_ref, wg_ref, bg_ref, ng_ref, o_ref, state_ref,
                *, ts, dk, dv, hb):
    s = pl.program_id(2)

    @pl.when(s == 0)
    def _():
        state_ref[...] = jnp.zeros_like(state_ref)

    row = lax.broadcasted_iota(jnp.int32, (ts, ts), 0)
    col = lax.broadcasted_iota(jnp.int32, (ts, ts), 1)
    shift = CHUNK.bit_length() - 1
    later = ((row >> shift) == (col >> shift)) & (col > row)
    later = jnp.where(later, 1.0, 0.0).astype(BF16)
    lanes = min(dv, V7X_LANES)
    ones = jnp.ones((CHUNK, lanes), BF16)
    tn = (((0,), (0,)), ((), ()))
    code = gl_ref[...].astype(BF16)

    las, k_dec = [], []
    for h in range(hb):
        kc = slice(h * dk, (h + 1) * dk)
        z = jnp.dot(code, wg_ref[:, kc], preferred_element_type=F32) + bg_ref[:, kc]
        la = -(jnp.maximum(-z, 0.0) + jnp.log(1.0 + jnp.exp(-jnp.abs(z)))) * (1.0 / GLA_TAU)
        la = la.astype(BF16)
        to_end = jnp.dot(later, la, preferred_element_type=F32)
        las.append(la)
        k_dec.append((k_ref[:, kc].astype(F32) * jnp.exp(to_end)).astype(BF16))

    for c in range(ts // CHUNK):
        sl = slice(c * CHUNK, (c + 1) * CHUNK)
        states = []
        for h in range(hb):
            vc = slice(h * dv, (h + 1) * dv)
            chunk_log = lax.dot_general(las[h][sl], ones, tn, preferred_element_type=F32)
            decay = jnp.exp(chunk_log)
            kv = lax.dot_general(k_dec[h][sl], v_ref[sl, vc], tn, preferred_element_type=F32)
            state = jnp.concatenate([decay] * (dv // lanes), axis=1) * state_ref[h] + kv
            state_ref[h] = state
            states.append(state.astype(BF16))
        for h in range(hb):
            kc = slice(h * dk, (h + 1) * dk)
            vc = slice(h * dv, (h + 1) * dv)
            o = jnp.dot(q_ref[sl, kc], states[h], preferred_element_type=F32) * (dk ** -0.5)
            o = o * lax.rsqrt(jnp.mean(o * o, axis=-1, keepdims=True) + RMS_EPS) * ng_ref[...]
            r = r_ref[sl, vc].astype(F32)
            o_ref[sl, vc] = (o * (r * jax.nn.sigmoid(r))).astype(o_ref.dtype)


def _gla_core(proj, glow, wg2, bg2, norm_g, bsz, seq):
    t = proj.shape[0]
    qk = wg2.shape[1]
    nh = GLA_HEADS
    dk = qk // nh
    dv = norm_g.shape[0]
    hb = GLA_HB
    ng = nh // hb
    ts = min(GLA_TS, seq)
    ns = seq // ts
    v_blk0 = 2 * qk // (hb * dv)
    r_blk0 = v_blk0 + ng
    gw = glow.shape[1]
    return pl.pallas_call(
        functools.partial(_gla_kernel, ts=ts, dk=dk, dv=dv, hb=hb),
        out_shape=jax.ShapeDtypeStruct((t, nh * dv), BF16),
        grid=(bsz, ng, ns),
        in_specs=[pl.BlockSpec((ts, hb * dk), lambda b, g, s: (b * ns + s, g)),
                  pl.BlockSpec((ts, hb * dk), lambda b, g, s: (b * ns + s, ng + g)),
                  pl.BlockSpec((ts, hb * dv), lambda b, g, s: (b * ns + s, v_blk0 + g)),
                  pl.BlockSpec((ts, hb * dv), lambda b, g, s: (b * ns + s, r_blk0 + g)),
                  pl.BlockSpec((ts, gw), lambda b, g, s: (b * ns + s, 0)),
                  pl.BlockSpec((gw, hb * dk), lambda b, g, s: (0, g)),
                  pl.BlockSpec((1, hb * dk), lambda b, g, s: (0, g)),
                  pl.BlockSpec((1, dv), lambda b, g, s: (0, 0))],
        out_specs=pl.BlockSpec((ts, hb * dv), lambda b, g, s: (b * ns + s, g)),
        scratch_shapes=[pltpu.VMEM((hb, dk, dv), F32)],
        compiler_params=_params(("parallel", "parallel", "arbitrary")),
        name="gla_chunk_scan",
    )(proj, proj, proj, proj, glow, wg2, bg2.reshape(1, qk), norm_g.reshape(1, dv))


def _gla_mixer(x_bf, w_in, w_gate2, b_gate2, norm_g, bsz, seq):
    qk = w_gate2.shape[1]
    n_main = w_in.shape[1] - GLA_GATE_RANK
    w_code = jnp.pad(w_in[:, n_main:], ((0, 0), (0, V7X_LANES - GLA_GATE_RANK))).astype(BF16)
    wg2 = jnp.pad(w_gate2, ((0, V7X_LANES - GLA_GATE_RANK), (0, 0))).astype(BF16)
    proj = _matmul_f32w(x_bf, w_in, n_main, BF16)
    glow = _matmul(x_bf, w_code, F32, bn=V7X_LANES)
    return _gla_core(proj, glow, wg2, b_gate2, norm_g, bsz, seq)


def _sb_kernel(q_ref, k_ref, v_ref, o_ref, *, tq, dh, hb):
    i = pl.program_id(2)
    scale = dh ** -0.5
    row = lax.broadcasted_iota(jnp.int32, (tq, tq), 0)
    col = lax.broadcasted_iota(jnp.int32, (tq, tq), 1)
    suffix = jnp.where(row >= col, 1.0, 0.0).astype(BF16)
    causal = col < row
    nt = (((1,), (1,)), ((), ()))

    def block(jb, carry, masked):
        start = pl.multiple_of(jb * tq, tq)
        heads = [slice(h * dh, (h + 1) * dh) for h in range(hb)]
        zs = [lax.dot_general(q_ref[:, c], k_ref[pl.ds(start, tq), c], nt,
                              preferred_element_type=F32) * (scale * LOG2_E) for c in heads]
        row_sums, tails = [], []
        for z in zs:
            sp = jnp.maximum(z, 0.0) + jnp.log2(1.0 + jnp.exp2(-jnp.abs(z)))
            if masked:
                sp = jnp.where(causal, sp, 0.0)
            row_sums.append(jnp.sum(sp, axis=-1, keepdims=True))
            tails.append(jnp.dot(sp.astype(BF16), suffix, preferred_element_type=F32))
        out = []
        for h, c in enumerate(heads):
            right, acc = carry[h]
            w = jnp.exp2(zs[h] - tails[h] - right)
            if masked:
                w = jnp.where(causal, w, 0.0)
            acc = acc + jnp.dot(w.astype(BF16), v_ref[pl.ds(start, tq), c],
                                preferred_element_type=F32)
            right = right + row_sums[h]
            out.append((right, acc))
        return tuple(out)

    carry = tuple((jnp.zeros((tq, 1), F32), jnp.zeros((tq, dh), F32)) for _ in range(hb))
    carry = block(i, carry, True)
    carry = lax.fori_loop(0, i, lambda n, c: block(i - 1 - n, c, False), carry)
    for h in range(hb):
        o_ref[:, h * dh:(h + 1) * dh] = carry[h][1].astype(o_ref.dtype)


def _sb_attention(qp, kvp, bsz, seq):
    t, d = qp.shape
    nh = SB_HEADS
    dh = d // nh
    hb = SB_HB
    ng = nh // hb
    tq = min(SB_TQ, seq)
    nq = seq // tq
    return pl.pallas_call(
        functools.partial(_sb_kernel, tq=tq, dh=dh, hb=hb),
        out_shape=jax.ShapeDtypeStruct((t, d), BF16),
        grid=(bsz, ng, nq),
        in_specs=[pl.BlockSpec((tq, hb * dh), lambda b, g, i: (b * nq + i, g)),
                  pl.BlockSpec((seq, hb * dh), lambda b, g, i: (b, g)),
                  pl.BlockSpec((seq, hb * dh), lambda b, g, i: (b, ng + g))],
        out_specs=pl.BlockSpec((tq, hb * dh), lambda b, g, i: (b * nq + i, g)),
        compiler_params=_params(("parallel", "parallel", "arbitrary")),
        name="stick_breaking_attention",
    )(qp, kvp, kvp)


def _router_kernel(x_ref, w_ref, b_ref, idx_ref, gate_ref, rank_ref, cnt_ref, run_ref):
    i = pl.program_id(0)

    @pl.when(i == 0)
    def _():
        run_ref[...] = jnp.zeros_like(run_ref)

    logits = jnp.dot(x_ref[...], w_ref[...], precision=lax.Precision.HIGHEST,
                     preferred_element_type=F32) + b_ref[...]
    bm, n_e = logits.shape
    lane = lax.broadcasted_iota(jnp.int32, logits.shape, 1)
    vals = logits
    top_v, top_i = [], []
    for _ in range(TOP_K):
        m = jnp.max(vals, axis=-1, keepdims=True)
        sel = jnp.min(jnp.where(vals == m, lane, n_e), axis=-1, keepdims=True)
        top_v.append(m)
        top_i.append(sel)
        vals = jnp.where(lane == sel, -jnp.inf, vals)
    ex = [jnp.exp(v - top_v[0]) for v in top_v]
    denom = ex[0] + ex[1] + ex[2] + ex[3]

    chosen = lane == top_i[0]
    for kk in range(1, TOP_K):
        chosen = chosen | (lane == top_i[kk])
    hit = jnp.where(chosen, 1.0, 0.0)
    row = lax.broadcasted_iota(jnp.int32, (bm, bm), 0)
    col = lax.broadcasted_iota(jnp.int32, (bm, bm), 1)
    earlier = jnp.where(col < row, 1.0, 0.0).astype(BF16)
    before = jnp.dot(earlier, hit.astype(BF16), preferred_element_type=F32) + run_ref[...]
    for kk in range(TOP_K):
        idx_ref[:, kk:kk + 1] = top_i[kk]
        gate_ref[:, kk:kk + 1] = ex[kk] / denom
        rank = jnp.sum(jnp.where(lane == top_i[kk], before, 0.0), axis=-1, keepdims=True)
        rank_ref[:, kk:kk + 1] = rank.astype(jnp.int32)
    run_ref[...] += jnp.sum(hit, axis=0, keepdims=True)
    cnt_ref[...] = run_ref[...].astype(jnp.int32)


def _router(x, w_r, b_r):
    t, d = x.shape
    n_e = w_r.shape[1]
    bm = min(ROUTER_BM, t)
    return pl.pallas_call(
        _router_kernel,
        out_shape=(jax.ShapeDtypeStruct((t, TOP_K), jnp.int32),
                   jax.ShapeDtypeStruct((t, TOP_K), F32),
                   jax.ShapeDtypeStruct((t, TOP_K), jnp.int32),
                   jax.ShapeDtypeStruct((1, n_e), jnp.int32)),
        grid=(t // bm,),
        in_specs=[pl.BlockSpec((bm, d), lambda i: (i, 0)),
                  pl.BlockSpec((d, n_e), lambda i: (0, 0)),
                  pl.BlockSpec((1, n_e), lambda i: (0, 0))],
        out_specs=(pl.BlockSpec((bm, TOP_K), lambda i: (i, 0)),
                   pl.BlockSpec((bm, TOP_K), lambda i: (i, 0)),
                   pl.BlockSpec((bm, TOP_K), lambda i: (i, 0)),
                   pl.BlockSpec((1, n_e), lambda i: (0, 0))),
        scratch_shapes=[pltpu.VMEM((1, n_e), F32)],
        compiler_params=_params(("arbitrary",)),
        name="router_topk",
    )(x, w_r, b_r.reshape(1, n_e))


def _route_tables(idx, rank, counts, tm):
    t = idx.shape[0]
    n_e = counts.shape[0]
    padded = ((counts + tm - 1) // tm) * tm
    pend = jnp.cumsum(padded)
    pstart = pend - padded
    dest = (pstart[idx] + rank).astype(jnp.int32)
    n_rows = t * TOP_K + n_e * tm
    n_tiles = n_rows // tm
    tok = jnp.repeat(jnp.arange(t, dtype=jnp.int32), TOP_K)
    row_tok = (jnp.arange(n_rows, dtype=jnp.int32) % t).at[dest.reshape(-1)].set(tok)
    tile_expert = jnp.clip(jnp.searchsorted(pend, jnp.arange(n_tiles, dtype=jnp.int32) * tm,
                                            side='right'), 0, n_e - 1).astype(jnp.int32)
    n_used = (pend[-1:] // tm).astype(jnp.int32)
    return dest, row_tok, tile_expert, n_used, n_rows, n_tiles


def _dispatch_kernel(rt_ref, nu_ref, x_hbm, o_ref, buf, sem, *, tm):
    t = pl.program_id(0)
    n_used = nu_ref[0]

    def start(tile, slot):
        base = tile * tm

        def body(r, c):
            tok = rt_ref[base + r]
            pltpu.make_async_copy(x_hbm.at[pl.ds(tok, 1)], buf.at[slot, pl.ds(r, 1)],
                                  sem.at[slot]).start()
            return c

        lax.fori_loop(0, tm, body, 0)

    @pl.when(t == 0)
    def _():
        start(0, 0)

    @pl.when(t + 1 < n_used)
    def _():
        start(t + 1, (t + 1) % 2)

    @pl.when(t < n_used)
    def _():
        slot = t % 2
        pltpu.make_async_copy(x_hbm.at[pl.ds(0, tm)], buf.at[slot], sem.at[slot]).wait()
        o_ref[...] = buf[slot].astype(o_ref.dtype)

    @pl.when(t >= n_used)
    def _():
        o_ref[...] = jnp.zeros_like(o_ref)


def _dispatch(x, row_tok, n_used, tm, n_rows, n_tiles):
    d = x.shape[1]
    return pl.pallas_call(
        functools.partial(_dispatch_kernel, tm=tm),
        out_shape=jax.ShapeDtypeStruct((n_rows, d), BF16),
        grid_spec=pltpu.PrefetchScalarGridSpec(
            num_scalar_prefetch=2,
            grid=(n_tiles,),
            in_specs=[pl.BlockSpec(memory_space=pl.ANY)],
            out_specs=pl.BlockSpec((tm, d), lambda t, rt, nu: (t, 0)),
            scratch_shapes=[pltpu.VMEM((2, tm, d), F32), pltpu.SemaphoreType.DMA((2,))]),
        compiler_params=_params(("arbitrary",)),
        name="moe_dispatch_gather",
    )(row_tok, n_used, x)


def _expert_changed(te_ref, t):
    return (t == 0) | (te_ref[t] != te_ref[jnp.maximum(t - 1, 0)])


def _moe_up_kernel(te_ref, nu_ref, x_ref, wg_ref, wu_ref, bg_ref, bu_ref, o_ref, w_bf, *, fc):
    t = pl.program_id(1)

    @pl.when(_expert_changed(te_ref, t))
    def _():
        w_bf[:, :fc] = wg_ref[...].astype(BF16)
        w_bf[:, fc:] = wu_ref[...].astype(BF16)

    @pl.when(t < nu_ref[0])
    def _():
        h = jnp.dot(x_ref[...], w_bf[...], preferred_element_type=F32)
        h_glu = h[:, :fc] + bg_ref[...]
        h_lin = h[:, fc:] + bu_ref[...]
        h_glu = jnp.minimum(h_glu, SWIGLU_LIMIT)
        h_lin = jnp.clip(h_lin, -SWIGLU_LIMIT, SWIGLU_LIMIT)
        act = h_glu * jax.nn.sigmoid(SWIGLU_ALPHA * h_glu) * (h_lin + 1.0)
        o_ref[...] = act.astype(o_ref.dtype)

    @pl.when(t >= nu_ref[0])
    def _():
        o_ref[...] = jnp.zeros_like(o_ref)


def _moe_up(xs, w_gu, b_gu, layer, tile_expert, n_used, tm, n_tiles):
    n_rows, d = xs.shape
    n_l, n_e, _, ff2 = w_gu.shape
    ff = ff2 // 2
    fc = min(MOE_FC, ff)
    nj = ff // fc

    def row_map(j, t, te, nu):
        return (jnp.minimum(t, nu[0] - 1), 0)

    return pl.pallas_call(
        functools.partial(_moe_up_kernel, fc=fc),
        out_shape=jax.ShapeDtypeStruct((n_rows, ff), BF16),
        grid_spec=pltpu.PrefetchScalarGridSpec(
            num_scalar_prefetch=2,
            grid=(nj, n_tiles),
            in_specs=[pl.BlockSpec((tm, d), row_map),
                      pl.BlockSpec((None, None, d, fc), lambda j, t, te, nu: (layer, te[t], 0, j)),
                      pl.BlockSpec((None, None, d, fc),
                                   lambda j, t, te, nu: (layer, te[t], 0, nj + j)),
                      pl.BlockSpec((None, None, 1, fc), lambda j, t, te, nu: (layer, te[t], 0, j)),
                      pl.BlockSpec((None, None, 1, fc),
                                   lambda j, t, te, nu: (layer, te[t], 0, nj + j))],
            out_specs=pl.BlockSpec((tm, fc), lambda j, t, te, nu: (t, j)),
            scratch_shapes=[pltpu.VMEM((d, 2 * fc), BF16)]),
        compiler_params=_params(("arbitrary", "arbitrary")),
        name="moe_gate_up",
    )(tile_expert, n_used, xs, w_gu, w_gu, b_gu.reshape(n_l, n_e, 1, ff2),
      b_gu.reshape(n_l, n_e, 1, ff2))


def _pack_bf16_pair(lo, hi):
    lo_bits = lax.bitcast_convert_type(lo.astype(BF16).astype(F32), jnp.int32)
    hi_bits = lax.bitcast_convert_type(hi.astype(BF16).astype(F32), jnp.int32)
    return hi_bits | lax.shift_right_logical(lo_bits, 16)


def _unpack_bf16_pair(packed):
    lo = lax.bitcast_convert_type(lax.shift_left(packed, 16), F32)
    hi = lax.bitcast_convert_type(packed & jnp.int32(-65536), F32)
    return lo, hi


def _moe_down_kernel(te_ref, nu_ref, a_ref, w_ref, b_ref, o_ref, w_bf):
    t = pl.program_id(0)

    @pl.when(_expert_changed(te_ref, t))
    def _():
        w_bf[...] = w_ref[...].astype(BF16)

    @pl.when(t < nu_ref[0])
    def _():
        y = jnp.dot(a_ref[...], w_bf[...], preferred_element_type=F32) + b_ref[...]
        half = y.shape[1] // 2
        o_ref[...] = _pack_bf16_pair(y[:, :half], y[:, half:])

    @pl.when(t >= nu_ref[0])
    def _():
        o_ref[...] = jnp.zeros_like(o_ref)


def _moe_down(act, w_dn, b_dn, layer, tile_expert, n_used, tm, n_tiles):
    n_rows, ff = act.shape
    n_l, n_e, _, d = w_dn.shape
    return pl.pallas_call(
        _moe_down_kernel,
        out_shape=jax.ShapeDtypeStruct((n_rows, d // 2), jnp.int32),
        grid_spec=pltpu.PrefetchScalarGridSpec(
            num_scalar_prefetch=2,
            grid=(n_tiles,),
            in_specs=[pl.BlockSpec((tm, ff), lambda t, te, nu: (jnp.minimum(t, nu[0] - 1), 0)),
                      pl.BlockSpec((None, None, ff, d), lambda t, te, nu: (layer, te[t], 0, 0)),
                      pl.BlockSpec((None, None, 1, d), lambda t, te, nu: (layer, te[t], 0, 0))],
            out_specs=pl.BlockSpec((tm, d // 2), lambda t, te, nu: (t, 0)),
            scratch_shapes=[pltpu.VMEM((ff, d), BF16)]),
        compiler_params=_params(("arbitrary",)),
        name="moe_down",
    )(tile_expert, n_used, act, w_dn, b_dn.reshape(n_l, n_e, 1, d))


def _combine_kernel(dest_ref, y_hbm, gate_ref, x_ref, g_ref, b_ref, of_ref, ob_ref, buf, sem,
                    *, tc, n_tiles):
    t = pl.program_id(0)

    def start(tile, slot):
        base = tile * tc * TOP_K

        def body(r, c):
            for kk in range(TOP_K):
                row = dest_ref[base + r * TOP_K + kk]
                pltpu.make_async_copy(y_hbm.at[pl.ds(row, 1)], buf.at[slot, kk, pl.ds(r, 1)],
                                      sem.at[slot]).start()
            return c

        lax.fori_loop(0, tc, body, 0)

    @pl.when(t == 0)
    def _():
        start(0, 0)

    @pl.when(t + 1 < n_tiles)
    def _():
        start(t + 1, (t + 1) % 2)

    slot = t % 2
    for kk in range(TOP_K):
        pltpu.make_async_copy(y_hbm.at[pl.ds(0, tc)], buf.at[slot, kk], sem.at[slot]).wait()
    gates = gate_ref[...]
    lo, hi = _unpack_bf16_pair(buf[slot, 0])
    ffn_lo = gates[:, 0:1] * lo
    ffn_hi = gates[:, 0:1] * hi
    for kk in range(1, TOP_K):
        lo, hi = _unpack_bf16_pair(buf[slot, kk])
        ffn_lo = ffn_lo + gates[:, kk:kk + 1] * lo
        ffn_hi = ffn_hi + gates[:, kk:kk + 1] * hi
    ffn = jnp.concatenate([ffn_lo, ffn_hi], axis=1)
    out = _layer_norm_rows(DN_ALPHA * x_ref[...] + ffn, g_ref[...], b_ref[...])
    of_ref[...] = out
    ob_ref[...] = out.astype(ob_ref.dtype)


def _combine_ln(y, dest, gates, xres, g, b):
    t, d = xres.shape
    tc = min(COMBINE_TC, t)
    n_tiles = t // tc
    return pl.pallas_call(
        functools.partial(_combine_kernel, tc=tc, n_tiles=n_tiles),
        out_shape=(jax.ShapeDtypeStruct((t, d), F32), jax.ShapeDtypeStruct((t, d), BF16)),
        grid_spec=pltpu.PrefetchScalarGridSpec(
            num_scalar_prefetch=1,
            grid=(n_tiles,),
            in_specs=[pl.BlockSpec(memory_space=pl.ANY),
                      pl.BlockSpec((tc, TOP_K), lambda i, ds: (i, 0)),
                      pl.BlockSpec((tc, d), lambda i, ds: (i, 0)),
                      pl.BlockSpec((1, d), lambda i, ds: (0, 0)),
                      pl.BlockSpec((1, d), lambda i, ds: (0, 0))],
            out_specs=(pl.BlockSpec((tc, d), lambda i, ds: (i, 0)),
                       pl.BlockSpec((tc, d), lambda i, ds: (i, 0))),
            scratch_shapes=[pltpu.VMEM((2, TOP_K, tc, d // 2), jnp.int32),
                            pltpu.SemaphoreType.DMA((2,))]),
        compiler_params=_params(("arbitrary",)),
        name="moe_combine_deepnorm",
    )(dest.reshape(-1), y, gates, xres, g.reshape(1, d), b.reshape(1, d))


def _moe_block(x, w_r, b_r, w_gu, b_gu, w_dn, b_dn, layer, g, b):
    tm = MOE_TM
    idx, gates, rank, counts = _router(x, w_r, b_r)
    dest, row_tok, tile_expert, n_used, n_rows, n_tiles = _route_tables(idx, rank, counts[0], tm)
    xs = _dispatch(x, row_tok, n_used, tm, n_rows, n_tiles)
    act = _moe_up(xs, w_gu, b_gu, layer, tile_expert, n_used, tm, n_tiles)
    y = _moe_down(act, w_dn, b_dn, layer, tile_expert, n_used, tm, n_tiles)
    return _combine_ln(y, dest, gates, x, g, b)


def kernel(x, gla_w_in, gla_w_gate2, gla_b_gate2, gla_norm_g, gla_w_out, sb_w_q, sb_w_out,
           shared_w_kv, router_w, router_b, moe_w_gate_up, moe_b_gate_up, moe_w_down, moe_b_down,
           ln1_g, ln1_b, ln2_g, ln2_b):
    bsz, seq, d = x.shape
    n_a = DEPTH // 2
    xf = x.reshape(bsz * seq, d)
    xb = xf.astype(BF16)
    kvp = None
    for layer in range(DEPTH):
        if layer < n_a:
            mix = _gla_mixer(xb, gla_w_in[layer], gla_w_gate2[layer], gla_b_gate2[layer],
                             gla_norm_g[layer], bsz, seq)
            w_out = gla_w_out[layer]
        else:
            if layer == n_a:
                kvp = _matmul_f32w(xb, shared_w_kv, shared_w_kv.shape[1], BF16)
            j = layer - n_a
            qp = _matmul_f32w(xb, sb_w_q[j], d, BF16)
            mix = _sb_attention(qp, kvp, bsz, seq)
            w_out = sb_w_out[j]
        xf = _matmul_ln(mix, w_out.astype(BF16), xf, ln1_g[layer], ln1_b[layer])
        xf, xb = _moe_block(xf, router_w[layer], router_b[layer], moe_w_gate_up, moe_b_gate_up,
                            moe_w_down, moe_b_down, layer, ln2_g[layer], ln2_b[layer])
    return xf.reshape(bsz, seq, d)
```

```python
import functools

import jax
import jax.numpy as jnp
from jax import lax
from jax.experimental import pallas as pl
from jax.experimental.pallas import tpu as pltpu

F32 = jnp.float32
BF16 = jnp.bfloat16

DEPTH = 2
CHUNK = 64
GLA_HEADS = 8
GLA_GATE_RANK = 16
GLA_TAU = 16.0
SB_HEADS = 32
N_EXPERTS = 32
TOP_K = 4
EXPERT_FF = 768
SWIGLU_LIMIT = 7.0
SWIGLU_ALPHA = 1.702
DN_ALPHA = (2.0 * DEPTH) ** 0.25
LN_EPS = 1e-5
RMS_EPS = 1e-6
LOG2_E = 1.4426950408889634

V7X_LANES = 128
V7X_VMEM_LIMIT_BYTES = 56 * 1024 * 1024

MM_BM = 1024
MMW_BN = 512
LN_BM = 512
LN_BK = 512
GLA_TS = 256
GLA_HB = 4
SB_TQ = 256
SB_HB = 8
ROUTER_BM = 512
MOE_TM = 256
MOE_FC = 384
COMBINE_TC = 128


def _params(semantics):
    return pltpu.CompilerParams(dimension_semantics=semantics,
                                vmem_limit_bytes=V7X_VMEM_LIMIT_BYTES)


def _mm_f32w_kernel(a_ref, w_ref, o_ref, w_bf, *, w_is_nk):
    @pl.when(pl.program_id(1) == 0)
    def _():
        w_bf[...] = w_ref[...].astype(BF16)

    contract = (((1,), (1 if w_is_nk else 0,)), ((), ()))
    o_ref[...] = lax.dot_general(a_ref[...], w_bf[...], contract,
                                 preferred_element_type=F32).astype(o_ref.dtype)


def _matmul_f32w(a, w, n, out_dtype, w_is_nk=False):
    m, k = a.shape
    bm = min(MM_BM, m)
    bn = min(MMW_BN, n)
    assert m % bm == 0 and n % bn == 0, (m, n, bm, bn)
    w_spec = (pl.BlockSpec((bn, k), lambda j, i: (j, 0)) if w_is_nk
              else pl.BlockSpec((k, bn), lambda j, i: (0, j)))
    return pl.pallas_call(
        functools.partial(_mm_f32w_kernel, w_is_nk=w_is_nk),
        out_shape=jax.ShapeDtypeStruct((m, n), out_dtype),
        grid=(n // bn, m // bm),
        in_specs=[pl.BlockSpec((bm, k), lambda j, i: (i, 0)), w_spec],
        out_specs=pl.BlockSpec((bm, bn), lambda j, i: (i, j)),
        scratch_shapes=[pltpu.VMEM((bn, k) if w_is_nk else (k, bn), BF16)],
        compiler_params=_params(("arbitrary", "arbitrary")),
        name="dense_matmul_f32w",
    )(a, w)


def _layer_norm_rows(y, g, b):
    mu = jnp.mean(y, axis=-1, keepdims=True)
    yc = y - mu
    var = jnp.mean(yc * yc, axis=-1, keepdims=True)
    return yc * lax.rsqrt(var + LN_EPS) * g + b


def _mm_ln_kernel(a_ref, w_ref, x_ref, g_ref, b_ref, o_ref, *, nk):
    k = pl.program_id(1)

    @pl.when(k == 0)
    def _():
        o_ref[...] = DN_ALPHA * x_ref[...]

    o_ref[...] += jnp.dot(a_ref[...], w_ref[...], preferred_element_type=F32)

    @pl.when(k == nk - 1)
    def _():
        o_ref[...] = _layer_norm_rows(o_ref[...], g_ref[...], b_ref[...])


def _matmul_ln(a, w, xres, g, b):
    m, k = a.shape
    n = w.shape[1]
    bm = min(LN_BM, m)
    bk = min(LN_BK, k)
    nk = k // bk
    return pl.pallas_call(
        functools.partial(_mm_ln_kernel, nk=nk),
        out_shape=jax.ShapeDtypeStruct((m, n), F32),
        grid=(m // bm, nk),
        in_specs=[pl.BlockSpec((bm, bk), lambda i, kk: (i, kk)),
                  pl.BlockSpec((bk, n), lambda i, kk: (kk, 0)),
                  pl.BlockSpec((bm, n), lambda i, kk: (i, 0)),
                  pl.BlockSpec((1, n), lambda i, kk: (0, 0)),
                  pl.BlockSpec((1, n), lambda i, kk: (0, 0))],
        out_specs=pl.BlockSpec((bm, n), lambda i, kk: (i, 0)),
        compiler_params=_params(("parallel", "arbitrary")),
        name="matmul_deepnorm",
    )(a, w, xres, g.reshape(1, n), b.reshape(1, n))


def _gla_kernel(q_ref, k_ref, v_ref, r_ref, gl_ref, wg_ref, bg_ref, ng_ref, o_ref, state_ref,
                *, ts, dk, dv, hb):
    s = pl.program_id(2)

    @pl.when(s == 0)
    def _():
        state_ref[...] = jnp.zeros_like(state_ref)

    row = lax.broadcasted_iota(jnp.int32, (ts, ts), 0)
    col = lax.broadcasted_iota(jnp.int32, (ts, ts), 1)
    shift = CHUNK.bit_length() - 1
    later = ((row >> shift) == (col >> shift)) & (col > row)
    later = jnp.where(later, 1.0, 0.0).astype(BF16)
    lanes = min(dv, V7X_LANES)
    ones = jnp.ones((CHUNK, lanes), BF16)
    tn = (((0,), (0,)), ((), ()))
    code = gl_ref[...].astype(BF16)

    las, k_dec = [], []
    for h in range(hb):
        kc = slice(h * dk, (h + 1) * dk)
        z = jnp.dot(code, wg_ref[:, kc], preferred_element_type=F32) + bg_ref[:, kc]
        la = -(jnp.maximum(-z, 0.0) + jnp.log(1.0 + jnp.exp(-jnp.abs(z)))) * (1.0 / GLA_TAU)
        la = la.astype(BF16)
        to_end = jnp.dot(later, la, preferred_element_type=F32)
        las.append(la)
        k_dec.append((k_ref[:, kc].astype(F32) * jnp.exp(to_end)).astype(BF16))

    for c in range(ts // CHUNK):
        sl = slice(c * CHUNK, (c + 1) * CHUNK)
        states = []
        for h in range(hb):
            vc = slice(h * dv, (h + 1) * dv)
            chunk_log = lax.dot_general(las[h][sl], ones, tn, preferred_element_type=F32)
            decay = jnp.exp(chunk_log)
            kv = lax.dot_general(k_dec[h][sl], v_ref[sl, vc], tn, preferred_element_type=F32)
            state = jnp.concatenate([decay] * (dv // lanes), axis=1) * state_ref[h] + kv
            state_ref[h] = state
            states.append(state.astype(BF16))
        for h in range(hb):
            kc = slice(h * dk, (h + 1) * dk)
            vc = slice(h * dv, (h + 1) * dv)
            o = jnp.dot(q_ref[sl, kc], states[h], preferred_element_type=F32) * (dk ** -0.5)
            o = o * lax.rsqrt(jnp.mean(o * o, axis=-1, keepdims=True) + RMS_EPS) * ng_ref[...]
            r = r_ref[sl, vc].astype(F32)
            o_ref[sl, vc] = (o * (r * jax.nn.sigmoid(r))).astype(o_ref.dtype)


def _gla_core(proj, glow, wg2, bg2, norm_g, bsz, seq):
    t = proj.shape[0]
    qk = wg2.shape[1]
    nh = GLA_HEADS
    dk = qk // nh
    dv = norm_g.shape[0]
    hb = GLA_HB
    ng = nh // hb
    ts = min(GLA_TS, seq)
    ns = seq // ts
    v_blk0 = 2 * qk // (hb * dv)
    r_blk0 = v_blk0 + ng
    gw = glow.shape[1]
    return pl.pallas_call(
        functools.partial(_gla_kernel, ts=ts, dk=dk, dv=dv, hb=hb),
        out_shape=jax.ShapeDtypeStruct((t, nh * dv), BF16),
        grid=(bsz, ng, ns),
        in_specs=[pl.BlockSpec((ts, hb * dk), lambda b, g, s: (b * ns + s, g)),
                  pl.BlockSpec((ts, hb * dk), lambda b, g, s: (b * ns + s, ng + g)),
                  pl.BlockSpec((ts, hb * dv), lambda b, g, s: (b * ns + s, v_blk0 + g)),
                  pl.BlockSpec((ts, hb * dv), lambda b, g, s: (b * ns + s, r_blk0 + g)),
                  pl.BlockSpec((ts, gw), lambda b, g, s: (b * ns + s, 0)),
                  pl.BlockSpec((gw, hb * dk), lambda b, g, s: (0, g)),
                  pl.BlockSpec((1, hb * dk), lambda b, g, s: (0, g)),
                  pl.BlockSpec((1, dv), lambda b, g, s: (0, 0))],
        out_specs=pl.BlockSpec((ts, hb * dv), lambda b, g, s: (b * ns + s, g)),
        scratch_shapes=[pltpu.VMEM((hb, dk, dv), F32)],
        compiler_params=_params(("parallel", "parallel", "arbitrary")),
        name="gla_chunk_scan",
    )(proj, proj, proj, proj, glow, wg2, bg2.reshape(1, qk), norm_g.reshape(1, dv))


def _gla_mixer(x_bf, w_in, w_gate2, b_gate2, norm_g, bsz, seq):
    qk = w_gate2.shape[1]
    n_main = w_in.shape[1] - GLA_GATE_RANK
    w_in_t = w_in.T
    w_code_t = jnp.pad(w_in_t[n_main:], ((0, V7X_LANES - GLA_GATE_RANK), (0, 0)))
    wg2 = jnp.pad(w_gate2, ((0, V7X_LANES - GLA_GATE_RANK), (0, 0))).astype(BF16)
    proj = _matmul_f32w(x_bf, w_in_t, n_main, BF16, w_is_nk=True)
    glow = _matmul_f32w(x_bf, w_code_t, V7X_LANES, F32, w_is_nk=True)
    return _gla_core(proj, glow, wg2, b_gate2, norm_g, bsz, seq)


def _sb_kernel(q_ref, k_ref, v_ref, o_ref, *, tq, dh, hb):
    i = pl.program_id(2)
    scale = dh ** -0.5
    row = lax.broadcasted_iota(jnp.int32, (tq, tq), 0)
    col = lax.broadcasted_iota(jnp.int32, (tq, tq), 1)
    suffix = jnp.where(row >= col, 1.0, 0.0).astype(BF16)
    causal = col < row
    nt = (((1,), (1,)), ((), ()))

    def block(jb, carry, masked):
        start = pl.multiple_of(jb * tq, tq)
        heads = [slice(h * dh, (h + 1) * dh) for h in range(hb)]
        zs = [lax.dot_general(q_ref[:, c], k_ref[pl.ds(start, tq), c], nt,
                              preferred_element_type=F32) * (scale * LOG2_E) for c in heads]
        row_sums, tails = [], []
        for z in zs:
            sp = jnp.maximum(z, 0.0) + jnp.log2(1.0 + jnp.exp2(-jnp.abs(z)))
            if masked:
                sp = jnp.where(causal, sp, 0.0)
            row_sums.append(jnp.sum(sp, axis=-1, keepdims=True))
            tails.append(jnp.dot(sp.astype(BF16), suffix, preferred_element_type=F32))
        out = []
        for h, c in enumerate(heads):
            right, acc = carry[h]
            w = jnp.exp2(zs[h] - tails[h] - right)
            if masked:
                w = jnp.where(causal, w, 0.0)
            acc = acc + jnp.dot(w.astype(BF16), v_ref[pl.ds(start, tq), c],
                                preferred_element_type=F32)
            right = right + row_sums[h]
            out.append((right, acc))
        return tuple(out)

    carry = tuple((jnp.zeros((tq, 1), F32), jnp.zeros((tq, dh), F32)) for _ in range(hb))
    carry = block(i, carry, True)
    carry = lax.fori_loop(0, i, lambda n, c: block(i - 1 - n, c, False), carry)
    for h in range(hb):
        o_ref[:, h * dh:(h + 1) * dh] = carry[h][1].astype(o_ref.dtype)


def _sb_attention(qp, kvp, bsz, seq):
    t, d = qp.shape
    nh = SB_HEADS
    dh = d // nh
    hb = SB_HB
    ng = nh // hb
    tq = min(SB_TQ, seq)
    nq = seq // tq
    return pl.pallas_call(
        functools.partial(_sb_kernel, tq=tq, dh=dh, hb=hb),
        out_shape=jax.ShapeDtypeStruct((t, d), BF16),
        grid=(bsz, ng, nq),
        in_specs=[pl.BlockSpec((tq, hb * dh), lambda b, g, i: (b * nq + i, g)),
                  pl.BlockSpec((seq, hb * dh), lambda b, g, i: (b, g)),
                  pl.BlockSpec((seq, hb * dh), lambda b, g, i: (b, ng + g))],
        out_specs=pl.BlockSpec((tq, hb * dh), lambda b, g, i: (b * nq + i, g)),
        compiler_params=_params(("parallel", "parallel", "arbitrary")),
        name="stick_breaking_attention",
    )(qp, kvp, kvp)


def _router_kernel(x_ref, w_ref, b_ref, idx_ref, gate_ref, rank_ref, cnt_ref, run_ref):
    i = pl.program_id(0)

    @pl.when(i == 0)
    def _():
        run_ref[...] = jnp.zeros_like(run_ref)

    logits = jnp.dot(x_ref[...], w_ref[...], precision=lax.Precision.HIGHEST,
                     preferred_element_type=F32) + b_ref[...]
    bm, n_e = logits.shape
    lane = lax.broadcasted_iota(jnp.int32, logits.shape, 1)
    vals = logits
    top_v, top_i = [], []
    for _ in range(TOP_K):
        m = jnp.max(vals, axis=-1, keepdims=True)
        sel = jnp.min(jnp.where(vals == m, lane, n_e), axis=-1, keepdims=True)
        top_v.append(m)
        top_i.append(sel)
        vals = jnp.where(lane == sel, -jnp.inf, vals)
    ex = [jnp.exp(v - top_v[0]) for v in top_v]
    denom = ex[0] + ex[1] + ex[2] + ex[3]

    chosen = lane == top_i[0]
    for kk in range(1, TOP_K):
        chosen = chosen | (lane == top_i[kk])
    hit = jnp.where(chosen, 1.0, 0.0)
    row = lax.broadcasted_iota(jnp.int32, (bm, bm), 0)
    col = lax.broadcasted_iota(jnp.int32, (bm, bm), 1)
    earlier = jnp.where(col < row, 1.0, 0.0).astype(BF16)
    before = jnp.dot(earlier, hit.astype(BF16), preferred_element_type=F32) + run_ref[...]
    for kk in range(TOP_K):
        idx_ref[:, kk:kk + 1] = top_i[kk]
        gate_ref[:, kk:kk + 1] = ex[kk] / denom
        rank = jnp.sum(jnp.where(lane == top_i[kk], before, 0.0), axis=-1, keepdims=True)
        rank_ref[:, kk:kk + 1] = rank.astype(jnp.int32)
    run_ref[...] += jnp.sum(hit, axis=0, keepdims=True)
    cnt_ref[...] = run_ref[...].astype(jnp.int32)


def _router(x, w_r, b_r):
    t, d = x.shape
    n_e = w_r.shape[1]
    bm = min(ROUTER_BM, t)
    return pl.pallas_call(
        _router_kernel,
        out_shape=(jax.ShapeDtypeStruct((t, TOP_K), jnp.int32),
                   jax.ShapeDtypeStruct((t, TOP_K), F32),
                   jax.ShapeDtypeStruct((t, TOP_K), jnp.int32),
                   jax.ShapeDtypeStruct((1, n_e), jnp.int32)),
        grid=(t // bm,),
        in_specs=[pl.BlockSpec((bm, d), lambda i: (i, 0)),
                  pl.BlockSpec((d, n_e), lambda i: (0, 0)),
                  pl.BlockSpec((1, n_e), lambda i: (0, 0))],
        out_specs=(pl.BlockSpec((bm, TOP_K), lambda i: (i, 0)),
                   pl.BlockSpec((bm, TOP_K), lambda i: (i, 0)),
                   pl.BlockSpec((bm, TOP_K), lambda i: (i, 0)),
                   pl.BlockSpec((1, n_e), lambda i: (0, 0))),
        scratch_shapes=[pltpu.VMEM((1, n_e), F32)],
        compiler_params=_params(("arbitrary",)),
        name="router_topk",
    )(x, w_r, b_r.reshape(1, n_e))


def _route_tables(idx, rank, counts, tm):
    t = idx.shape[0]
    n_e = counts.shape[0]
    padded = ((counts + tm - 1) // tm) * tm
    pend = jnp.cumsum(padded)
    pstart = pend - padded
    dest = (pstart[idx] + rank).astype(jnp.int32)
    n_rows = t * TOP_K + n_e * tm
    n_tiles = n_rows // tm
    tok = jnp.repeat(jnp.arange(t, dtype=jnp.int32), TOP_K)
    row_tok = (jnp.arange(n_rows, dtype=jnp.int32) % t).at[dest.reshape(-1)].set(tok)
    tile_end = pend // tm
    tile_id = jnp.arange(n_tiles, dtype=jnp.int32)
    tile_expert = jnp.minimum(jnp.sum(tile_end[None, :] <= tile_id[:, None], axis=1),
                              n_e - 1).astype(jnp.int32)
    n_used = tile_end[-1:].astype(jnp.int32)
    nonempty = (counts > 0).astype(jnp.int32)
    group_of_expert = jnp.cumsum(nonempty) - nonempty
    tile_group = group_of_expert[tile_expert].astype(jnp.int32)
    next_tile = tile_end[tile_expert]
    next_expert = jnp.where(next_tile < n_used[0],
                            tile_expert[jnp.minimum(next_tile, n_tiles - 1)], -1).astype(jnp.int32)
    n_groups = jnp.sum(nonempty, keepdims=True).astype(jnp.int32)
    sched = (tile_expert, next_expert, tile_group, n_groups, n_used)
    return dest, row_tok, sched, n_rows, n_tiles


def _dispatch_kernel(rt_ref, nu_ref, x_hbm, o_ref, buf, sem, *, tm):
    t = pl.program_id(0)
    n_used = nu_ref[0]

    def start(tile, slot):
        base = tile * tm

        def body(r, c):
            tok = rt_ref[base + r]
            pltpu.make_async_copy(x_hbm.at[pl.ds(tok, 1)], buf.at[slot, pl.ds(r, 1)],
                                  sem.at[slot]).start()
            return c

        lax.fori_loop(0, tm, body, 0)

    @pl.when((t == 0) & (n_used > 0))
    def _():
        start(0, 0)

    @pl.when(t + 1 < n_used)
    def _():
        start(t + 1, (t + 1) % 2)

    @pl.when(t < n_used)
    def _():
        slot = t % 2
        pltpu.make_async_copy(x_hbm.at[pl.ds(0, tm)], buf.at[slot], sem.at[slot]).wait()
        o_ref[...] = buf[slot].astype(o_ref.dtype)

    @pl.when(t >= n_used)
    def _():
        o_ref[...] = jnp.zeros_like(o_ref)


def _dispatch(x, row_tok, n_used, tm, n_rows, n_tiles):
    d = x.shape[1]
    return pl.pallas_call(
        functools.partial(_dispatch_kernel, tm=tm),
        out_shape=jax.ShapeDtypeStruct((n_rows, d), BF16),
        grid_spec=pltpu.PrefetchScalarGridSpec(
            num_scalar_prefetch=2,
            grid=(n_tiles,),
            in_specs=[pl.BlockSpec(memory_space=pl.ANY)],
            out_specs=pl.BlockSpec((tm, d), lambda t, rt, nu: (t, 0)),
            scratch_shapes=[pltpu.VMEM((2, tm, d), F32), pltpu.SemaphoreType.DMA((2,))]),
        compiler_params=_params(("arbitrary",)),
        name="moe_dispatch_gather",
    )(row_tok, n_used, x)


def _group_start(te_ref, t):
    return (t == 0) | (te_ref[t] != te_ref[jnp.maximum(t - 1, 0)])


def _moe_up_kernel(te_ref, ne_ref, tg_ref, ng_ref, nu_ref, x_ref, w_hbm, bg_ref, bu_ref, o_ref,
                   stage, w_bf, sem, *, layer, fc, ff, nj):
    j = pl.program_id(0)
    t = pl.program_id(1)
    n_used = nu_ref[0]

    def weight_copies(jj, e, slot):
        gate_col = pl.multiple_of(jj * fc, V7X_LANES)
        up_col = pl.multiple_of(ff + jj * fc, V7X_LANES)
        return (pltpu.make_async_copy(w_hbm.at[layer, e, :, pl.ds(gate_col, fc)],
                                      stage.at[slot, 0], sem.at[slot, 0]),
                pltpu.make_async_copy(w_hbm.at[layer, e, :, pl.ds(up_col, fc)],
                                      stage.at[slot, 1], sem.at[slot, 1]))

    @pl.when((t < n_used) & _group_start(te_ref, t))
    def _():
        e = te_ref[t]
        slot = (j * ng_ref[0] + tg_ref[t]) % 2

        @pl.when((j == 0) & (t == 0))
        def _():
            for cp in weight_copies(j, e, slot):
                cp.start()

        nxt = ne_ref[t]
        more_here = nxt >= 0
        more_later = jnp.logical_not(more_here) & (j + 1 < nj)

        @pl.when(more_here | more_later)
        def _():
            for cp in weight_copies(jnp.where(more_here, j, j + 1),
                                    jnp.where(more_here, nxt, te_ref[0]), 1 - slot):
                cp.start()

        for cp in weight_copies(j, e, slot):
            cp.wait()
        w_bf[:, :fc] = stage[slot, 0].astype(BF16)
        w_bf[:, fc:] = stage[slot, 1].astype(BF16)

    @pl.when(t < n_used)
    def _():
        h = jnp.dot(x_ref[...], w_bf[...], preferred_element_type=F32)
        h_glu = h[:, :fc] + bg_ref[...]
        h_lin = h[:, fc:] + bu_ref[...]
        h_glu = jnp.minimum(h_glu, SWIGLU_LIMIT)
        h_lin = jnp.clip(h_lin, -SWIGLU_LIMIT, SWIGLU_LIMIT)
        act = h_glu * jax.nn.sigmoid(SWIGLU_ALPHA * h_glu) * (h_lin + 1.0)
        o_ref[...] = act.astype(o_ref.dtype)

    @pl.when(t >= n_used)
    def _():
        o_ref[...] = jnp.zeros_like(o_ref)


def _moe_up(xs, w_gu, b_gu, layer, sched, tm, n_tiles):
    n_rows, d = xs.shape
    n_l, n_e, _, ff2 = w_gu.shape
    ff = ff2 // 2
    fc = min(MOE_FC, ff)
    nj = ff // fc

    def row_map(j, t, te, ne, tg, ng, nu):
        return (jnp.maximum(jnp.minimum(t, nu[0] - 1), 0), 0)

    return pl.pallas_call(
        functools.partial(_moe_up_kernel, layer=layer, fc=fc, ff=ff, nj=nj),
        out_shape=jax.ShapeDtypeStruct((n_rows, ff), BF16),
        grid_spec=pltpu.PrefetchScalarGridSpec(
            num_scalar_prefetch=5,
            grid=(nj, n_tiles),
            in_specs=[pl.BlockSpec((tm, d), row_map),
                      pl.BlockSpec(memory_space=pl.ANY),
                      pl.BlockSpec((None, None, 1, fc),
                                   lambda j, t, te, ne, tg, ng, nu: (layer, te[t], 0, j)),
                      pl.BlockSpec((None, None, 1, fc),
                                   lambda j, t, te, ne, tg, ng, nu: (layer, te[t], 0, nj + j))],
            out_specs=pl.BlockSpec((tm, fc), lambda j, t, te, ne, tg, ng, nu: (t, j)),
            scratch_shapes=[pltpu.VMEM((2, 2, d, fc), F32),
                            pltpu.VMEM((d, 2 * fc), BF16),
                            pltpu.SemaphoreType.DMA((2, 2))]),
        compiler_params=_params(("arbitrary", "arbitrary")),
        name="moe_gate_up",
    )(*sched, xs, w_gu, b_gu.reshape(n_l, n_e, 1, ff2), b_gu.reshape(n_l, n_e, 1, ff2))


def _pack_bf16_pair(lo, hi):
    lo_bits = lax.bitcast_convert_type(lo.astype(BF16).astype(F32), jnp.int32)
    hi_bits = lax.bitcast_convert_type(hi.astype(BF16).astype(F32), jnp.int32)
    return hi_bits | lax.shift_right_logical(lo_bits, 16)


def _unpack_bf16_pair(packed):
    lo = lax.bitcast_convert_type(lax.shift_left(packed, 16), F32)
    hi = lax.bitcast_convert_type(packed & jnp.int32(-65536), F32)
    return lo, hi


def _moe_down_kernel(te_ref, ne_ref, tg_ref, ng_ref, nu_ref, a_ref, w_hbm, b_ref, o_ref,
                     stage, w_bf, sem, *, layer):
    t = pl.program_id(0)
    n_used = nu_ref[0]

    def weight_copy(e, slot):
        return pltpu.make_async_copy(w_hbm.at[layer, e], stage.at[slot], sem.at[slot])

    @pl.when((t < n_used) & _group_start(te_ref, t))
    def _():
        e = te_ref[t]
        slot = tg_ref[t] % 2

        @pl.when(t == 0)
        def _():
            weight_copy(e, slot).start()

        nxt = ne_ref[t]

        @pl.when(nxt >= 0)
        def _():
            weight_copy(nxt, 1 - slot).start()

        weight_copy(e, slot).wait()
        w_bf[...] = stage[slot].astype(BF16)

    @pl.when(t < n_used)
    def _():
        y = jnp.dot(a_ref[...], w_bf[...], preferred_element_type=F32) + b_ref[...]
        half = y.shape[1] // 2
        o_ref[...] = _pack_bf16_pair(y[:, :half], y[:, half:])

    @pl.when(t >= n_used)
    def _():
        o_ref[...] = jnp.zeros_like(o_ref)


def _moe_down(act, w_dn, b_dn, layer, sched, tm, n_tiles):
    n_rows, ff = act.shape
    n_l, n_e, _, d = w_dn.shape
    return pl.pallas_call(
        functools.partial(_moe_down_kernel, layer=layer),
        out_shape=jax.ShapeDtypeStruct((n_rows, d // 2), jnp.int32),
        grid_spec=pltpu.PrefetchScalarGridSpec(
            num_scalar_prefetch=5,
            grid=(n_tiles,),
            in_specs=[pl.BlockSpec((tm, ff), lambda t, te, ne, tg, ng, nu:
                                   (jnp.maximum(jnp.minimum(t, nu[0] - 1), 0), 0)),
                      pl.BlockSpec(memory_space=pl.ANY),
                      pl.BlockSpec((None, None, 1, d),
                                   lambda t, te, ne, tg, ng, nu: (layer, te[t], 0, 0))],
            out_specs=pl.BlockSpec((tm, d // 2), lambda t, te, ne, tg, ng, nu: (t, 0)),
            scratch_shapes=[pltpu.VMEM((2, ff, d), F32),
                            pltpu.VMEM((ff, d), BF16),
                            pltpu.SemaphoreType.DMA((2,))]),
        compiler_params=_params(("arbitrary",)),
        name="moe_down",
    )(*sched, act, w_dn, b_dn.reshape(n_l, n_e, 1, d))


def _combine_kernel(dest_ref, y_hbm, gate_ref, x_ref, g_ref, b_ref, of_ref, ob_ref, buf, sem,
                    *, tc, n_tiles):
    t = pl.program_id(0)

    def start(tile, slot):
        base = tile * tc * TOP_K

        def body(r, c):
            for kk in range(TOP_K):
                row = dest_ref[base + r * TOP_K + kk]
                pltpu.make_async_copy(y_hbm.at[pl.ds(row, 1)], buf.at[slot, kk, pl.ds(r, 1)],
                                      sem.at[slot]).start()
            return c

        lax.fori_loop(0, tc, body, 0)

    @pl.when(t == 0)
    def _():
        start(0, 0)

    @pl.when(t + 1 < n_tiles)
    def _():
        start(t + 1, (t + 1) % 2)

    slot = t % 2
    for kk in range(TOP_K):
        pltpu.make_async_copy(y_hbm.at[pl.ds(0, tc)], buf.at[slot, kk], sem.at[slot]).wait()
    gates = gate_ref[...]
    lo, hi = _unpack_bf16_pair(buf[slot, 0])
    ffn_lo = gates[:, 0:1] * lo
    ffn_hi = gates[:, 0:1] * hi
    for kk in range(1, TOP_K):
        lo, hi = _unpack_bf16_pair(buf[slot, kk])
        ffn_lo = ffn_lo + gates[:, kk:kk + 1] * lo
        ffn_hi = ffn_hi + gates[:, kk:kk + 1] * hi
    ffn = jnp.concatenate([ffn_lo, ffn_hi], axis=1)
    out = _layer_norm_rows(DN_ALPHA * x_ref[...] + ffn, g_ref[...], b_ref[...])
    of_ref[...] = out
    ob_ref[...] = out.astype(ob_ref.dtype)


def _combine_ln(y, dest, gates, xres, g, b):
    t, d = xres.shape
    tc = min(COMBINE_TC, t)
    n_tiles = t // tc
    return pl.pallas_call(
        functools.partial(_combine_kernel, tc=tc, n_tiles=n_tiles),
        out_shape=(jax.ShapeDtypeStruct((t, d), F32), jax.ShapeDtypeStruct((t, d), BF16)),
        grid_spec=pltpu.PrefetchScalarGridSpec(
            num_scalar_prefetch=1,
            grid=(n_tiles,),
            in_specs=[pl.BlockSpec(memory_space=pl.ANY),
                      pl.BlockSpec((tc, TOP_K), lambda i, ds: (i, 0)),
                      pl.BlockSpec((tc, d), lambda i, ds: (i, 0)),
                      pl.BlockSpec((1, d), lambda i, ds: (0, 0)),
                      pl.BlockSpec((1, d), lambda i, ds: (0, 0))],
            out_specs=(pl.BlockSpec((tc, d), lambda i, ds: (i, 0)),
                       pl.BlockSpec((tc, d), lambda i, ds: (i, 0))),
            scratch_shapes=[pltpu.VMEM((2, TOP_K, tc, d // 2), jnp.int32),
                            pltpu.SemaphoreType.DMA((2,))]),
        compiler_params=_params(("arbitrary",)),
        name="moe_combine_deepnorm",
    )(dest.reshape(-1), y, gates, xres, g.reshape(1, d), b.reshape(1, d))


def _moe_block(x, w_r, b_r, w_gu, b_gu, w_dn, b_dn, layer, g, b):
    tm = MOE_TM
    idx, gates, rank, counts = _router(x, w_r, b_r)
    dest, row_tok, sched, n_rows, n_tiles = _route_tables(idx, rank, counts[0], tm)
    xs = _dispatch(x, row_tok, sched[-1], tm, n_rows, n_tiles)
    act = _moe_up(xs, w_gu, b_gu, layer, sched, tm, n_tiles)
    y = _moe_down(act, w_dn, b_dn, layer, sched, tm, n_tiles)
    return _combine_ln(y, dest, gates, x, g, b)


def kernel(x, gla_w_in, gla_w_gate2, gla_b_gate2, gla_norm_g, gla_w_out, sb_w_q, sb_w_out,
           shared_w_kv, router_w, router_b, moe_w_gate_up, moe_b_gate_up, moe_w_down, moe_b_down,
           ln1_g, ln1_b, ln2_g, ln2_b):
    bsz, seq, d = x.shape
    n_a = DEPTH // 2
    xf = x.reshape(bsz * seq, d)
    xb = xf.astype(BF16)
    kvp = None
    for layer in range(DEPTH):
        if layer < n_a:
            mix = _gla_mixer(xb, gla_w_in[layer], gla_w_gate2[layer], gla_b_gate2[layer],
                             gla_norm_g[layer], bsz, seq)
            w_out = gla_w_out[layer]
        else:
            if layer == n_a:
                kvp = _matmul_f32w(xb, shared_w_kv, shared_w_kv.shape[1], BF16)
            j = layer - n_a
            qp = _matmul_f32w(xb, sb_w_q[j], d, BF16)
            mix = _sb_attention(qp, kvp, bsz, seq)
            w_out = sb_w_out[j]
        xf = _matmul_ln(mix, w_out.astype(BF16), xf, ln1_g[layer], ln1_b[layer])
        xf, xb = _moe_block(xf, router_w[layer], router_b[layer], moe_w_gate_up, moe_b_gate_up,
                            moe_w_down, moe_b_down, layer, ln2_g[layer], ln2_b[layer])
    return xf.reshape(bsz, seq, d)
```

```python
import functools

import jax
import jax.numpy as jnp
from jax import lax
from jax.experimental import pallas as pl
from jax.experimental.pallas import tpu as pltpu

F32 = jnp.float32
BF16 = jnp.bfloat16

DEPTH = 2
CHUNK = 64
GLA_HEADS = 8
GLA_GATE_RANK = 16
GLA_TAU = 16.0
SB_HEADS = 32
N_EXPERTS = 32
TOP_K = 4
EXPERT_FF = 768
SWIGLU_LIMIT = 7.0
SWIGLU_ALPHA = 1.702
DN_ALPHA = (2.0 * DEPTH) ** 0.25
LN_EPS = 1e-5
RMS_EPS = 1e-6
LOG2_E = 1.4426950408889634

V7X_LANES = 128
V7X_VMEM_LIMIT_BYTES = 56 * 1024 * 1024

MM_BM = 512
MMW_BN = 1024
LN_BM = 512
LN_BK = 512
GLA_TS = 256
GLA_HB = 4
SB_TQ = 256
SB_HB = 8
ROUTER_BM = 512
MOE_TM = 256
MOE_FC = 384
COMBINE_TC = 128


def _params(semantics):
    return pltpu.CompilerParams(dimension_semantics=semantics,
                                vmem_limit_bytes=V7X_VMEM_LIMIT_BYTES)


def _mm_f32w_kernel(a_ref, w_ref, o_ref, w_bf, *, w_is_nk):
    @pl.when(pl.program_id(1) == 0)
    def _():
        w_bf[...] = w_ref[...].astype(BF16)

    contract = (((1,), (1 if w_is_nk else 0,)), ((), ()))
    o_ref[...] = lax.dot_general(a_ref[...], w_bf[...], contract,
                                 preferred_element_type=F32).astype(o_ref.dtype)


def _matmul_f32w(a, w, n, out_dtype, w_is_nk=False):
    m, k = a.shape
    bm = min(MM_BM, m)
    bn = min(MMW_BN, n)
    assert m % bm == 0 and n % bn == 0, (m, n, bm, bn)
    w_spec = (pl.BlockSpec((bn, k), lambda j, i: (j, 0)) if w_is_nk
              else pl.BlockSpec((k, bn), lambda j, i: (0, j)))
    return pl.pallas_call(
        functools.partial(_mm_f32w_kernel, w_is_nk=w_is_nk),
        out_shape=jax.ShapeDtypeStruct((m, n), out_dtype),
        grid=(n // bn, m // bm),
        in_specs=[pl.BlockSpec((bm, k), lambda j, i: (i, 0)), w_spec],
        out_specs=pl.BlockSpec((bm, bn), lambda j, i: (i, j)),
        scratch_shapes=[pltpu.VMEM((bn, k) if w_is_nk else (k, bn), BF16)],
        compiler_params=_params(("arbitrary", "arbitrary")),
        name="dense_matmul_f32w",
    )(a, w)


def _layer_norm_rows(y, g, b):
    mu = jnp.mean(y, axis=-1, keepdims=True)
    yc = y - mu
    var = jnp.mean(yc * yc, axis=-1, keepdims=True)
    return yc * lax.rsqrt(var + LN_EPS) * g + b


def _mm_ln_kernel(a_ref, w_ref, x_ref, g_ref, b_ref, o_ref, *, nk):
    k = pl.program_id(1)

    @pl.when(k == 0)
    def _():
        o_ref[...] = DN_ALPHA * x_ref[...]

    o_ref[...] += jnp.dot(a_ref[...], w_ref[...], preferred_element_type=F32)

    @pl.when(k == nk - 1)
    def _():
        o_ref[...] = _layer_norm_rows(o_ref[...], g_ref[...], b_ref[...])


def _matmul_ln(a, w, xres, g, b):
    m, k = a.shape
    n = w.shape[1]
    bm = min(LN_BM, m)
    bk = min(LN_BK, k)
    nk = k // bk
    return pl.pallas_call(
        functools.partial(_mm_ln_kernel, nk=nk),
        out_shape=jax.ShapeDtypeStruct((m, n), F32),
        grid=(m // bm, nk),
        in_specs=[pl.BlockSpec((bm, bk), lambda i, kk: (i, kk)),
                  pl.BlockSpec((bk, n), lambda i, kk: (kk, 0)),
                  pl.BlockSpec((bm, n), lambda i, kk: (i, 0)),
                  pl.BlockSpec((1, n), lambda i, kk: (0, 0)),
                  pl.BlockSpec((1, n), lambda i, kk: (0, 0))],
        out_specs=pl.BlockSpec((bm, n), lambda i, kk: (i, 0)),
        compiler_params=_params(("parallel", "arbitrary")),
        name="matmul_deepnorm",
    )(a, w, xres, g.reshape(1, n), b.reshape(1, n))


def _gla_kernel(q_ref, k_ref, v_ref, r_ref, gl_ref, wg_ref, bg_ref, ng_ref, o_ref, state_ref,
                *, ts, dk, dv, hb):
    s = pl.program_id(2)

    @pl.when(s == 0)
    def _():
        state_ref[...] = jnp.zeros_like(state_ref)

    row = lax.broadcasted_iota(jnp.int32, (ts, ts), 0)
    col = lax.broadcasted_iota(jnp.int32, (ts, ts), 1)
    shift = CHUNK.bit_length() - 1
    later = ((row >> shift) == (col >> shift)) & (col > row)
    later = jnp.where(later, 1.0, 0.0).astype(BF16)
    lanes = min(dv, V7X_LANES)
    ones = jnp.ones((CHUNK, lanes), BF16)
    tn = (((0,), (0,)), ((), ()))
    code = gl_ref[...].astype(BF16)

    las, k_dec = [], []
    for h in range(hb):
        kc = slice(h * dk, (h + 1) * dk)
        z = jnp.dot(code, wg_ref[:, kc], preferred_element_type=F32) + bg_ref[:, kc]
        la = -(jnp.maximum(-z, 0.0) + jnp.log(1.0 + jnp.exp(-jnp.abs(z)))) * (1.0 / GLA_TAU)
        la = la.astype(BF16)
        to_end = jnp.dot(later, la, preferred_element_type=F32)
        las.append(la)
        k_dec.append((k_ref[:, kc].astype(F32) * jnp.exp(to_end)).astype(BF16))

    for c in range(ts // CHUNK):
        sl = slice(c * CHUNK, (c + 1) * CHUNK)
        states = []
        for h in range(hb):
            vc = slice(h * dv, (h + 1) * dv)
            chunk_log = lax.dot_general(las[h][sl], ones, tn, preferred_element_type=F32)
            decay = jnp.exp(chunk_log)
            kv = lax.dot_general(k_dec[h][sl], v_ref[sl, vc], tn, preferred_element_type=F32)
            state = jnp.concatenate([decay] * (dv // lanes), axis=1) * state_ref[h] + kv
            state_ref[h] = state
            states.append(state.astype(BF16))
        for h in range(hb):
            kc = slice(h * dk, (h + 1) * dk)
            vc = slice(h * dv, (h + 1) * dv)
            o = jnp.dot(q_ref[sl, kc], states[h], preferred_element_type=F32) * (dk ** -0.5)
            o = o * lax.rsqrt(jnp.mean(o * o, axis=-1, keepdims=True) + RMS_EPS) * ng_ref[...]
            r = r_ref[sl, vc].astype(F32)
            o_ref[sl, vc] = (o * (r * jax.nn.sigmoid(r))).astype(o_ref.dtype)


def _gla_core(proj, glow, wg2, bg2, norm_g, bsz, seq):
    t = proj.shape[0]
    qk = wg2.shape[1]
    nh = GLA_HEADS
    dk = qk // nh
    dv = norm_g.shape[0]
    hb = GLA_HB
    ng = nh // hb
    ts = min(GLA_TS, seq)
    ns = seq // ts
    v_blk0 = 2 * qk // (hb * dv)
    r_blk0 = v_blk0 + ng
    gw = glow.shape[1]
    return pl.pallas_call(
        functools.partial(_gla_kernel, ts=ts, dk=dk, dv=dv, hb=hb),
        out_shape=jax.ShapeDtypeStruct((t, nh * dv), BF16),
        grid=(bsz, ng, ns),
        in_specs=[pl.BlockSpec((ts, hb * dk), lambda b, g, s: (b * ns + s, g)),
                  pl.BlockSpec((ts, hb * dk), lambda b, g, s: (b * ns + s, ng + g)),
                  pl.BlockSpec((ts, hb * dv), lambda b, g, s: (b * ns + s, v_blk0 + g)),
                  pl.BlockSpec((ts, hb * dv), lambda b, g, s: (b * ns + s, r_blk0 + g)),
                  pl.BlockSpec((ts, gw), lambda b, g, s: (b * ns + s, 0)),
                  pl.BlockSpec((gw, hb * dk), lambda b, g, s: (0, g)),
                  pl.BlockSpec((1, hb * dk), lambda b, g, s: (0, g)),
                  pl.BlockSpec((1, dv), lambda b, g, s: (0, 0))],
        out_specs=pl.BlockSpec((ts, hb * dv), lambda b, g, s: (b * ns + s, g)),
        scratch_shapes=[pltpu.VMEM((hb, dk, dv), F32)],
        compiler_params=_params(("parallel", "parallel", "arbitrary")),
        name="gla_chunk_scan",
    )(proj, proj, proj, proj, glow, wg2, bg2.reshape(1, qk), norm_g.reshape(1, dv))


def _gla_mixer(x_bf, w_in, w_gate2, b_gate2, norm_g, bsz, seq):
    qk = w_gate2.shape[1]
    n_main = w_in.shape[1] - GLA_GATE_RANK
    w_in_t = w_in.T
    w_code_t = jnp.pad(w_in_t[n_main:], ((0, V7X_LANES - GLA_GATE_RANK), (0, 0)))
    wg2 = jnp.pad(w_gate2, ((0, V7X_LANES - GLA_GATE_RANK), (0, 0))).astype(BF16)
    proj = _matmul_f32w(x_bf, w_in_t, n_main, BF16, w_is_nk=True)
    glow = _matmul_f32w(x_bf, w_code_t, V7X_LANES, F32, w_is_nk=True)
    return _gla_core(proj, glow, wg2, b_gate2, norm_g, bsz, seq)


def _sb_kernel(q_ref, k_ref, v_ref, o_ref, *, tq, dh, hb):
    i = pl.program_id(2)
    scale = dh ** -0.5
    row = lax.broadcasted_iota(jnp.int32, (tq, tq), 0)
    col = lax.broadcasted_iota(jnp.int32, (tq, tq), 1)
    suffix = jnp.where(row >= col, 1.0, 0.0).astype(BF16)
    causal = col < row
    nt = (((1,), (1,)), ((), ()))

    def block(jb, carry, masked):
        start = pl.multiple_of(jb * tq, tq)
        heads = [slice(h * dh, (h + 1) * dh) for h in range(hb)]
        zs = [lax.dot_general(q_ref[:, c], k_ref[pl.ds(start, tq), c], nt,
                              preferred_element_type=F32) * (scale * LOG2_E) for c in heads]
        tails = []
        for z in zs:
            sp = jnp.maximum(z, 0.0) + jnp.log2(1.0 + jnp.exp2(-jnp.abs(z)))
            if masked:
                sp = jnp.where(causal, sp, 0.0)
            tails.append(jnp.dot(sp.astype(BF16), suffix, preferred_element_type=F32))
        out = []
        for h, c in enumerate(heads):
            right, acc = carry[h]
            w = jnp.exp2(zs[h] - tails[h] - right)
            if masked:
                w = jnp.where(causal, w, 0.0)
            acc = acc + jnp.dot(w.astype(BF16), v_ref[pl.ds(start, tq), c],
                                preferred_element_type=F32)
            right = right + tails[h][:, :1]
            out.append((right, acc))
        return tuple(out)

    carry = tuple((jnp.zeros((tq, 1), F32), jnp.zeros((tq, dh), F32)) for _ in range(hb))
    carry = block(i, carry, True)
    carry = lax.fori_loop(0, i, lambda n, c: block(i - 1 - n, c, False), carry)
    for h in range(hb):
        o_ref[:, h * dh:(h + 1) * dh] = carry[h][1].astype(o_ref.dtype)


def _sb_attention(qp, kvp, bsz, seq):
    t, d = qp.shape
    nh = SB_HEADS
    dh = d // nh
    hb = SB_HB
    ng = nh // hb
    tq = min(SB_TQ, seq)
    nq = seq // tq
    return pl.pallas_call(
        functools.partial(_sb_kernel, tq=tq, dh=dh, hb=hb),
        out_shape=jax.ShapeDtypeStruct((t, d), BF16),
        grid=(bsz, ng, nq),
        in_specs=[pl.BlockSpec((tq, hb * dh), lambda b, g, i: (b * nq + i, g)),
                  pl.BlockSpec((seq, hb * dh), lambda b, g, i: (b, g)),
                  pl.BlockSpec((seq, hb * dh), lambda b, g, i: (b, ng + g))],
        out_specs=pl.BlockSpec((tq, hb * dh), lambda b, g, i: (b * nq + i, g)),
        compiler_params=_params(("parallel", "parallel", "arbitrary")),
        name="stick_breaking_attention",
    )(qp, kvp, kvp)


def _router_kernel(x_ref, w_ref, b_ref, idx_ref, gate_ref, rank_ref, cnt_ref, run_ref):
    i = pl.program_id(0)

    @pl.when(i == 0)
    def _():
        run_ref[...] = jnp.zeros_like(run_ref)

    x = x_ref[...]
    w = w_ref[...]
    x_hi = x.astype(BF16)
    x_lo = (x - x_hi.astype(F32)).astype(BF16)
    w_hi = w.astype(BF16)
    w_lo = (w - w_hi.astype(F32)).astype(BF16)
    logits = (jnp.dot(x_hi, w_hi, preferred_element_type=F32)
              + jnp.dot(x_lo, w_hi, preferred_element_type=F32)
              + jnp.dot(x_hi, w_lo, preferred_element_type=F32)) + b_ref[...]
    bm, n_e = logits.shape
    lane = lax.broadcasted_iota(jnp.int32, logits.shape, 1)
    vals = logits
    top_v, top_i = [], []
    for _ in range(TOP_K):
        m = jnp.max(vals, axis=-1, keepdims=True)
        sel = jnp.min(jnp.where(vals == m, lane, n_e), axis=-1, keepdims=True)
        top_v.append(m)
        top_i.append(sel)
        vals = jnp.where(lane == sel, -jnp.inf, vals)
    ex = [jnp.exp(v - top_v[0]) for v in top_v]
    denom = ex[0] + ex[1] + ex[2] + ex[3]

    chosen = lane == top_i[0]
    for kk in range(1, TOP_K):
        chosen = chosen | (lane == top_i[kk])
    hit = jnp.where(chosen, 1.0, 0.0)
    row = lax.broadcasted_iota(jnp.int32, (bm, bm), 0)
    col = lax.broadcasted_iota(jnp.int32, (bm, bm), 1)
    earlier = jnp.where(col < row, 1.0, 0.0).astype(BF16)
    before = jnp.dot(earlier, hit.astype(BF16), preferred_element_type=F32) + run_ref[...]
    for kk in range(TOP_K):
        idx_ref[:, kk:kk + 1] = top_i[kk]
        gate_ref[:, kk:kk + 1] = ex[kk] / denom
        rank = jnp.sum(jnp.where(lane == top_i[kk], before, 0.0), axis=-1, keepdims=True)
        rank_ref[:, kk:kk + 1] = rank.astype(jnp.int32)
    run_ref[...] += jnp.sum(hit, axis=0, keepdims=True)
    cnt_ref[...] = run_ref[...].astype(jnp.int32)


def _router(x, w_r, b_r):
    t, d = x.shape
    n_e = w_r.shape[1]
    bm = min(ROUTER_BM, t)
    return pl.pallas_call(
        _router_kernel,
        out_shape=(jax.ShapeDtypeStruct((t, TOP_K), jnp.int32),
                   jax.ShapeDtypeStruct((t, TOP_K), F32),
                   jax.ShapeDtypeStruct((t, TOP_K), jnp.int32),
                   jax.ShapeDtypeStruct((1, n_e), jnp.int32)),
        grid=(t // bm,),
        in_specs=[pl.BlockSpec((bm, d), lambda i: (i, 0)),
                  pl.BlockSpec((d, n_e), lambda i: (0, 0)),
                  pl.BlockSpec((1, n_e), lambda i: (0, 0))],
        out_specs=(pl.BlockSpec((bm, TOP_K), lambda i: (i, 0)),
                   pl.BlockSpec((bm, TOP_K), lambda i: (i, 0)),
                   pl.BlockSpec((bm, TOP_K), lambda i: (i, 0)),
                   pl.BlockSpec((1, n_e), lambda i: (0, 0))),
        scratch_shapes=[pltpu.VMEM((1, n_e), F32)],
        compiler_params=_params(("arbitrary",)),
        name="router_topk",
    )(x, w_r, b_r.reshape(1, n_e))


def _route_tables(idx, rank, counts, tm):
    t = idx.shape[0]
    n_e = counts.shape[0]
    padded = ((counts + tm - 1) // tm) * tm
    pend = jnp.cumsum(padded)
    pstart = pend - padded
    dest = (pstart[idx] + rank).astype(jnp.int32)
    n_rows = t * TOP_K + n_e * tm
    n_tiles = n_rows // tm
    tok = jnp.repeat(jnp.arange(t, dtype=jnp.int32), TOP_K)
    row_tok = (jnp.arange(n_rows, dtype=jnp.int32) % t).at[dest.reshape(-1)].set(tok)
    tile_end = pend // tm
    tile_id = jnp.arange(n_tiles, dtype=jnp.int32)
    tile_expert = jnp.minimum(jnp.sum(tile_end[None, :] <= tile_id[:, None], axis=1),
                              n_e - 1).astype(jnp.int32)
    n_used = tile_end[-1:].astype(jnp.int32)
    nonempty = (counts > 0).astype(jnp.int32)
    group_of_expert = jnp.cumsum(nonempty) - nonempty
    tile_group = group_of_expert[tile_expert].astype(jnp.int32)
    next_tile = tile_end[tile_expert]
    next_expert = jnp.where(next_tile < n_used[0],
                            tile_expert[jnp.minimum(next_tile, n_tiles - 1)], -1).astype(jnp.int32)
    n_groups = jnp.sum(nonempty, keepdims=True).astype(jnp.int32)
    sched = (tile_expert, next_expert, tile_group, n_groups, n_used)
    return dest, row_tok, sched, n_rows, n_tiles


def _dispatch_kernel(rt_ref, nu_ref, x_hbm, o_ref, buf, sem, *, tm):
    t = pl.program_id(0)
    n_used = nu_ref[0]

    def start(tile, slot):
        base = tile * tm

        def body(pair, c):
            for prio in range(2):
                r = pair * 2 + prio
                tok = rt_ref[base + r]
                pltpu.make_async_copy(x_hbm.at[pl.ds(tok, 1)], buf.at[slot, pl.ds(r, 1)],
                                      sem.at[slot]).start(priority=prio)
            return c

        lax.fori_loop(0, tm // 2, body, 0)

    @pl.when((t == 0) & (n_used > 0))
    def _():
        start(0, 0)

    @pl.when(t + 1 < n_used)
    def _():
        start(t + 1, (t + 1) % 2)

    @pl.when(t < n_used)
    def _():
        slot = t % 2
        pltpu.make_async_copy(x_hbm.at[pl.ds(0, tm)], buf.at[slot], sem.at[slot]).wait()
        o_ref[...] = buf[slot].astype(o_ref.dtype)

    @pl.when(t >= n_used)
    def _():
        o_ref[...] = jnp.zeros_like(o_ref)


def _dispatch(x, row_tok, n_used, tm, n_rows, n_tiles):
    d = x.shape[1]
    return pl.pallas_call(
        functools.partial(_dispatch_kernel, tm=tm),
        out_shape=jax.ShapeDtypeStruct((n_rows, d), BF16),
        grid_spec=pltpu.PrefetchScalarGridSpec(
            num_scalar_prefetch=2,
            grid=(n_tiles,),
            in_specs=[pl.BlockSpec(memory_space=pl.ANY)],
            out_specs=pl.BlockSpec((tm, d), lambda t, rt, nu: (t, 0)),
            scratch_shapes=[pltpu.VMEM((2, tm, d), F32), pltpu.SemaphoreType.DMA((2,))]),
        compiler_params=_params(("arbitrary",)),
        name="moe_dispatch_gather",
    )(row_tok, n_used, x)


def _group_start(te_ref, t):
    return (t == 0) | (te_ref[t] != te_ref[jnp.maximum(t - 1, 0)])


def _moe_up_kernel(te_ref, ne_ref, tg_ref, ng_ref, nu_ref, x_ref, w_hbm, bg_ref, bu_ref, o_ref,
                   stage, w_bf, sem, *, layer, fc, ff, nj):
    j = pl.program_id(0)
    t = pl.program_id(1)
    n_used = nu_ref[0]

    def weight_copies(jj, e, slot):
        gate_col = pl.multiple_of(jj * fc, V7X_LANES)
        up_col = pl.multiple_of(ff + jj * fc, V7X_LANES)
        return (pltpu.make_async_copy(w_hbm.at[layer, e, :, pl.ds(gate_col, fc)],
                                      stage.at[slot, 0], sem.at[slot, 0]),
                pltpu.make_async_copy(w_hbm.at[layer, e, :, pl.ds(up_col, fc)],
                                      stage.at[slot, 1], sem.at[slot, 1]))

    @pl.when((t < n_used) & _group_start(te_ref, t))
    def _():
        e = te_ref[t]
        slot = (j * ng_ref[0] + tg_ref[t]) % 2

        @pl.when((j == 0) & (t == 0))
        def _():
            for cp in weight_copies(j, e, slot):
                cp.start()

        nxt = ne_ref[t]
        more_here = nxt >= 0
        more_later = jnp.logical_not(more_here) & (j + 1 < nj)

        @pl.when(more_here | more_later)
        def _():
            for cp in weight_copies(jnp.where(more_here, j, j + 1),
                                    jnp.where(more_here, nxt, te_ref[0]), 1 - slot):
                cp.start()

        for cp in weight_copies(j, e, slot):
            cp.wait()
        w_bf[:, :fc] = stage[slot, 0].astype(BF16)
        w_bf[:, fc:] = stage[slot, 1].astype(BF16)

    @pl.when(t < n_used)
    def _():
        h = jnp.dot(x_ref[...], w_bf[...], preferred_element_type=F32)
        h_glu = h[:, :fc] + bg_ref[...]
        h_lin = h[:, fc:] + bu_ref[...]
        h_glu = jnp.minimum(h_glu, SWIGLU_LIMIT)
        h_lin = jnp.clip(h_lin, -SWIGLU_LIMIT, SWIGLU_LIMIT)
        act = h_glu * jax.nn.sigmoid(SWIGLU_ALPHA * h_glu) * (h_lin + 1.0)
        o_ref[...] = act.astype(o_ref.dtype)

    @pl.when(t >= n_used)
    def _():
        o_ref[...] = jnp.zeros_like(o_ref)


def _moe_up(xs, w_gu, b_gu, layer, sched, tm, n_tiles):
    n_rows, d = xs.shape
    n_l, n_e, _, ff2 = w_gu.shape
    ff = ff2 // 2
    fc = min(MOE_FC, ff)
    nj = ff // fc

    def row_map(j, t, te, ne, tg, ng, nu):
        return (jnp.maximum(jnp.minimum(t, nu[0] - 1), 0), 0)

    return pl.pallas_call(
        functools.partial(_moe_up_kernel, layer=layer, fc=fc, ff=ff, nj=nj),
        out_shape=jax.ShapeDtypeStruct((n_rows, ff), BF16),
        grid_spec=pltpu.PrefetchScalarGridSpec(
            num_scalar_prefetch=5,
            grid=(nj, n_tiles),
            in_specs=[pl.BlockSpec((tm, d), row_map),
                      pl.BlockSpec(memory_space=pl.ANY),
                      pl.BlockSpec((None, None, 1, fc),
                                   lambda j, t, te, ne, tg, ng, nu: (layer, te[t], 0, j)),
                      pl.BlockSpec((None, None, 1, fc),
                                   lambda j, t, te, ne, tg, ng, nu: (layer, te[t], 0, nj + j))],
            out_specs=pl.BlockSpec((tm, fc), lambda j, t, te, ne, tg, ng, nu: (t, j)),
            scratch_shapes=[pltpu.VMEM((2, 2, d, fc), F32),
                            pltpu.VMEM((d, 2 * fc), BF16),
                            pltpu.SemaphoreType.DMA((2, 2))]),
        compiler_params=_params(("arbitrary", "arbitrary")),
        name="moe_gate_up",
    )(*sched, xs, w_gu, b_gu.reshape(n_l, n_e, 1, ff2), b_gu.reshape(n_l, n_e, 1, ff2))


def _pack_bf16_pair(lo, hi):
    lo_bits = lax.bitcast_convert_type(lo.astype(BF16).astype(F32), jnp.int32)
    hi_bits = lax.bitcast_convert_type(hi.astype(BF16).astype(F32), jnp.int32)
    return hi_bits | lax.shift_right_logical(lo_bits, 16)


def _unpack_bf16_pair(packed):
    lo = lax.bitcast_convert_type(lax.shift_left(packed, 16), F32)
    hi = lax.bitcast_convert_type(packed & jnp.int32(-65536), F32)
    return lo, hi


def _moe_down_kernel(te_ref, ne_ref, tg_ref, ng_ref, nu_ref, a_ref, w_hbm, b_ref, o_ref,
                     stage, w_bf, sem, *, layer):
    t = pl.program_id(0)
    n_used = nu_ref[0]

    def weight_copy(e, slot):
        return pltpu.make_async_copy(w_hbm.at[layer, e], stage.at[slot], sem.at[slot])

    @pl.when((t < n_used) & _group_start(te_ref, t))
    def _():
        e = te_ref[t]
        slot = tg_ref[t] % 2

        @pl.when(t == 0)
        def _():
            weight_copy(e, slot).start()

        nxt = ne_ref[t]

        @pl.when(nxt >= 0)
        def _():
            weight_copy(nxt, 1 - slot).start()

        weight_copy(e, slot).wait()
        w_bf[...] = stage[slot].astype(BF16)

    @pl.when(t < n_used)
    def _():
        y = jnp.dot(a_ref[...], w_bf[...], preferred_element_type=F32) + b_ref[...]
        half = y.shape[1] // 2
        o_ref[...] = _pack_bf16_pair(y[:, :half], y[:, half:])

    @pl.when(t >= n_used)
    def _():
        o_ref[...] = jnp.zeros_like(o_ref)


def _moe_down(act, w_dn, b_dn, layer, sched, tm, n_tiles):
    n_rows, ff = act.shape
    n_l, n_e, _, d = w_dn.shape
    return pl.pallas_call(
        functools.partial(_moe_down_kernel, layer=layer),
        out_shape=jax.ShapeDtypeStruct((n_rows, d // 2), jnp.int32),
        grid_spec=pltpu.PrefetchScalarGridSpec(
            num_scalar_prefetch=5,
            grid=(n_tiles,),
            in_specs=[pl.BlockSpec((tm, ff), lambda t, te, ne, tg, ng, nu:
                                   (jnp.maximum(jnp.minimum(t, nu[0] - 1), 0), 0)),
                      pl.BlockSpec(memory_space=pl.ANY),
                      pl.BlockSpec((None, None, 1, d),
                                   lambda t, te, ne, tg, ng, nu: (layer, te[t], 0, 0))],
            out_specs=pl.BlockSpec((tm, d // 2), lambda t, te, ne, tg, ng, nu: (t, 0)),
            scratch_shapes=[pltpu.VMEM((2, ff, d), F32),
                            pltpu.VMEM((ff, d), BF16),
                            pltpu.SemaphoreType.DMA((2,))]),
        compiler_params=_params(("arbitrary",)),
        name="moe_down",
    )(*sched, act, w_dn, b_dn.reshape(n_l, n_e, 1, d))


def _combine_kernel(dest_ref, y_hbm, gate_ref, x_ref, g_ref, b_ref, of_ref, ob_ref, buf, sem,
                    *, tc, n_tiles):
    t = pl.program_id(0)

    def start(tile, slot):
        base = tile * tc * TOP_K

        def body(r, c):
            for kk in range(TOP_K):
                row = dest_ref[base + r * TOP_K + kk]
                pltpu.make_async_copy(y_hbm.at[pl.ds(row, 1)], buf.at[slot, kk, pl.ds(r, 1)],
                                      sem.at[slot]).start(priority=kk % 2)
            return c

        lax.fori_loop(0, tc, body, 0)

    @pl.when(t == 0)
    def _():
        start(0, 0)

    @pl.when(t + 1 < n_tiles)
    def _():
        start(t + 1, (t + 1) % 2)

    slot = t % 2
    for kk in range(TOP_K):
        pltpu.make_async_copy(y_hbm.at[pl.ds(0, tc)], buf.at[slot, kk], sem.at[slot]).wait()
    gates = gate_ref[...]
    lo, hi = _unpack_bf16_pair(buf[slot, 0])
    ffn_lo = gates[:, 0:1] * lo
    ffn_hi = gates[:, 0:1] * hi
    for kk in range(1, TOP_K):
        lo, hi = _unpack_bf16_pair(buf[slot, kk])
        ffn_lo = ffn_lo + gates[:, kk:kk + 1] * lo
        ffn_hi = ffn_hi + gates[:, kk:kk + 1] * hi
    ffn = jnp.concatenate([ffn_lo, ffn_hi], axis=1)
    out = _layer_norm_rows(DN_ALPHA * x_ref[...] + ffn, g_ref[...], b_ref[...])
    of_ref[...] = out
    ob_ref[...] = out.astype(ob_ref.dtype)


def _combine_ln(y, dest, gates, xres, g, b):
    t, d = xres.shape
    tc = min(COMBINE_TC, t)
    n_tiles = t // tc
    return pl.pallas_call(
        functools.partial(_combine_kernel, tc=tc, n_tiles=n_tiles),
        out_shape=(jax.ShapeDtypeStruct((t, d), F32), jax.ShapeDtypeStruct((t, d), BF16)),
        grid_spec=pltpu.PrefetchScalarGridSpec(
            num_scalar_prefetch=1,
            grid=(n_tiles,),
            in_specs=[pl.BlockSpec(memory_space=pl.ANY),
                      pl.BlockSpec((tc, TOP_K), lambda i, ds: (i, 0)),
                      pl.BlockSpec((tc, d), lambda i, ds: (i, 0)),
                      pl.BlockSpec((1, d), lambda i, ds: (0, 0)),
                      pl.BlockSpec((1, d), lambda i, ds: (0, 0))],
            out_specs=(pl.BlockSpec((tc, d), lambda i, ds: (i, 0)),
                       pl.BlockSpec((tc, d), lambda i, ds: (i, 0))),
            scratch_shapes=[pltpu.VMEM((2, TOP_K, tc, d // 2), jnp.int32),
                            pltpu.SemaphoreType.DMA((2,))]),
        compiler_params=_params(("arbitrary",)),
        name="moe_combine_deepnorm",
    )(dest.reshape(-1), y, gates, xres, g.reshape(1, d), b.reshape(1, d))


def _moe_block(x, w_r, b_r, w_gu, b_gu, w_dn, b_dn, layer, g, b):
    tm = MOE_TM
    idx, gates, rank, counts = _router(x, w_r, b_r)
    dest, row_tok, sched, n_rows, n_tiles = _route_tables(idx, rank, counts[0], tm)
    xs = _dispatch(x, row_tok, sched[-1], tm, n_rows, n_tiles)
    act = _moe_up(xs, w_gu, b_gu, layer, sched, tm, n_tiles)
    y = _moe_down(act, w_dn, b_dn, layer, sched, tm, n_tiles)
    return _combine_ln(y, dest, gates, x, g, b)


def kernel(x, gla_w_in, gla_w_gate2, gla_b_gate2, gla_norm_g, gla_w_out, sb_w_q, sb_w_out,
           shared_w_kv, router_w, router_b, moe_w_gate_up, moe_b_gate_up, moe_w_down, moe_b_down,
           ln1_g, ln1_b, ln2_g, ln2_b):
    bsz, seq, d = x.shape
    n_a = DEPTH // 2
    xf = x.reshape(bsz * seq, d)
    xb = xf.astype(BF16)
    kvp = None
    for layer in range(DEPTH):
        if layer < n_a:
            mix = _gla_mixer(xb, gla_w_in[layer], gla_w_gate2[layer], gla_b_gate2[layer],
                             gla_norm_g[layer], bsz, seq)
            w_out = gla_w_out[layer]
        else:
            if layer == n_a:
                kvp = _matmul_f32w(xb, shared_w_kv, shared_w_kv.shape[1], BF16)
            j = layer - n_a
            qp = _matmul_f32w(xb, sb_w_q[j], d, BF16)
            mix = _sb_attention(qp, kvp, bsz, seq)
            w_out = sb_w_out[j]
        xf = _matmul_ln(mix, w_out.astype(BF16), xf, ln1_g[layer], ln1_b[layer])
        xf, xb = _moe_block(xf, router_w[layer], router_b[layer], moe_w_gate_up, moe_b_gate_up,
                            moe_w_down, moe_b_down, layer, ln2_g[layer], ln2_b[layer])
    return xf.reshape(bsz, seq, d)
```

```python
import functools

import jax
import jax.numpy as jnp
from jax import lax
from jax.experimental import pallas as pl
from jax.experimental.pallas import tpu as pltpu

F32 = jnp.float32
BF16 = jnp.bfloat16

DEPTH = 2
CHUNK = 64
GLA_HEADS = 8
GLA_GATE_RANK = 16
GLA_TAU = 16.0
SB_HEADS = 32
N_EXPERTS = 32
TOP_K = 4
EXPERT_FF = 768
SWIGLU_LIMIT = 7.0
SWIGLU_ALPHA = 1.702
DN_ALPHA = (2.0 * DEPTH) ** 0.25
LN_EPS = 1e-5
RMS_EPS = 1e-6
LOG2_E = 1.4426950408889634

V7X_LANES = 128
V7X_VMEM_LIMIT_BYTES = 56 * 1024 * 1024

MM_BM = 512
MMW_BN = 1024
LN_BM = 512
LN_BK = 512
GLA_TS = 256
GLA_HB = 4
SB_TQ = 256
SB_HB = 8
SB_STRIP = 32
ROUTER_BM = 512
MOE_TM = 256
MOE_FC = 384
COMBINE_TC = 128


def _params(semantics):
    return pltpu.CompilerParams(dimension_semantics=semantics,
                                vmem_limit_bytes=V7X_VMEM_LIMIT_BYTES)


def _mm_f32w_kernel(a_ref, w_ref, o_ref, w_bf, *, w_is_nk):
    @pl.when(pl.program_id(1) == 0)
    def _():
        w_bf[...] = w_ref[...].astype(BF16)

    contract = (((1,), (1 if w_is_nk else 0,)), ((), ()))
    o_ref[...] = lax.dot_general(a_ref[...], w_bf[...], contract,
                                 preferred_element_type=F32).astype(o_ref.dtype)


def _matmul_f32w(a, w, n, out_dtype, w_is_nk=False):
    m, k = a.shape
    bm = min(MM_BM, m)
    bn = min(MMW_BN, n)
    assert m % bm == 0 and n % bn == 0, (m, n, bm, bn)
    w_spec = (pl.BlockSpec((bn, k), lambda j, i: (j, 0)) if w_is_nk
              else pl.BlockSpec((k, bn), lambda j, i: (0, j)))
    return pl.pallas_call(
        functools.partial(_mm_f32w_kernel, w_is_nk=w_is_nk),
        out_shape=jax.ShapeDtypeStruct((m, n), out_dtype),
        grid=(n // bn, m // bm),
        in_specs=[pl.BlockSpec((bm, k), lambda j, i: (i, 0)), w_spec],
        out_specs=pl.BlockSpec((bm, bn), lambda j, i: (i, j)),
        scratch_shapes=[pltpu.VMEM((bn, k) if w_is_nk else (k, bn), BF16)],
        compiler_params=_params(("arbitrary", "arbitrary")),
        name="dense_matmul_f32w",
    )(a, w)


def _layer_norm_rows(y, g, b):
    mu = jnp.mean(y, axis=-1, keepdims=True)
    yc = y - mu
    var = jnp.mean(yc * yc, axis=-1, keepdims=True)
    return yc * lax.rsqrt(var + LN_EPS) * g + b


def _mm_ln_kernel(a_ref, w_ref, x_ref, g_ref, b_ref, o_ref, *, nk):
    k = pl.program_id(1)

    @pl.when(k == 0)
    def _():
        o_ref[...] = DN_ALPHA * x_ref[...]

    o_ref[...] += jnp.dot(a_ref[...], w_ref[...], preferred_element_type=F32)

    @pl.when(k == nk - 1)
    def _():
        o_ref[...] = _layer_norm_rows(o_ref[...], g_ref[...], b_ref[...])


def _matmul_ln(a, w, xres, g, b):
    m, k = a.shape
    n = w.shape[1]
    bm = min(LN_BM, m)
    bk = min(LN_BK, k)
    nk = k // bk
    return pl.pallas_call(
        functools.partial(_mm_ln_kernel, nk=nk),
        out_shape=jax.ShapeDtypeStruct((m, n), F32),
        grid=(m // bm, nk),
        in_specs=[pl.BlockSpec((bm, bk), lambda i, kk: (i, kk)),
                  pl.BlockSpec((bk, n), lambda i, kk: (kk, 0)),
                  pl.BlockSpec((bm, n), lambda i, kk: (i, 0)),
                  pl.BlockSpec((1, n), lambda i, kk: (0, 0)),
                  pl.BlockSpec((1, n), lambda i, kk: (0, 0))],
        out_specs=pl.BlockSpec((bm, n), lambda i, kk: (i, 0)),
        compiler_params=_params(("parallel", "arbitrary")),
        name="matmul_deepnorm",
    )(a, w, xres, g.reshape(1, n), b.reshape(1, n))


def _gla_kernel(q_ref, k_ref, v_ref, r_ref, gl_ref, wg_ref, bg_ref, ng_ref, o_ref, state_ref,
                *, ts, dk, dv, hb):
    s = pl.program_id(2)

    @pl.when(s == 0)
    def _():
        state_ref[...] = jnp.zeros_like(state_ref)

    row = lax.broadcasted_iota(jnp.int32, (ts, ts), 0)
    col = lax.broadcasted_iota(jnp.int32, (ts, ts), 1)
    shift = CHUNK.bit_length() - 1
    later = ((row >> shift) == (col >> shift)) & (col > row)
    later = jnp.where(later, 1.0, 0.0).astype(BF16)
    lanes = min(dv, V7X_LANES)
    ones = jnp.ones((CHUNK, lanes), BF16)
    tn = (((0,), (0,)), ((), ()))
    code = gl_ref[...].astype(BF16)

    las, k_dec = [], []
    for h in range(hb):
        kc = slice(h * dk, (h + 1) * dk)
        z = jnp.dot(code, wg_ref[:, kc], preferred_element_type=F32) + bg_ref[:, kc]
        la = -(jnp.maximum(-z, 0.0) + jnp.log(1.0 + jnp.exp(-jnp.abs(z)))) * (1.0 / GLA_TAU)
        la = la.astype(BF16)
        to_end = jnp.dot(later, la, preferred_element_type=F32)
        las.append(la)
        k_dec.append((k_ref[:, kc].astype(F32) * jnp.exp(to_end)).astype(BF16))

    for c in range(ts // CHUNK):
        sl = slice(c * CHUNK, (c + 1) * CHUNK)
        states = []
        for h in range(hb):
            vc = slice(h * dv, (h + 1) * dv)
            chunk_log = lax.dot_general(las[h][sl], ones, tn, preferred_element_type=F32)
            decay = jnp.exp(chunk_log)
            kv = lax.dot_general(k_dec[h][sl], v_ref[sl, vc], tn, preferred_element_type=F32)
            state = jnp.concatenate([decay] * (dv // lanes), axis=1) * state_ref[h] + kv
            state_ref[h] = state
            states.append(state.astype(BF16))
        for h in range(hb):
            kc = slice(h * dk, (h + 1) * dk)
            vc = slice(h * dv, (h + 1) * dv)
            o = jnp.dot(q_ref[sl, kc], states[h], preferred_element_type=F32) * (dk ** -0.5)
            o = o * lax.rsqrt(jnp.mean(o * o, axis=-1, keepdims=True) + RMS_EPS) * ng_ref[...]
            r = r_ref[sl, vc].astype(F32)
            o_ref[sl, vc] = (o * (r * jax.nn.sigmoid(r))).astype(o_ref.dtype)


def _gla_core(proj, glow, wg2, bg2, norm_g, bsz, seq):
    t = proj.shape[0]
    qk = wg2.shape[1]
    nh = GLA_HEADS
    dk = qk // nh
    dv = norm_g.shape[0]
    hb = GLA_HB
    ng = nh // hb
    ts = min(GLA_TS, seq)
    ns = seq // ts
    v_blk0 = 2 * qk // (hb * dv)
    r_blk0 = v_blk0 + ng
    gw = glow.shape[1]
    return pl.pallas_call(
        functools.partial(_gla_kernel, ts=ts, dk=dk, dv=dv, hb=hb),
        out_shape=jax.ShapeDtypeStruct((t, nh * dv), BF16),
        grid=(bsz, ng, ns),
        in_specs=[pl.BlockSpec((ts, hb * dk), lambda b, g, s: (b * ns + s, g)),
                  pl.BlockSpec((ts, hb * dk), lambda b, g, s: (b * ns + s, ng + g)),
                  pl.BlockSpec((ts, hb * dv), lambda b, g, s: (b * ns + s, v_blk0 + g)),
                  pl.BlockSpec((ts, hb * dv), lambda b, g, s: (b * ns + s, r_blk0 + g)),
                  pl.BlockSpec((ts, gw), lambda b, g, s: (b * ns + s, 0)),
                  pl.BlockSpec((gw, hb * dk), lambda b, g, s: (0, g)),
                  pl.BlockSpec((1, hb * dk), lambda b, g, s: (0, g)),
                  pl.BlockSpec((1, dv), lambda b, g, s: (0, 0))],
        out_specs=pl.BlockSpec((ts, hb * dv), lambda b, g, s: (b * ns + s, g)),
        scratch_shapes=[pltpu.VMEM((hb, dk, dv), F32)],
        compiler_params=_params(("parallel", "parallel", "arbitrary")),
        name="gla_chunk_scan",
    )(proj, proj, proj, proj, glow, wg2, bg2.reshape(1, qk), norm_g.reshape(1, dv))


def _gla_mixer(x_bf, w_in, w_gate2, b_gate2, norm_g, bsz, seq):
    qk = w_gate2.shape[1]
    n_main = w_in.shape[1] - GLA_GATE_RANK
    w_in_t = w_in.T
    w_code_t = jnp.pad(w_in_t[n_main:], ((0, V7X_LANES - GLA_GATE_RANK), (0, 0)))
    wg2 = jnp.pad(w_gate2, ((0, V7X_LANES - GLA_GATE_RANK), (0, 0))).astype(BF16)
    proj = _matmul_f32w(x_bf, w_in_t, n_main, BF16, w_is_nk=True)
    glow = _matmul_f32w(x_bf, w_code_t, V7X_LANES, F32, w_is_nk=True)
    return _gla_core(proj, glow, wg2, b_gate2, norm_g, bsz, seq)


def _sb_kernel(q_ref, k_ref, v_ref, o_ref, z_scr, sp_scr, tail_scr, w_scr, right_scr, acc_scr,
               *, tq, dh, hb, rs):
    i = pl.program_id(2)
    scale = dh ** -0.5
    row = lax.broadcasted_iota(jnp.int32, (tq, tq), 0)
    col = lax.broadcasted_iota(jnp.int32, (tq, tq), 1)
    suffix = jnp.where(row >= col, 1.0, 0.0).astype(BF16)
    nt = (((1,), (1,)), ((), ()))
    heads = [slice(h * dh, (h + 1) * dh) for h in range(hb)]
    strips = [slice(r0, r0 + rs) for r0 in range(0, tq, rs)]

    def causal(rows):
        r = lax.broadcasted_iota(jnp.int32, (rs, tq), 0) + rows.start
        c = lax.broadcasted_iota(jnp.int32, (rs, tq), 1)
        return c < r

    right_scr[...] = jnp.zeros_like(right_scr)
    acc_scr[...] = jnp.zeros_like(acc_scr)

    def block(jb, masked):
        start = pl.multiple_of(jb * tq, tq)
        for h, c in enumerate(heads):
            z_scr[h] = lax.dot_general(q_ref[:, c], k_ref[pl.ds(start, tq), c], nt,
                                       preferred_element_type=F32) * (scale * LOG2_E)
        for h in range(hb):
            for rows in strips:
                z = z_scr[h, rows, :]
                sp = jnp.maximum(z, 0.0) + jnp.log2(1.0 + jnp.exp2(-jnp.abs(z)))
                if masked:
                    sp = jnp.where(causal(rows), sp, 0.0)
                sp_scr[h, rows, :] = sp.astype(BF16)
            tail_scr[h] = jnp.dot(sp_scr[h], suffix, preferred_element_type=F32)
        for h, c in enumerate(heads):
            for rows in strips:
                w = jnp.exp2(z_scr[h, rows, :] - tail_scr[h, rows, :] - right_scr[h, rows, :])
                if masked:
                    w = jnp.where(causal(rows), w, 0.0)
                w_scr[h, rows, :] = w.astype(BF16)
            acc_scr[h] += jnp.dot(w_scr[h], v_ref[pl.ds(start, tq), c],
                                  preferred_element_type=F32)
            right_scr[h] += tail_scr[h, :, :1]

    block(i, True)

    def body(n, carry):
        block(i - 1 - n, False)
        return carry

    lax.fori_loop(0, i, body, 0)
    for h, c in enumerate(heads):
        o_ref[:, c] = acc_scr[h].astype(o_ref.dtype)


def _sb_attention(qp, kvp, bsz, seq):
    t, d = qp.shape
    nh = SB_HEADS
    dh = d // nh
    hb = SB_HB
    ng = nh // hb
    tq = min(SB_TQ, seq)
    nq = seq // tq
    rs = min(SB_STRIP, tq)
    return pl.pallas_call(
        functools.partial(_sb_kernel, tq=tq, dh=dh, hb=hb, rs=rs),
        out_shape=jax.ShapeDtypeStruct((t, d), BF16),
        grid=(bsz, ng, nq),
        in_specs=[pl.BlockSpec((tq, hb * dh), lambda b, g, i: (b * nq + i, g)),
                  pl.BlockSpec((seq, hb * dh), lambda b, g, i: (b, g)),
                  pl.BlockSpec((seq, hb * dh), lambda b, g, i: (b, ng + g))],
        out_specs=pl.BlockSpec((tq, hb * dh), lambda b, g, i: (b * nq + i, g)),
        scratch_shapes=[pltpu.VMEM((hb, tq, tq), F32),
                        pltpu.VMEM((hb, tq, tq), BF16),
                        pltpu.VMEM((hb, tq, tq), F32),
                        pltpu.VMEM((hb, tq, tq), BF16),
                        pltpu.VMEM((hb, tq, 1), F32),
                        pltpu.VMEM((hb, tq, dh), F32)],
        compiler_params=_params(("parallel", "parallel", "arbitrary")),
        name="stick_breaking_attention",
    )(qp, kvp, kvp)


def _router_kernel(x_ref, w_ref, b_ref, idx_ref, gate_ref, rank_ref, cnt_ref, run_ref):
    i = pl.program_id(0)

    @pl.when(i == 0)
    def _():
        run_ref[...] = jnp.zeros_like(run_ref)

    x = x_ref[...]
    w = w_ref[...]
    x_hi = x.astype(BF16)
    x_lo = (x - x_hi.astype(F32)).astype(BF16)
    w_hi = w.astype(BF16)
    w_lo = (w - w_hi.astype(F32)).astype(BF16)
    logits = (jnp.dot(x_hi, w_hi, preferred_element_type=F32)
              + jnp.dot(x_lo, w_hi, preferred_element_type=F32)
              + jnp.dot(x_hi, w_lo, preferred_element_type=F32)) + b_ref[...]
    bm, n_e = logits.shape
    lane = lax.broadcasted_iota(jnp.int32, logits.shape, 1)
    vals = logits
    top_v, top_i = [], []
    for _ in range(TOP_K):
        m = jnp.max(vals, axis=-1, keepdims=True)
        sel = jnp.min(jnp.where(vals == m, lane, n_e), axis=-1, keepdims=True)
        top_v.append(m)
        top_i.append(sel)
        vals = jnp.where(lane == sel, -jnp.inf, vals)
    ex = [jnp.exp(v - top_v[0]) for v in top_v]
    denom = ex[0] + ex[1] + ex[2] + ex[3]

    chosen = lane == top_i[0]
    for kk in range(1, TOP_K):
        chosen = chosen | (lane == top_i[kk])
    hit = jnp.where(chosen, 1.0, 0.0)
    row = lax.broadcasted_iota(jnp.int32, (bm, bm), 0)
    col = lax.broadcasted_iota(jnp.int32, (bm, bm), 1)
    earlier = jnp.where(col < row, 1.0, 0.0).astype(BF16)
    before = jnp.dot(earlier, hit.astype(BF16), preferred_element_type=F32) + run_ref[...]
    for kk in range(TOP_K):
        idx_ref[:, kk:kk + 1] = top_i[kk]
        gate_ref[:, kk:kk + 1] = ex[kk] / denom
        rank = jnp.sum(jnp.where(lane == top_i[kk], before, 0.0), axis=-1, keepdims=True)
        rank_ref[:, kk:kk + 1] = rank.astype(jnp.int32)
    run_ref[...] += jnp.sum(hit, axis=0, keepdims=True)
    cnt_ref[...] = run_ref[...].astype(jnp.int32)


def _router(x, w_r, b_r):
    t, d = x.shape
    n_e = w_r.shape[1]
    bm = min(ROUTER_BM, t)
    return pl.pallas_call(
        _router_kernel,
        out_shape=(jax.ShapeDtypeStruct((t, TOP_K), jnp.int32),
                   jax.ShapeDtypeStruct((t, TOP_K), F32),
                   jax.ShapeDtypeStruct((t, TOP_K), jnp.int32),
                   jax.ShapeDtypeStruct((1, n_e), jnp.int32)),
        grid=(t // bm,),
        in_specs=[pl.BlockSpec((bm, d), lambda i: (i, 0)),
                  pl.BlockSpec((d, n_e), lambda i: (0, 0)),
                  pl.BlockSpec((1, n_e), lambda i: (0, 0))],
        out_specs=(pl.BlockSpec((bm, TOP_K), lambda i: (i, 0)),
                   pl.BlockSpec((bm, TOP_K), lambda i: (i, 0)),
                   pl.BlockSpec((bm, TOP_K), lambda i: (i, 0)),
                   pl.BlockSpec((1, n_e), lambda i: (0, 0))),
        scratch_shapes=[pltpu.VMEM((1, n_e), F32)],
        compiler_params=_params(("arbitrary",)),
        name="router_topk",
    )(x, w_r, b_r.reshape(1, n_e))


def _route_tables(idx, rank, counts, tm):
    t = idx.shape[0]
    n_e = counts.shape[0]
    padded = ((counts + tm - 1) // tm) * tm
    pend = jnp.cumsum(padded)
    pstart = pend - padded
    dest = (pstart[idx] + rank).astype(jnp.int32)
    n_rows = t * TOP_K + n_e * tm
    n_tiles = n_rows // tm
    tok = jnp.repeat(jnp.arange(t, dtype=jnp.int32), TOP_K)
    row_tok = (jnp.arange(n_rows, dtype=jnp.int32) % t).at[dest.reshape(-1)].set(tok)
    tile_end = pend // tm
    tile_id = jnp.arange(n_tiles, dtype=jnp.int32)
    tile_expert = jnp.minimum(jnp.sum(tile_end[None, :] <= tile_id[:, None], axis=1),
                              n_e - 1).astype(jnp.int32)
    n_used = tile_end[-1:].astype(jnp.int32)
    nonempty = (counts > 0).astype(jnp.int32)
    group_of_expert = jnp.cumsum(nonempty) - nonempty
    tile_group = group_of_expert[tile_expert].astype(jnp.int32)
    next_tile = tile_end[tile_expert]
    next_expert = jnp.where(next_tile < n_used[0],
                            tile_expert[jnp.minimum(next_tile, n_tiles - 1)], -1).astype(jnp.int32)
    n_groups = jnp.sum(nonempty, keepdims=True).astype(jnp.int32)
    sched = (tile_expert, next_expert, tile_group, n_groups, n_used)
    return dest, row_tok, sched, n_rows, n_tiles


def _dispatch_kernel(rt_ref, nu_ref, x_hbm, o_ref, buf, sem, *, tm):
    t = pl.program_id(0)
    n_used = nu_ref[0]

    def start(tile, slot):
        base = tile * tm

        def body(pair, c):
            for prio in range(2):
                r = pair * 2 + prio
                tok = rt_ref[base + r]
                pltpu.make_async_copy(x_hbm.at[pl.ds(tok, 1)], buf.at[slot, pl.ds(r, 1)],
                                      sem.at[slot]).start(priority=prio)
            return c

        lax.fori_loop(0, tm // 2, body, 0)

    @pl.when((t == 0) & (n_used > 0))
    def _():
        start(0, 0)

    @pl.when(t + 1 < n_used)
    def _():
        start(t + 1, (t + 1) % 2)

    @pl.when(t < n_used)
    def _():
        slot = t % 2
        pltpu.make_async_copy(x_hbm.at[pl.ds(0, tm)], buf.at[slot], sem.at[slot]).wait()
        o_ref[...] = buf[slot].astype(o_ref.dtype)

    @pl.when(t >= n_used)
    def _():
        o_ref[...] = jnp.zeros_like(o_ref)


def _dispatch(x, row_tok, n_used, tm, n_rows, n_tiles):
    d = x.shape[1]
    return pl.pallas_call(
        functools.partial(_dispatch_kernel, tm=tm),
        out_shape=jax.ShapeDtypeStruct((n_rows, d), BF16),
        grid_spec=pltpu.PrefetchScalarGridSpec(
            num_scalar_prefetch=2,
            grid=(n_tiles,),
            in_specs=[pl.BlockSpec(memory_space=pl.ANY)],
            out_specs=pl.BlockSpec((tm, d), lambda t, rt, nu: (t, 0)),
            scratch_shapes=[pltpu.VMEM((2, tm, d), F32), pltpu.SemaphoreType.DMA((2,))]),
        compiler_params=_params(("arbitrary",)),
        name="moe_dispatch_gather",
    )(row_tok, n_used, x)


def _group_start(te_ref, t):
    return (t == 0) | (te_ref[t] != te_ref[jnp.maximum(t - 1, 0)])


def _moe_up_kernel(te_ref, ne_ref, tg_ref, ng_ref, nu_ref, x_ref, w_hbm, bg_ref, bu_ref, o_ref,
                   stage, w_bf, sem, *, layer, fc, ff, nj):
    j = pl.program_id(0)
    t = pl.program_id(1)
    n_used = nu_ref[0]

    def weight_copies(jj, e, slot):
        gate_col = pl.multiple_of(jj * fc, V7X_LANES)
        up_col = pl.multiple_of(ff + jj * fc, V7X_LANES)
        return (pltpu.make_async_copy(w_hbm.at[layer, e, :, pl.ds(gate_col, fc)],
                                      stage.at[slot, 0], sem.at[slot, 0]),
                pltpu.make_async_copy(w_hbm.at[layer, e, :, pl.ds(up_col, fc)],
                                      stage.at[slot, 1], sem.at[slot, 1]))

    @pl.when((t < n_used) & _group_start(te_ref, t))
    def _():
        e = te_ref[t]
        slot = (j * ng_ref[0] + tg_ref[t]) % 2

        @pl.when((j == 0) & (t == 0))
        def _():
            for cp in weight_copies(j, e, slot):
                cp.start()

        nxt = ne_ref[t]
        more_here = nxt >= 0
        more_later = jnp.logical_not(more_here) & (j + 1 < nj)

        @pl.when(more_here | more_later)
        def _():
            for cp in weight_copies(jnp.where(more_here, j, j + 1),
                                    jnp.where(more_here, nxt, te_ref[0]), 1 - slot):
                cp.start()

        for cp in weight_copies(j, e, slot):
            cp.wait()
        w_bf[:, :fc] = stage[slot, 0].astype(BF16)
        w_bf[:, fc:] = stage[slot, 1].astype(BF16)

    @pl.when(t < n_used)
    def _():
        h = jnp.dot(x_ref[...], w_bf[...], preferred_element_type=F32)
        h_glu = h[:, :fc] + bg_ref[...]
        h_lin = h[:, fc:] + bu_ref[...]
        h_glu = jnp.minimum(h_glu, SWIGLU_LIMIT)
        h_lin = jnp.clip(h_lin, -SWIGLU_LIMIT, SWIGLU_LIMIT)
        act = h_glu * jax.nn.sigmoid(SWIGLU_ALPHA * h_glu) * (h_lin + 1.0)
        o_ref[...] = act.astype(o_ref.dtype)

    @pl.when(t >= n_used)
    def _():
        o_ref[...] = jnp.zeros_like(o_ref)


def _moe_up(xs, w_gu, b_gu, layer, sched, tm, n_tiles):
    n_rows, d = xs.shape
    n_l, n_e, _, ff2 = w_gu.shape
    ff = ff2 // 2
    fc = min(MOE_FC, ff)
    nj = ff // fc

    def row_map(j, t, te, ne, tg, ng, nu):
        return (jnp.maximum(jnp.minimum(t, nu[0] - 1), 0), 0)

    return pl.pallas_call(
        functools.partial(_moe_up_kernel, layer=layer, fc=fc, ff=ff, nj=nj),
        out_shape=jax.ShapeDtypeStruct((n_rows, ff), BF16),
        grid_spec=pltpu.PrefetchScalarGridSpec(
            num_scalar_prefetch=5,
            grid=(nj, n_tiles),
            in_specs=[pl.BlockSpec((tm, d), row_map),
                      pl.BlockSpec(memory_space=pl.ANY),
                      pl.BlockSpec((None, None, 1, fc),
                                   lambda j, t, te, ne, tg, ng, nu: (layer, te[t], 0, j)),
                      pl.BlockSpec((None, None, 1, fc),
                                   lambda j, t, te, ne, tg, ng, nu: (layer, te[t], 0, nj + j))],
            out_specs=pl.BlockSpec((tm, fc), lambda j, t, te, ne, tg, ng, nu: (t, j)),
            scratch_shapes=[pltpu.VMEM((2, 2, d, fc), F32),
                            pltpu.VMEM((d, 2 * fc), BF16),
                            pltpu.SemaphoreType.DMA((2, 2))]),
        compiler_params=_params(("arbitrary", "arbitrary")),
        name="moe_gate_up",
    )(*sched, xs, w_gu, b_gu.reshape(n_l, n_e, 1, ff2), b_gu.reshape(n_l, n_e, 1, ff2))


def _pack_bf16_pair(lo, hi):
    lo_bits = lax.bitcast_convert_type(lo.astype(BF16).astype(F32), jnp.int32)
    hi_bits = lax.bitcast_convert_type(hi.astype(BF16).astype(F32), jnp.int32)
    return hi_bits | lax.shift_right_logical(lo_bits, 16)


def _unpack_bf16_pair(packed):
    lo = lax.bitcast_convert_type(lax.shift_left(packed, 16), F32)
    hi = lax.bitcast_convert_type(packed & jnp.int32(-65536), F32)
    return lo, hi


def _moe_down_kernel(te_ref, ne_ref, tg_ref, ng_ref, nu_ref, a_ref, w_hbm, b_ref, o_ref,
                     stage, w_bf, sem, *, layer):
    t = pl.program_id(0)
    n_used = nu_ref[0]

    def weight_copy(e, slot):
        return pltpu.make_async_copy(w_hbm.at[layer, e], stage.at[slot], sem.at[slot])

    @pl.when((t < n_used) & _group_start(te_ref, t))
    def _():
        e = te_ref[t]
        slot = tg_ref[t] % 2

        @pl.when(t == 0)
        def _():
            weight_copy(e, slot).start()

        nxt = ne_ref[t]

        @pl.when(nxt >= 0)
        def _():
            weight_copy(nxt, 1 - slot).start()

        weight_copy(e, slot).wait()
        w_bf[...] = stage[slot].astype(BF16)

    @pl.when(t < n_used)
    def _():
        y = jnp.dot(a_ref[...], w_bf[...], preferred_element_type=F32) + b_ref[...]
        half = y.shape[1] // 2
        o_ref[...] = _pack_bf16_pair(y[:, :half], y[:, half:])

    @pl.when(t >= n_used)
    def _():
        o_ref[...] = jnp.zeros_like(o_ref)


def _moe_down(act, w_dn, b_dn, layer, sched, tm, n_tiles):
    n_rows, ff = act.shape
    n_l, n_e, _, d = w_dn.shape
    return pl.pallas_call(
        functools.partial(_moe_down_kernel, layer=layer),
        out_shape=jax.ShapeDtypeStruct((n_rows, d // 2), jnp.int32),
        grid_spec=pltpu.PrefetchScalarGridSpec(
            num_scalar_prefetch=5,
            grid=(n_tiles,),
            in_specs=[pl.BlockSpec((tm, ff), lambda t, te, ne, tg, ng, nu:
                                   (jnp.maximum(jnp.minimum(t, nu[0] - 1), 0), 0)),
                      pl.BlockSpec(memory_space=pl.ANY),
                      pl.BlockSpec((None, None, 1, d),
                                   lambda t, te, ne, tg, ng, nu: (layer, te[t], 0, 0))],
            out_specs=pl.BlockSpec((tm, d // 2), lambda t, te, ne, tg, ng, nu: (t, 0)),
            scratch_shapes=[pltpu.VMEM((2, ff, d), F32),
                            pltpu.VMEM((ff, d), BF16),
                            pltpu.SemaphoreType.DMA((2,))]),
        compiler_params=_params(("arbitrary",)),
        name="moe_down",
    )(*sched, act, w_dn, b_dn.reshape(n_l, n_e, 1, d))


def _combine_kernel(dest_ref, y_hbm, gate_ref, x_ref, g_ref, b_ref, of_ref, ob_ref, buf, sem,
                    *, tc, n_tiles):
    t = pl.program_id(0)

    def start(tile, slot):
        base = tile * tc * TOP_K

        def body(r, c):
            for kk in range(TOP_K):
                row = dest_ref[base + r * TOP_K + kk]
                pltpu.make_async_copy(y_hbm.at[pl.ds(row, 1)], buf.at[slot, kk, pl.ds(r, 1)],
                                      sem.at[slot]).start(priority=kk % 2)
            return c

        lax.fori_loop(0, tc, body, 0)

    @pl.when(t == 0)
    def _():
        start(0, 0)

    @pl.when(t + 1 < n_tiles)
    def _():
        start(t + 1, (t + 1) % 2)

    slot = t % 2
    for kk in range(TOP_K):
        pltpu.make_async_copy(y_hbm.at[pl.ds(0, tc)], buf.at[slot, kk], sem.at[slot]).wait()
    gates = gate_ref[...]
    lo, hi = _unpack_bf16_pair(buf[slot, 0])
    ffn_lo = gates[:, 0:1] * lo
    ffn_hi = gates[:, 0:1] * hi
    for kk in range(1, TOP_K):
        lo, hi = _unpack_bf16_pair(buf[slot, kk])
        ffn_lo = ffn_lo + gates[:, kk:kk + 1] * lo
        ffn_hi = ffn_hi + gates[:, kk:kk + 1] * hi
    ffn = jnp.concatenate([ffn_lo, ffn_hi], axis=1)
    out = _layer_norm_rows(DN_ALPHA * x_ref[...] + ffn, g_ref[...], b_ref[...])
    of_ref[...] = out
    ob_ref[...] = out.astype(ob_ref.dtype)


def _combine_ln(y, dest, gates, xres, g, b):
    t, d = xres.shape
    tc = min(COMBINE_TC, t)
    n_tiles = t // tc
    return pl.pallas_call(
        functools.partial(_combine_kernel, tc=tc, n_tiles=n_tiles),
        out_shape=(jax.ShapeDtypeStruct((t, d), F32), jax.ShapeDtypeStruct((t, d), BF16)),
        grid_spec=pltpu.PrefetchScalarGridSpec(
            num_scalar_prefetch=1,
            grid=(n_tiles,),
            in_specs=[pl.BlockSpec(memory_space=pl.ANY),
                      pl.BlockSpec((tc, TOP_K), lambda i, ds: (i, 0)),
                      pl.BlockSpec((tc, d), lambda i, ds: (i, 0)),
                      pl.BlockSpec((1, d), lambda i, ds: (0, 0)),
                      pl.BlockSpec((1, d), lambda i, ds: (0, 0))],
            out_specs=(pl.BlockSpec((tc, d), lambda i, ds: (i, 0)),
                       pl.BlockSpec((tc, d), lambda i, ds: (i, 0))),
            scratch_shapes=[pltpu.VMEM((2, TOP_K, tc, d // 2), jnp.int32),
                            pltpu.SemaphoreType.DMA((2,))]),
        compiler_params=_params(("arbitrary",)),
        name="moe_combine_deepnorm",
    )(dest.reshape(-1), y, gates, xres, g.reshape(1, d), b.reshape(1, d))


def _moe_block(x, w_r, b_r, w_gu, b_gu, w_dn, b_dn, layer, g, b):
    tm = MOE_TM
    idx, gates, rank, counts = _router(x, w_r, b_r)
    dest, row_tok, sched, n_rows, n_tiles = _route_tables(idx, rank, counts[0], tm)
    xs = _dispatch(x, row_tok, sched[-1], tm, n_rows, n_tiles)
    act = _moe_up(xs, w_gu, b_gu, layer, sched, tm, n_tiles)
    y = _moe_down(act, w_dn, b_dn, layer, sched, tm, n_tiles)
    return _combine_ln(y, dest, gates, x, g, b)


def kernel(x, gla_w_in, gla_w_gate2, gla_b_gate2, gla_norm_g, gla_w_out, sb_w_q, sb_w_out,
           shared_w_kv, router_w, router_b, moe_w_gate_up, moe_b_gate_up, moe_w_down, moe_b_down,
           ln1_g, ln1_b, ln2_g, ln2_b):
    bsz, seq, d = x.shape
    n_a = DEPTH // 2
    xf = x.reshape(bsz * seq, d)
    xb = xf.astype(BF16)
    kvp = None
    for layer in range(DEPTH):
        if layer < n_a:
            mix = _gla_mixer(xb, gla_w_in[layer], gla_w_gate2[layer], gla_b_gate2[layer],
                             gla_norm_g[layer], bsz, seq)
            w_out = gla_w_out[layer]
        else:
            if layer == n_a:
                kvp = _matmul_f32w(xb, shared_w_kv, shared_w_kv.shape[1], BF16)
            j = layer - n_a
            qp = _matmul_f32w(xb, sb_w_q[j], d, BF16)
            mix = _sb_attention(qp, kvp, bsz, seq)
            w_out = sb_w_out[j]
        xf = _matmul_ln(mix, w_out.astype(BF16), xf, ln1_g[layer], ln1_b[layer])
        xf, xb = _moe_block(xf, router_w[layer], router_b[layer], moe_w_gate_up, moe_b_gate_up,
                            moe_w_down, moe_b_down, layer, ln2_g[layer], ln2_b[layer])
    return xf.reshape(bsz, seq, d)
```

```python
import functools

import jax
import jax.numpy as jnp
from jax import lax
from jax.experimental import pallas as pl
from jax.experimental.pallas import tpu as pltpu

F32 = jnp.float32
BF16 = jnp.bfloat16

DEPTH = 2
CHUNK = 64
GLA_HEADS = 8
GLA_GATE_RANK = 16
GLA_TAU = 16.0
SB_HEADS = 32
N_EXPERTS = 32
TOP_K = 4
EXPERT_FF = 768
SWIGLU_LIMIT = 7.0
SWIGLU_ALPHA = 1.702
DN_ALPHA = (2.0 * DEPTH) ** 0.25
LN_EPS = 1e-5
RMS_EPS = 1e-6
LOG2_E = 1.4426950408889634

V7X_LANES = 128
V7X_VMEM_LIMIT_BYTES = 56 * 1024 * 1024

MM_BM = 512
MMW_BN = 1024
LN_BM = 512
LN_BK = 512
GLA_TS = 256
GLA_HB = 4
SB_TQ = 256
SB_HB = 8
SB_STRIP = 32
ROUTER_BM = 512
MOE_TM = 256
COMBINE_TC = 128


def _params(semantics):
    return pltpu.CompilerParams(dimension_semantics=semantics,
                                vmem_limit_bytes=V7X_VMEM_LIMIT_BYTES)


def _mm_f32w_kernel(a_ref, w_ref, o_ref, w_bf, *, w_is_nk):
    @pl.when(pl.program_id(1) == 0)
    def _():
        w_bf[...] = w_ref[...].astype(BF16)

    contract = (((1,), (1 if w_is_nk else 0,)), ((), ()))
    o_ref[...] = lax.dot_general(a_ref[...], w_bf[...], contract,
                                 preferred_element_type=F32).astype(o_ref.dtype)


def _matmul_f32w(a, w, n, out_dtype, w_is_nk=False):
    m, k = a.shape
    bm = min(MM_BM, m)
    bn = min(MMW_BN, n)
    assert m % bm == 0 and n % bn == 0, (m, n, bm, bn)
    w_spec = (pl.BlockSpec((bn, k), lambda j, i: (j, 0)) if w_is_nk
              else pl.BlockSpec((k, bn), lambda j, i: (0, j)))
    return pl.pallas_call(
        functools.partial(_mm_f32w_kernel, w_is_nk=w_is_nk),
        out_shape=jax.ShapeDtypeStruct((m, n), out_dtype),
        grid=(n // bn, m // bm),
        in_specs=[pl.BlockSpec((bm, k), lambda j, i: (i, 0)), w_spec],
        out_specs=pl.BlockSpec((bm, bn), lambda j, i: (i, j)),
        scratch_shapes=[pltpu.VMEM((bn, k) if w_is_nk else (k, bn), BF16)],
        compiler_params=_params(("arbitrary", "arbitrary")),
        name="dense_matmul_f32w",
    )(a, w)


def _layer_norm_rows(y, g, b):
    mu = jnp.mean(y, axis=-1, keepdims=True)
    yc = y - mu
    var = jnp.mean(yc * yc, axis=-1, keepdims=True)
    return yc * lax.rsqrt(var + LN_EPS) * g + b


def _mm_ln_kernel(a_ref, w_ref, x_ref, g_ref, b_ref, o_ref, *, nk):
    k = pl.program_id(1)

    @pl.when(k == 0)
    def _():
        o_ref[...] = DN_ALPHA * x_ref[...]

    o_ref[...] += jnp.dot(a_ref[...], w_ref[...], preferred_element_type=F32)

    @pl.when(k == nk - 1)
    def _():
        o_ref[...] = _layer_norm_rows(o_ref[...], g_ref[...], b_ref[...])


def _matmul_ln(a, w, xres, g, b):
    m, k = a.shape
    n = w.shape[1]
    bm = min(LN_BM, m)
    bk = min(LN_BK, k)
    nk = k // bk
    return pl.pallas_call(
        functools.partial(_mm_ln_kernel, nk=nk),
        out_shape=jax.ShapeDtypeStruct((m, n), F32),
        grid=(m // bm, nk),
        in_specs=[pl.BlockSpec((bm, bk), lambda i, kk: (i, kk)),
                  pl.BlockSpec((bk, n), lambda i, kk: (kk, 0)),
                  pl.BlockSpec((bm, n), lambda i, kk: (i, 0)),
                  pl.BlockSpec((1, n), lambda i, kk: (0, 0)),
                  pl.BlockSpec((1, n), lambda i, kk: (0, 0))],
        out_specs=pl.BlockSpec((bm, n), lambda i, kk: (i, 0)),
        compiler_params=_params(("parallel", "arbitrary")),
        name="matmul_deepnorm",
    )(a, w, xres, g.reshape(1, n), b.reshape(1, n))


def _gla_kernel(q_ref, k_ref, v_ref, r_ref, gl_ref, wg_ref, bg_ref, ng_ref, o_ref, state_ref,
                *, ts, dk, dv, hb):
    s = pl.program_id(2)

    @pl.when(s == 0)
    def _():
        state_ref[...] = jnp.zeros_like(state_ref)

    row = lax.broadcasted_iota(jnp.int32, (ts, ts), 0)
    col = lax.broadcasted_iota(jnp.int32, (ts, ts), 1)
    shift = CHUNK.bit_length() - 1
    later = ((row >> shift) == (col >> shift)) & (col > row)
    later = jnp.where(later, 1.0, 0.0).astype(BF16)
    lanes = min(dv, V7X_LANES)
    ones = jnp.ones((CHUNK, lanes), BF16)
    tn = (((0,), (0,)), ((), ()))
    code = gl_ref[...].astype(BF16)

    las, k_dec = [], []
    for h in range(hb):
        kc = slice(h * dk, (h + 1) * dk)
        z = jnp.dot(code, wg_ref[:, kc], preferred_element_type=F32) + bg_ref[:, kc]
        la = -(jnp.maximum(-z, 0.0) + jnp.log(1.0 + jnp.exp(-jnp.abs(z)))) * (1.0 / GLA_TAU)
        la = la.astype(BF16)
        to_end = jnp.dot(later, la, preferred_element_type=F32)
        las.append(la)
        k_dec.append((k_ref[:, kc].astype(F32) * jnp.exp(to_end)).astype(BF16))

    for c in range(ts // CHUNK):
        sl = slice(c * CHUNK, (c + 1) * CHUNK)
        states = []
        for h in range(hb):
            vc = slice(h * dv, (h + 1) * dv)
            chunk_log = lax.dot_general(las[h][sl], ones, tn, preferred_element_type=F32)
            decay = jnp.exp(chunk_log)
            kv = lax.dot_general(k_dec[h][sl], v_ref[sl, vc], tn, preferred_element_type=F32)
            state = jnp.concatenate([decay] * (dv // lanes), axis=1) * state_ref[h] + kv
            state_ref[h] = state
            states.append(state.astype(BF16))
        for h in range(hb):
            kc = slice(h * dk, (h + 1) * dk)
            vc = slice(h * dv, (h + 1) * dv)
            o = jnp.dot(q_ref[sl, kc], states[h], preferred_element_type=F32) * (dk ** -0.5)
            o = o * lax.rsqrt(jnp.mean(o * o, axis=-1, keepdims=True) + RMS_EPS) * ng_ref[...]
            r = r_ref[sl, vc].astype(F32)
            o_ref[sl, vc] = (o * (r * jax.nn.sigmoid(r))).astype(o_ref.dtype)


def _gla_core(proj, glow, wg2, bg2, norm_g, bsz, seq):
    t = proj.shape[0]
    qk = wg2.shape[1]
    nh = GLA_HEADS
    dk = qk // nh
    dv = norm_g.shape[0]
    hb = GLA_HB
    ng = nh // hb
    ts = min(GLA_TS, seq)
    ns = seq // ts
    v_blk0 = 2 * qk // (hb * dv)
    r_blk0 = v_blk0 + ng
    gw = glow.shape[1]
    return pl.pallas_call(
        functools.partial(_gla_kernel, ts=ts, dk=dk, dv=dv, hb=hb),
        out_shape=jax.ShapeDtypeStruct((t, nh * dv), BF16),
        grid=(bsz, ng, ns),
        in_specs=[pl.BlockSpec((ts, hb * dk), lambda b, g, s: (b * ns + s, g)),
                  pl.BlockSpec((ts, hb * dk), lambda b, g, s: (b * ns + s, ng + g)),
                  pl.BlockSpec((ts, hb * dv), lambda b, g, s: (b * ns + s, v_blk0 + g)),
                  pl.BlockSpec((ts, hb * dv), lambda b, g, s: (b * ns + s, r_blk0 + g)),
                  pl.BlockSpec((ts, gw), lambda b, g, s: (b * ns + s, 0)),
                  pl.BlockSpec((gw, hb * dk), lambda b, g, s: (0, g)),
                  pl.BlockSpec((1, hb * dk), lambda b, g, s: (0, g)),
                  pl.BlockSpec((1, dv), lambda b, g, s: (0, 0))],
        out_specs=pl.BlockSpec((ts, hb * dv), lambda b, g, s: (b * ns + s, g)),
        scratch_shapes=[pltpu.VMEM((hb, dk, dv), F32)],
        compiler_params=_params(("parallel", "parallel", "arbitrary")),
        name="gla_chunk_scan",
    )(proj, proj, proj, proj, glow, wg2, bg2.reshape(1, qk), norm_g.reshape(1, dv))


def _gla_mixer(x_bf, w_in, w_gate2, b_gate2, norm_g, bsz, seq):
    qk = w_gate2.shape[1]
    n_main = w_in.shape[1] - GLA_GATE_RANK
    w_in_t = w_in.T
    w_code_t = jnp.pad(w_in_t[n_main:], ((0, V7X_LANES - GLA_GATE_RANK), (0, 0)))
    wg2 = jnp.pad(w_gate2, ((0, V7X_LANES - GLA_GATE_RANK), (0, 0))).astype(BF16)
    proj = _matmul_f32w(x_bf, w_in_t, n_main, BF16, w_is_nk=True)
    glow = _matmul_f32w(x_bf, w_code_t, V7X_LANES, F32, w_is_nk=True)
    return _gla_core(proj, glow, wg2, b_gate2, norm_g, bsz, seq)


def _sb_kernel(q_ref, k_ref, v_ref, o_ref, z_scr, sp_scr, tail_scr, w_scr, right_scr, acc_scr,
               *, tq, dh, hb, rs):
    i = pl.program_id(2)
    scale = dh ** -0.5
    row = lax.broadcasted_iota(jnp.int32, (tq, tq), 0)
    col = lax.broadcasted_iota(jnp.int32, (tq, tq), 1)
    suffix = jnp.where(row >= col, 1.0, 0.0).astype(BF16)
    nt = (((1,), (1,)), ((), ()))
    heads = [slice(h * dh, (h + 1) * dh) for h in range(hb)]
    strips = [slice(r0, r0 + rs) for r0 in range(0, tq, rs)]

    def causal(rows):
        r = lax.broadcasted_iota(jnp.int32, (rs, tq), 0) + rows.start
        c = lax.broadcasted_iota(jnp.int32, (rs, tq), 1)
        return c < r

    right_scr[...] = jnp.zeros_like(right_scr)
    acc_scr[...] = jnp.zeros_like(acc_scr)

    def block(jb, masked):
        start = pl.multiple_of(jb * tq, tq)
        for h, c in enumerate(heads):
            z_scr[h] = lax.dot_general(q_ref[:, c], k_ref[pl.ds(start, tq), c], nt,
                                       preferred_element_type=F32) * (scale * LOG2_E)
        for h in range(hb):
            for rows in strips:
                z = z_scr[h, rows, :]
                sp = jnp.maximum(z, 0.0) + jnp.log2(1.0 + jnp.exp2(-jnp.abs(z)))
                if masked:
                    sp = jnp.where(causal(rows), sp, 0.0)
                sp_scr[h, rows, :] = sp.astype(BF16)
            tail_scr[h] = jnp.dot(sp_scr[h], suffix, preferred_element_type=F32)
        for h, c in enumerate(heads):
            for rows in strips:
                w = jnp.exp2(z_scr[h, rows, :] - tail_scr[h, rows, :] - right_scr[h, rows, :])
                if masked:
                    w = jnp.where(causal(rows), w, 0.0)
                w_scr[h, rows, :] = w.astype(BF16)
            acc_scr[h] += jnp.dot(w_scr[h], v_ref[pl.ds(start, tq), c],
                                  preferred_element_type=F32)
            right_scr[h] += tail_scr[h, :, :1]

    block(i, True)

    def body(n, carry):
        block(i - 1 - n, False)
        return carry

    lax.fori_loop(0, i, body, 0)
    for h, c in enumerate(heads):
        o_ref[:, c] = acc_scr[h].astype(o_ref.dtype)


def _sb_attention(qp, kvp, bsz, seq):
    t, d = qp.shape
    nh = SB_HEADS
    dh = d // nh
    hb = SB_HB
    ng = nh // hb
    tq = min(SB_TQ, seq)
    nq = seq // tq
    rs = min(SB_STRIP, tq)
    return pl.pallas_call(
        functools.partial(_sb_kernel, tq=tq, dh=dh, hb=hb, rs=rs),
        out_shape=jax.ShapeDtypeStruct((t, d), BF16),
        grid=(bsz, ng, nq),
        in_specs=[pl.BlockSpec((tq, hb * dh), lambda b, g, i: (b * nq + i, g)),
                  pl.BlockSpec((seq, hb * dh), lambda b, g, i: (b, g)),
                  pl.BlockSpec((seq, hb * dh), lambda b, g, i: (b, ng + g))],
        out_specs=pl.BlockSpec((tq, hb * dh), lambda b, g, i: (b * nq + i, g)),
        scratch_shapes=[pltpu.VMEM((hb, tq, tq), F32),
                        pltpu.VMEM((hb, tq, tq), BF16),
                        pltpu.VMEM((hb, tq, tq), F32),
                        pltpu.VMEM((hb, tq, tq), BF16),
                        pltpu.VMEM((hb, tq, 1), F32),
                        pltpu.VMEM((hb, tq, dh), F32)],
        compiler_params=_params(("parallel", "parallel", "arbitrary")),
        name="stick_breaking_attention",
    )(qp, kvp, kvp)


def _router_kernel(x_ref, w_ref, b_ref, idx_ref, gate_ref, rank_ref, cnt_ref, run_ref):
    i = pl.program_id(0)

    @pl.when(i == 0)
    def _():
        run_ref[...] = jnp.zeros_like(run_ref)

    x = x_ref[...]
    w = w_ref[...]
    x_hi = x.astype(BF16)
    x_lo = (x - x_hi.astype(F32)).astype(BF16)
    w_hi = w.astype(BF16)
    w_lo = (w - w_hi.astype(F32)).astype(BF16)
    logits = (jnp.dot(x_hi, w_hi, preferred_element_type=F32)
              + jnp.dot(x_lo, w_hi, preferred_element_type=F32)
              + jnp.dot(x_hi, w_lo, preferred_element_type=F32)) + b_ref[...]
    bm, n_e = logits.shape
    lane = lax.broadcasted_iota(jnp.int32, logits.shape, 1)
    vals = logits
    top_v, top_i = [], []
    for _ in range(TOP_K):
        m = jnp.max(vals, axis=-1, keepdims=True)
        sel = jnp.min(jnp.where(vals == m, lane, n_e), axis=-1, keepdims=True)
        top_v.append(m)
        top_i.append(sel)
        vals = jnp.where(lane == sel, -jnp.inf, vals)
    ex = [jnp.exp(v - top_v[0]) for v in top_v]
    denom = ex[0] + ex[1] + ex[2] + ex[3]

    chosen = lane == top_i[0]
    for kk in range(1, TOP_K):
        chosen = chosen | (lane == top_i[kk])
    hit = jnp.where(chosen, 1.0, 0.0)
    row = lax.broadcasted_iota(jnp.int32, (bm, bm), 0)
    col = lax.broadcasted_iota(jnp.int32, (bm, bm), 1)
    earlier = jnp.where(col < row, 1.0, 0.0).astype(BF16)
    before = jnp.dot(earlier, hit.astype(BF16), preferred_element_type=F32) + run_ref[...]
    for kk in range(TOP_K):
        idx_ref[:, kk:kk + 1] = top_i[kk]
        gate_ref[:, kk:kk + 1] = ex[kk] / denom
        rank = jnp.sum(jnp.where(lane == top_i[kk], before, 0.0), axis=-1, keepdims=True)
        rank_ref[:, kk:kk + 1] = rank.astype(jnp.int32)
    run_ref[...] += jnp.sum(hit, axis=0, keepdims=True)
    cnt_ref[...] = run_ref[...].astype(jnp.int32)


def _router(x, w_r, b_r):
    t, d = x.shape
    n_e = w_r.shape[1]
    bm = min(ROUTER_BM, t)
    return pl.pallas_call(
        _router_kernel,
        out_shape=(jax.ShapeDtypeStruct((t, TOP_K), jnp.int32),
                   jax.ShapeDtypeStruct((t, TOP_K), F32),
                   jax.ShapeDtypeStruct((t, TOP_K), jnp.int32),
                   jax.ShapeDtypeStruct((1, n_e), jnp.int32)),
        grid=(t // bm,),
        in_specs=[pl.BlockSpec((bm, d), lambda i: (i, 0)),
                  pl.BlockSpec((d, n_e), lambda i: (0, 0)),
                  pl.BlockSpec((1, n_e), lambda i: (0, 0))],
        out_specs=(pl.BlockSpec((bm, TOP_K), lambda i: (i, 0)),
                   pl.BlockSpec((bm, TOP_K), lambda i: (i, 0)),
                   pl.BlockSpec((bm, TOP_K), lambda i: (i, 0)),
                   pl.BlockSpec((1, n_e), lambda i: (0, 0))),
        scratch_shapes=[pltpu.VMEM((1, n_e), F32)],
        compiler_params=_params(("arbitrary",)),
        name="router_topk",
    )(x, w_r, b_r.reshape(1, n_e))


def _route_tables(idx, rank, counts, tm):
    t = idx.shape[0]
    n_e = counts.shape[0]
    padded = ((counts + tm - 1) // tm) * tm
    pend = jnp.cumsum(padded)
    pstart = pend - padded
    dest = (pstart[idx] + rank).astype(jnp.int32)
    n_rows = t * TOP_K + n_e * tm
    n_tiles = n_rows // tm
    tok = jnp.repeat(jnp.arange(t, dtype=jnp.int32), TOP_K)
    row_tok = (jnp.arange(n_rows, dtype=jnp.int32) % t).at[dest.reshape(-1)].set(tok)
    tile_end = pend // tm
    tile_id = jnp.arange(n_tiles, dtype=jnp.int32)
    tile_expert = jnp.minimum(jnp.sum(tile_end[None, :] <= tile_id[:, None], axis=1),
                              n_e - 1).astype(jnp.int32)
    n_used = tile_end[-1:].astype(jnp.int32)
    nonempty = (counts > 0).astype(jnp.int32)
    group_of_expert = jnp.cumsum(nonempty) - nonempty
    tile_group = group_of_expert[tile_expert].astype(jnp.int32)
    next_tile = tile_end[tile_expert]
    next_expert = jnp.where(next_tile < n_used[0],
                            tile_expert[jnp.minimum(next_tile, n_tiles - 1)], -1).astype(jnp.int32)
    sched = (tile_expert, next_expert, tile_group, n_used)
    return dest, row_tok, sched, n_rows, n_tiles


def _dispatch_kernel(rt_ref, nu_ref, x_hbm, o_ref, buf, sem, *, tm):
    t = pl.program_id(0)
    n_used = nu_ref[0]

    def start(tile, slot):
        base = tile * tm

        def body(pair, c):
            for prio in range(2):
                r = pair * 2 + prio
                tok = rt_ref[base + r]
                pltpu.make_async_copy(x_hbm.at[pl.ds(tok, 1)], buf.at[slot, pl.ds(r, 1)],
                                      sem.at[slot]).start(priority=prio)
            return c

        lax.fori_loop(0, tm // 2, body, 0)

    @pl.when((t == 0) & (n_used > 0))
    def _():
        start(0, 0)

    @pl.when(t + 1 < n_used)
    def _():
        start(t + 1, (t + 1) % 2)

    @pl.when(t < n_used)
    def _():
        slot = t % 2
        pltpu.make_async_copy(x_hbm.at[pl.ds(0, tm)], buf.at[slot], sem.at[slot]).wait()
        o_ref[...] = buf[slot].astype(o_ref.dtype)

    @pl.when(t >= n_used)
    def _():
        o_ref[...] = jnp.zeros_like(o_ref)


def _dispatch(x, row_tok, n_used, tm, n_rows, n_tiles):
    d = x.shape[1]
    return pl.pallas_call(
        functools.partial(_dispatch_kernel, tm=tm),
        out_shape=jax.ShapeDtypeStruct((n_rows, d), BF16),
        grid_spec=pltpu.PrefetchScalarGridSpec(
            num_scalar_prefetch=2,
            grid=(n_tiles,),
            in_specs=[pl.BlockSpec(memory_space=pl.ANY)],
            out_specs=pl.BlockSpec((tm, d), lambda t, rt, nu: (t, 0)),
            scratch_shapes=[pltpu.VMEM((2, tm, d), F32), pltpu.SemaphoreType.DMA((2,))]),
        compiler_params=_params(("arbitrary",)),
        name="moe_dispatch_gather",
    )(row_tok, n_used, x)


def _group_start(te_ref, t):
    return (t == 0) | (te_ref[t] != te_ref[jnp.maximum(t - 1, 0)])


def _moe_up_kernel(te_ref, ne_ref, tg_ref, nu_ref, x_ref, w_hbm, b_ref, o_ref,
                   stage, w_bf, sem, *, layer, ff):
    t = pl.program_id(0)
    n_used = nu_ref[0]

    def weight_copy(e):
        return pltpu.make_async_copy(w_hbm.at[layer, e], stage, sem.at[0])

    @pl.when((t == 0) & (n_used > 0))
    def _():
        weight_copy(te_ref[0]).start()

    @pl.when((t < n_used) & _group_start(te_ref, t))
    def _():
        weight_copy(te_ref[t]).wait()
        w_bf[...] = stage[...].astype(BF16)
        nxt = ne_ref[t]

        @pl.when(nxt >= 0)
        def _():
            weight_copy(nxt).start()

    @pl.when(t < n_used)
    def _():
        h = jnp.dot(x_ref[...], w_bf[...], preferred_element_type=F32) + b_ref[...]
        h_glu = h[:, :ff]
        h_lin = h[:, ff:]
        h_glu = jnp.minimum(h_glu, SWIGLU_LIMIT)
        h_lin = jnp.clip(h_lin, -SWIGLU_LIMIT, SWIGLU_LIMIT)
        act = h_glu * jax.nn.sigmoid(SWIGLU_ALPHA * h_glu) * (h_lin + 1.0)
        o_ref[...] = act.astype(o_ref.dtype)

    @pl.when(t >= n_used)
    def _():
        o_ref[...] = jnp.zeros_like(o_ref)


def _moe_up(xs, w_gu, b_gu, layer, sched, tm, n_tiles):
    n_rows, d = xs.shape
    n_l, n_e, _, ff2 = w_gu.shape
    ff = ff2 // 2

    def row_map(t, te, ne, tg, nu):
        return (jnp.maximum(jnp.minimum(t, nu[0] - 1), 0), 0)

    return pl.pallas_call(
        functools.partial(_moe_up_kernel, layer=layer, ff=ff),
        out_shape=jax.ShapeDtypeStruct((n_rows, ff), BF16),
        grid_spec=pltpu.PrefetchScalarGridSpec(
            num_scalar_prefetch=4,
            grid=(n_tiles,),
            in_specs=[pl.BlockSpec((tm, d), row_map),
                      pl.BlockSpec(memory_space=pl.ANY),
                      pl.BlockSpec((None, None, 1, ff2),
                                   lambda t, te, ne, tg, nu: (layer, te[t], 0, 0))],
            out_specs=pl.BlockSpec((tm, ff), lambda t, te, ne, tg, nu: (t, 0)),
            scratch_shapes=[pltpu.VMEM((d, ff2), F32),
                            pltpu.VMEM((d, ff2), BF16),
                            pltpu.SemaphoreType.DMA((1,))]),
        compiler_params=_params(("arbitrary",)),
        name="moe_gate_up",
    )(*sched, xs, w_gu, b_gu.reshape(n_l, n_e, 1, ff2))


def _pack_bf16_pair(lo, hi):
    lo_bits = lax.bitcast_convert_type(lo.astype(BF16).astype(F32), jnp.int32)
    hi_bits = lax.bitcast_convert_type(hi.astype(BF16).astype(F32), jnp.int32)
    return hi_bits | lax.shift_right_logical(lo_bits, 16)


def _unpack_bf16_pair(packed):
    lo = lax.bitcast_convert_type(lax.shift_left(packed, 16), F32)
    hi = lax.bitcast_convert_type(packed & jnp.int32(-65536), F32)
    return lo, hi


def _moe_down_kernel(te_ref, ne_ref, tg_ref, nu_ref, a_ref, w_hbm, b_ref, o_ref,
                     stage, w_bf, sem, *, layer):
    t = pl.program_id(0)
    n_used = nu_ref[0]

    def weight_copy(e, slot):
        return pltpu.make_async_copy(w_hbm.at[layer, e], stage.at[slot], sem.at[slot])

    @pl.when((t < n_used) & _group_start(te_ref, t))
    def _():
        e = te_ref[t]
        slot = tg_ref[t] % 2

        @pl.when(t == 0)
        def _():
            weight_copy(e, slot).start()

        nxt = ne_ref[t]

        @pl.when(nxt >= 0)
        def _():
            weight_copy(nxt, 1 - slot).start()

        weight_copy(e, slot).wait()
        w_bf[...] = stage[slot].astype(BF16)

    @pl.when(t < n_used)
    def _():
        y = jnp.dot(a_ref[...], w_bf[...], preferred_element_type=F32) + b_ref[...]
        half = y.shape[1] // 2
        o_ref[...] = _pack_bf16_pair(y[:, :half], y[:, half:])

    @pl.when(t >= n_used)
    def _():
        o_ref[...] = jnp.zeros_like(o_ref)


def _moe_down(act, w_dn, b_dn, layer, sched, tm, n_tiles):
    n_rows, ff = act.shape
    n_l, n_e, _, d = w_dn.shape
    return pl.pallas_call(
        functools.partial(_moe_down_kernel, layer=layer),
        out_shape=jax.ShapeDtypeStruct((n_rows, d // 2), jnp.int32),
        grid_spec=pltpu.PrefetchScalarGridSpec(
            num_scalar_prefetch=4,
            grid=(n_tiles,),
            in_specs=[pl.BlockSpec((tm, ff), lambda t, te, ne, tg, nu:
                                   (jnp.maximum(jnp.minimum(t, nu[0] - 1), 0), 0)),
                      pl.BlockSpec(memory_space=pl.ANY),
                      pl.BlockSpec((None, None, 1, d),
                                   lambda t, te, ne, tg, nu: (layer, te[t], 0, 0))],
            out_specs=pl.BlockSpec((tm, d // 2), lambda t, te, ne, tg, nu: (t, 0)),
            scratch_shapes=[pltpu.VMEM((2, ff, d), F32),
                            pltpu.VMEM((ff, d), BF16),
                            pltpu.SemaphoreType.DMA((2,))]),
        compiler_params=_params(("arbitrary",)),
        name="moe_down",
    )(*sched, act, w_dn, b_dn.reshape(n_l, n_e, 1, d))


def _combine_kernel(dest_ref, y_hbm, gate_ref, x_ref, g_ref, b_ref, of_ref, ob_ref, buf, sem,
                    *, tc, n_tiles):
    t = pl.program_id(0)

    def start(tile, slot):
        base = tile * tc * TOP_K

        def body(r, c):
            for kk in range(TOP_K):
                row = dest_ref[base + r * TOP_K + kk]
                pltpu.make_async_copy(y_hbm.at[pl.ds(row, 1)], buf.at[slot, kk, pl.ds(r, 1)],
                                      sem.at[slot]).start(priority=kk % 2)
            return c

        lax.fori_loop(0, tc, body, 0)

    @pl.when(t == 0)
    def _():
        start(0, 0)

    @pl.when(t + 1 < n_tiles)
    def _():
        start(t + 1, (t + 1) % 2)

    slot = t % 2
    for kk in range(TOP_K):
        pltpu.make_async_copy(y_hbm.at[pl.ds(0, tc)], buf.at[slot, kk], sem.at[slot]).wait()
    gates = gate_ref[...]
    lo, hi = _unpack_bf16_pair(buf[slot, 0])
    ffn_lo = gates[:, 0:1] * lo
    ffn_hi = gates[:, 0:1] * hi
    for kk in range(1, TOP_K):
        lo, hi = _unpack_bf16_pair(buf[slot, kk])
        ffn_lo = ffn_lo + gates[:, kk:kk + 1] * lo
        ffn_hi = ffn_hi + gates[:, kk:kk + 1] * hi
    ffn = jnp.concatenate([ffn_lo, ffn_hi], axis=1)
    out = _layer_norm_rows(DN_ALPHA * x_ref[...] + ffn, g_ref[...], b_ref[...])
    of_ref[...] = out
    ob_ref[...] = out.astype(ob_ref.dtype)


def _combine_ln(y, dest, gates, xres, g, b):
    t, d = xres.shape
    tc = min(COMBINE_TC, t)
    n_tiles = t // tc
    return pl.pallas_call(
        functools.partial(_combine_kernel, tc=tc, n_tiles=n_tiles),
        out_shape=(jax.ShapeDtypeStruct((t, d), F32), jax.ShapeDtypeStruct((t, d), BF16)),
        grid_spec=pltpu.PrefetchScalarGridSpec(
            num_scalar_prefetch=1,
            grid=(n_tiles,),
            in_specs=[pl.BlockSpec(memory_space=pl.ANY),
                      pl.BlockSpec((tc, TOP_K), lambda i, ds: (i, 0)),
                      pl.BlockSpec((tc, d), lambda i, ds: (i, 0)),
                      pl.BlockSpec((1, d), lambda i, ds: (0, 0)),
                      pl.BlockSpec((1, d), lambda i, ds: (0, 0))],
            out_specs=(pl.BlockSpec((tc, d), lambda i, ds: (i, 0)),
                       pl.BlockSpec((tc, d), lambda i, ds: (i, 0))),
            scratch_shapes=[pltpu.VMEM((2, TOP_K, tc, d // 2), jnp.int32),
                            pltpu.SemaphoreType.DMA((2,))]),
        compiler_params=_params(("arbitrary",)),
        name="moe_combine_deepnorm",
    )(dest.reshape(-1), y, gates, xres, g.reshape(1, d), b.reshape(1, d))


def _moe_block(x, w_r, b_r, w_gu, b_gu, w_dn, b_dn, layer, g, b):
    tm = MOE_TM
    idx, gates, rank, counts = _router(x, w_r, b_r)
    dest, row_tok, sched, n_rows, n_tiles = _route_tables(idx, rank, counts[0], tm)
    xs = _dispatch(x, row_tok, sched[-1], tm, n_rows, n_tiles)
    act = _moe_up(xs, w_gu, b_gu, layer, sched, tm, n_tiles)
    y = _moe_down(act, w_dn, b_dn, layer, sched, tm, n_tiles)
    return _combine_ln(y, dest, gates, x, g, b)


def kernel(x, gla_w_in, gla_w_gate2, gla_b_gate2, gla_norm_g, gla_w_out, sb_w_q, sb_w_out,
           shared_w_kv, router_w, router_b, moe_w_gate_up, moe_b_gate_up, moe_w_down, moe_b_down,
           ln1_g, ln1_b, ln2_g, ln2_b):
    bsz, seq, d = x.shape
    n_a = DEPTH // 2
    xf = x.reshape(bsz * seq, d)
    xb = xf.astype(BF16)
    kvp = None
    for layer in range(DEPTH):
        if layer < n_a:
            mix = _gla_mixer(xb, gla_w_in[layer], gla_w_gate2[layer], gla_b_gate2[layer],
                             gla_norm_g[layer], bsz, seq)
            w_out = gla_w_out[layer]
        else:
            if layer == n_a:
                kvp = _matmul_f32w(xb, shared_w_kv, shared_w_kv.shape[1], BF16)
            j = layer - n_a
            qp = _matmul_f32w(xb, sb_w_q[j], d, BF16)
            mix = _sb_attention(qp, kvp, bsz, seq)
            w_out = sb_w_out[j]
        xf = _matmul_ln(mix, w_out.astype(BF16), xf, ln1_g[layer], ln1_b[layer])
        xf, xb = _moe_block(xf, router_w[layer], router_b[layer], moe_w_gate_up, moe_b_gate_up,
                            moe_w_down, moe_b_down, layer, ln2_g[layer], ln2_b[layer])
    return xf.reshape(bsz, seq, d)
```

```python
import functools

import jax
import jax.numpy as jnp
from jax import lax
from jax.experimental import pallas as pl
from jax.experimental.pallas import tpu as pltpu

F32 = jnp.float32
BF16 = jnp.bfloat16

DEPTH = 2
CHUNK = 64
GLA_HEADS = 8
GLA_GATE_RANK = 16
GLA_TAU = 16.0
SB_HEADS = 32
N_EXPERTS = 32
TOP_K = 4
EXPERT_FF = 768
SWIGLU_LIMIT = 7.0
SWIGLU_ALPHA = 1.702
DN_ALPHA = (2.0 * DEPTH) ** 0.25
LN_EPS = 1e-5
RMS_EPS = 1e-6
LOG2_E = 1.4426950408889634

V7X_LANES = 128
V7X_VMEM_LIMIT_BYTES = 60 * 1024 * 1024

MM_BM = 512
MMW_BN = 1024
LN_BM = 512
LN_BK = 512
GLA_TS = 256
GLA_HB = 4
SB_TQ = 256
SB_HB = 8
SB_STRIP = 32
ROUTER_BM = 512
MOE_TM = 256
COMBINE_TC = 128


def _params(semantics):
    return pltpu.CompilerParams(dimension_semantics=semantics,
                                vmem_limit_bytes=V7X_VMEM_LIMIT_BYTES)


def _mm_f32w_kernel(a_ref, w_ref, o_ref, w_bf, *, w_is_nk):
    @pl.when(pl.program_id(1) == 0)
    def _():
        w_bf[...] = w_ref[...].astype(BF16)

    contract = (((1,), (1 if w_is_nk else 0,)), ((), ()))
    o_ref[...] = lax.dot_general(a_ref[...], w_bf[...], contract,
                                 preferred_element_type=F32).astype(o_ref.dtype)


def _matmul_f32w(a, w, n, out_dtype, w_is_nk=False):
    m, k = a.shape
    bm = min(MM_BM, m)
    bn = min(MMW_BN, n)
    assert m % bm == 0 and n % bn == 0, (m, n, bm, bn)
    w_spec = (pl.BlockSpec((bn, k), lambda j, i: (j, 0)) if w_is_nk
              else pl.BlockSpec((k, bn), lambda j, i: (0, j)))
    return pl.pallas_call(
        functools.partial(_mm_f32w_kernel, w_is_nk=w_is_nk),
        out_shape=jax.ShapeDtypeStruct((m, n), out_dtype),
        grid=(n // bn, m // bm),
        in_specs=[pl.BlockSpec((bm, k), lambda j, i: (i, 0)), w_spec],
        out_specs=pl.BlockSpec((bm, bn), lambda j, i: (i, j)),
        scratch_shapes=[pltpu.VMEM((bn, k) if w_is_nk else (k, bn), BF16)],
        compiler_params=_params(("arbitrary", "arbitrary")),
        name="dense_matmul_f32w",
    )(a, w)


def _pack_bf16_pair(lo, hi):
    lo_bits = lax.bitcast_convert_type(lo.astype(BF16).astype(F32), jnp.int32)
    hi_bits = lax.bitcast_convert_type(hi.astype(BF16).astype(F32), jnp.int32)
    return hi_bits | lax.shift_right_logical(lo_bits, 16)


def _unpack_bf16_pair(packed):
    lo = lax.bitcast_convert_type(lax.shift_left(packed, 16), F32)
    hi = lax.bitcast_convert_type(packed & jnp.int32(-65536), F32)
    return lo, hi


def _layer_norm_rows(y, g, b):
    mu = jnp.mean(y, axis=-1, keepdims=True)
    yc = y - mu
    var = jnp.mean(yc * yc, axis=-1, keepdims=True)
    return yc * lax.rsqrt(var + LN_EPS) * g + b


def _mm_ln_kernel(a_ref, w_ref, x_ref, g_ref, b_ref, o_ref, op_ref, *, nk):
    k = pl.program_id(1)

    @pl.when(k == 0)
    def _():
        o_ref[...] = DN_ALPHA * x_ref[...]

    o_ref[...] += jnp.dot(a_ref[...], w_ref[...], preferred_element_type=F32)

    @pl.when(k == nk - 1)
    def _():
        out = _layer_norm_rows(o_ref[...], g_ref[...], b_ref[...])
        o_ref[...] = out
        half = out.shape[1] // 2
        op_ref[...] = _pack_bf16_pair(out[:, :half], out[:, half:])


def _matmul_ln(a, w, xres, g, b):
    m, k = a.shape
    n = w.shape[1]
    bm = min(LN_BM, m)
    bk = min(LN_BK, k)
    nk = k // bk
    return pl.pallas_call(
        functools.partial(_mm_ln_kernel, nk=nk),
        out_shape=(jax.ShapeDtypeStruct((m, n), F32),
                   jax.ShapeDtypeStruct((m, n // 2), jnp.int32)),
        grid=(m // bm, nk),
        in_specs=[pl.BlockSpec((bm, bk), lambda i, kk: (i, kk)),
                  pl.BlockSpec((bk, n), lambda i, kk: (kk, 0)),
                  pl.BlockSpec((bm, n), lambda i, kk: (i, 0)),
                  pl.BlockSpec((1, n), lambda i, kk: (0, 0)),
                  pl.BlockSpec((1, n), lambda i, kk: (0, 0))],
        out_specs=(pl.BlockSpec((bm, n), lambda i, kk: (i, 0)),
                   pl.BlockSpec((bm, n // 2), lambda i, kk: (i, 0))),
        compiler_params=_params(("parallel", "arbitrary")),
        name="matmul_deepnorm",
    )(a, w, xres, g.reshape(1, n), b.reshape(1, n))


def _gla_kernel(q_ref, k_ref, v_ref, r_ref, gl_ref, wg_ref, bg_ref, ng_ref, o_ref, state_ref,
                *, ts, dk, dv, hb):
    s = pl.program_id(2)

    @pl.when(s == 0)
    def _():
        state_ref[...] = jnp.zeros_like(state_ref)

    row = lax.broadcasted_iota(jnp.int32, (ts, ts), 0)
    col = lax.broadcasted_iota(jnp.int32, (ts, ts), 1)
    shift = CHUNK.bit_length() - 1
    later = ((row >> shift) == (col >> shift)) & (col > row)
    later = jnp.where(later, 1.0, 0.0).astype(BF16)
    lanes = min(dv, V7X_LANES)
    ones = jnp.ones((CHUNK, lanes), BF16)
    tn = (((0,), (0,)), ((), ()))
    code = gl_ref[...].astype(BF16)

    las, k_dec = [], []
    for h in range(hb):
        kc = slice(h * dk, (h + 1) * dk)
        z = jnp.dot(code, wg_ref[:, kc], preferred_element_type=F32) + bg_ref[:, kc]
        la = -(jnp.maximum(-z, 0.0) + jnp.log(1.0 + jnp.exp(-jnp.abs(z)))) * (1.0 / GLA_TAU)
        la = la.astype(BF16)
        to_end = jnp.dot(later, la, preferred_element_type=F32)
        las.append(la)
        k_dec.append((k_ref[:, kc].astype(F32) * jnp.exp(to_end)).astype(BF16))

    for c in range(ts // CHUNK):
        sl = slice(c * CHUNK, (c + 1) * CHUNK)
        states = []
        for h in range(hb):
            vc = slice(h * dv, (h + 1) * dv)
            chunk_log = lax.dot_general(las[h][sl], ones, tn, preferred_element_type=F32)
            decay = jnp.exp(chunk_log)
            kv = lax.dot_general(k_dec[h][sl], v_ref[sl, vc], tn, preferred_element_type=F32)
            state = jnp.concatenate([decay] * (dv // lanes), axis=1) * state_ref[h] + kv
            state_ref[h] = state
            states.append(state.astype(BF16))
        for h in range(hb):
            kc = slice(h * dk, (h + 1) * dk)
            vc = slice(h * dv, (h + 1) * dv)
            o = jnp.dot(q_ref[sl, kc], states[h], preferred_element_type=F32) * (dk ** -0.5)
            o = o * lax.rsqrt(jnp.mean(o * o, axis=-1, keepdims=True) + RMS_EPS) * ng_ref[...]
            r = r_ref[sl, vc].astype(F32)
            o_ref[sl, vc] = (o * (r * jax.nn.sigmoid(r))).astype(o_ref.dtype)


def _gla_core(proj, glow, wg2, bg2, norm_g, bsz, seq):
    t = proj.shape[0]
    qk = wg2.shape[1]
    nh = GLA_HEADS
    dk = qk // nh
    dv = norm_g.shape[0]
    hb = GLA_HB
    ng = nh // hb
    ts = min(GLA_TS, seq)
    ns = seq // ts
    v_blk0 = 2 * qk // (hb * dv)
    r_blk0 = v_blk0 + ng
    gw = glow.shape[1]
    return pl.pallas_call(
        functools.partial(_gla_kernel, ts=ts, dk=dk, dv=dv, hb=hb),
        out_shape=jax.ShapeDtypeStruct((t, nh * dv), BF16),
        grid=(bsz, ng, ns),
        in_specs=[pl.BlockSpec((ts, hb * dk), lambda b, g, s: (b * ns + s, g)),
                  pl.BlockSpec((ts, hb * dk), lambda b, g, s: (b * ns + s, ng + g)),
                  pl.BlockSpec((ts, hb * dv), lambda b, g, s: (b * ns + s, v_blk0 + g)),
                  pl.BlockSpec((ts, hb * dv), lambda b, g, s: (b * ns + s, r_blk0 + g)),
                  pl.BlockSpec((ts, gw), lambda b, g, s: (b * ns + s, 0)),
                  pl.BlockSpec((gw, hb * dk), lambda b, g, s: (0, g)),
                  pl.BlockSpec((1, hb * dk), lambda b, g, s: (0, g)),
                  pl.BlockSpec((1, dv), lambda b, g, s: (0, 0))],
        out_specs=pl.BlockSpec((ts, hb * dv), lambda b, g, s: (b * ns + s, g)),
        scratch_shapes=[pltpu.VMEM((hb, dk, dv), F32)],
        compiler_params=_params(("parallel", "parallel", "arbitrary")),
        name="gla_chunk_scan",
    )(proj, proj, proj, proj, glow, wg2, bg2.reshape(1, qk), norm_g.reshape(1, dv))


def _gla_mixer(x_bf, w_in, w_gate2, b_gate2, norm_g, bsz, seq):
    qk = w_gate2.shape[1]
    n_main = w_in.shape[1] - GLA_GATE_RANK
    w_in_t = w_in.T
    w_code_t = jnp.pad(w_in_t[n_main:], ((0, V7X_LANES - GLA_GATE_RANK), (0, 0)))
    wg2 = jnp.pad(w_gate2, ((0, V7X_LANES - GLA_GATE_RANK), (0, 0))).astype(BF16)
    proj = _matmul_f32w(x_bf, w_in_t, n_main, BF16, w_is_nk=True)
    glow = _matmul_f32w(x_bf, w_code_t, V7X_LANES, F32, w_is_nk=True)
    return _gla_core(proj, glow, wg2, b_gate2, norm_g, bsz, seq)


def _sb_kernel(q_ref, k_ref, v_ref, o_ref, z_scr, sp_scr, tail_scr, w_scr, right_scr, acc_scr,
               *, tq, dh, hb, rs):
    i = pl.program_id(2)
    scale = dh ** -0.5
    row = lax.broadcasted_iota(jnp.int32, (tq, tq), 0)
    col = lax.broadcasted_iota(jnp.int32, (tq, tq), 1)
    suffix = jnp.where(row >= col, 1.0, 0.0).astype(BF16)
    nt = (((1,), (1,)), ((), ()))
    heads = [slice(h * dh, (h + 1) * dh) for h in range(hb)]
    strips = [slice(r0, r0 + rs) for r0 in range(0, tq, rs)]

    def causal(rows):
        r = lax.broadcasted_iota(jnp.int32, (rs, tq), 0) + rows.start
        c = lax.broadcasted_iota(jnp.int32, (rs, tq), 1)
        return c < r

    right_scr[...] = jnp.zeros_like(right_scr)
    acc_scr[...] = jnp.zeros_like(acc_scr)

    def block(jb, masked):
        start = pl.multiple_of(jb * tq, tq)
        for h, c in enumerate(heads):
            z_scr[h] = lax.dot_general(q_ref[:, c], k_ref[pl.ds(start, tq), c], nt,
                                       preferred_element_type=F32) * (scale * LOG2_E)
        for h in range(hb):
            for rows in strips:
                z = z_scr[h, rows, :]
                sp = jnp.maximum(z, 0.0) + jnp.log2(1.0 + jnp.exp2(-jnp.abs(z)))
                if masked:
                    sp = jnp.where(causal(rows), sp, 0.0)
                sp_scr[h, rows, :] = sp.astype(BF16)
            tail_scr[h] = jnp.dot(sp_scr[h], suffix, preferred_element_type=F32)
        for h, c in enumerate(heads):
            for rows in strips:
                w = jnp.exp2(z_scr[h, rows, :] - tail_scr[h, rows, :] - right_scr[h, rows, :])
                if masked:
                    w = jnp.where(causal(rows), w, 0.0)
                w_scr[h, rows, :] = w.astype(BF16)
            acc_scr[h] += jnp.dot(w_scr[h], v_ref[pl.ds(start, tq), c],
                                  preferred_element_type=F32)
            right_scr[h] += tail_scr[h, :, :1]

    block(i, True)

    def body(n, carry):
        block(i - 1 - n, False)
        return carry

    lax.fori_loop(0, i, body, 0)
    for h, c in enumerate(heads):
        o_ref[:, c] = acc_scr[h].astype(o_ref.dtype)


def _sb_attention(qp, kvp, bsz, seq):
    t, d = qp.shape
    nh = SB_HEADS
    dh = d // nh
    hb = SB_HB
    ng = nh // hb
    tq = min(SB_TQ, seq)
    nq = seq // tq
    rs = min(SB_STRIP, tq)
    return pl.pallas_call(
        functools.partial(_sb_kernel, tq=tq, dh=dh, hb=hb, rs=rs),
        out_shape=jax.ShapeDtypeStruct((t, d), BF16),
        grid=(bsz, ng, nq),
        in_specs=[pl.BlockSpec((tq, hb * dh), lambda b, g, i: (b * nq + i, g)),
                  pl.BlockSpec((seq, hb * dh), lambda b, g, i: (b, g)),
                  pl.BlockSpec((seq, hb * dh), lambda b, g, i: (b, ng + g))],
        out_specs=pl.BlockSpec((tq, hb * dh), lambda b, g, i: (b * nq + i, g)),
        scratch_shapes=[pltpu.VMEM((hb, tq, tq), F32),
                        pltpu.VMEM((hb, tq, tq), BF16),
                        pltpu.VMEM((hb, tq, tq), F32),
                        pltpu.VMEM((hb, tq, tq), BF16),
                        pltpu.VMEM((hb, tq, 1), F32),
                        pltpu.VMEM((hb, tq, dh), F32)],
        compiler_params=_params(("parallel", "parallel", "arbitrary")),
        name="stick_breaking_attention",
    )(qp, kvp, kvp)


def _router_kernel(x_ref, w_ref, b_ref, idx_ref, gate_ref, rank_ref, cnt_ref, run_ref):
    i = pl.program_id(0)

    @pl.when(i == 0)
    def _():
        run_ref[...] = jnp.zeros_like(run_ref)

    x = x_ref[...]
    w = w_ref[...]
    x_hi = x.astype(BF16)
    x_lo = (x - x_hi.astype(F32)).astype(BF16)
    w_hi = w.astype(BF16)
    w_lo = (w - w_hi.astype(F32)).astype(BF16)
    logits = (jnp.dot(x_hi, w_hi, preferred_element_type=F32)
              + jnp.dot(x_lo, w_hi, preferred_element_type=F32)
              + jnp.dot(x_hi, w_lo, preferred_element_type=F32)) + b_ref[...]
    bm, n_e = logits.shape
    lane = lax.broadcasted_iota(jnp.int32, logits.shape, 1)
    vals = logits
    top_v, top_i = [], []
    for _ in range(TOP_K):
        m = jnp.max(vals, axis=-1, keepdims=True)
        sel = jnp.min(jnp.where(vals == m, lane, n_e), axis=-1, keepdims=True)
        top_v.append(m)
        top_i.append(sel)
        vals = jnp.where(lane == sel, -jnp.inf, vals)
    ex = [jnp.exp(v - top_v[0]) for v in top_v]
    denom = ex[0] + ex[1] + ex[2] + ex[3]

    chosen = lane == top_i[0]
    for kk in range(1, TOP_K):
        chosen = chosen | (lane == top_i[kk])
    hit = jnp.where(chosen, 1.0, 0.0)
    row = lax.broadcasted_iota(jnp.int32, (bm, bm), 0)
    col = lax.broadcasted_iota(jnp.int32, (bm, bm), 1)
    earlier = jnp.where(col < row, 1.0, 0.0).astype(BF16)
    before = jnp.dot(earlier, hit.astype(BF16), preferred_element_type=F32) + run_ref[...]
    for kk in range(TOP_K):
        idx_ref[:, kk:kk + 1] = top_i[kk]
        gate_ref[:, kk:kk + 1] = ex[kk] / denom
        rank = jnp.sum(jnp.where(lane == top_i[kk], before, 0.0), axis=-1, keepdims=True)
        rank_ref[:, kk:kk + 1] = rank.astype(jnp.int32)
    run_ref[...] += jnp.sum(hit, axis=0, keepdims=True)
    cnt_ref[...] = run_ref[...].astype(jnp.int32)


def _router(x, w_r, b_r):
    t, d = x.shape
    n_e = w_r.shape[1]
    bm = min(ROUTER_BM, t)
    return pl.pallas_call(
        _router_kernel,
        out_shape=(jax.ShapeDtypeStruct((t, TOP_K), jnp.int32),
                   jax.ShapeDtypeStruct((t, TOP_K), F32),
                   jax.ShapeDtypeStruct((t, TOP_K), jnp.int32),
                   jax.ShapeDtypeStruct((1, n_e), jnp.int32)),
        grid=(t // bm,),
        in_specs=[pl.BlockSpec((bm, d), lambda i: (i, 0)),
                  pl.BlockSpec((d, n_e), lambda i: (0, 0)),
                  pl.BlockSpec((1, n_e), lambda i: (0, 0))],
        out_specs=(pl.BlockSpec((bm, TOP_K), lambda i: (i, 0)),
                   pl.BlockSpec((bm, TOP_K), lambda i: (i, 0)),
                   pl.BlockSpec((bm, TOP_K), lambda i: (i, 0)),
                   pl.BlockSpec((1, n_e), lambda i: (0, 0))),
        scratch_shapes=[pltpu.VMEM((1, n_e), F32)],
        compiler_params=_params(("arbitrary",)),
        name="router_topk",
    )(x, w_r, b_r.reshape(1, n_e))


def _route_tables(idx, rank, counts, tm):
    t = idx.shape[0]
    n_e = counts.shape[0]
    padded = ((counts + tm - 1) // tm) * tm
    pend = jnp.cumsum(padded)
    pstart = pend - padded
    dest = (pstart[idx] + rank).astype(jnp.int32)
    n_rows = t * TOP_K + n_e * tm
    n_tiles = n_rows // tm
    tok = jnp.repeat(jnp.arange(t, dtype=jnp.int32), TOP_K)
    row_tok = (jnp.arange(n_rows, dtype=jnp.int32) % t).at[dest.reshape(-1)].set(tok)
    tile_end = pend // tm
    tile_id = jnp.arange(n_tiles, dtype=jnp.int32)
    tile_expert = jnp.minimum(jnp.sum(tile_end[None, :] <= tile_id[:, None], axis=1),
                              n_e - 1).astype(jnp.int32)
    n_used = tile_end[-1:].astype(jnp.int32)
    nonempty = (counts > 0).astype(jnp.int32)
    group_of_expert = jnp.cumsum(nonempty) - nonempty
    tile_group = group_of_expert[tile_expert].astype(jnp.int32)
    next_tile = tile_end[tile_expert]
    next_expert = jnp.where(next_tile < n_used[0],
                            tile_expert[jnp.minimum(next_tile, n_tiles - 1)], -1).astype(jnp.int32)
    sched = (tile_expert, next_expert, tile_group, n_used)
    return dest, row_tok, sched, n_rows, n_tiles


def _dispatch_kernel(rt_ref, nu_ref, x_hbm, o_ref, buf, sem, *, tm):
    t = pl.program_id(0)
    n_used = nu_ref[0]

    def start(tile, slot):
        base = tile * tm

        def body(pair, c):
            for prio in range(2):
                r = pair * 2 + prio
                tok = rt_ref[base + r]
                pltpu.make_async_copy(x_hbm.at[pl.ds(tok, 1)], buf.at[slot, pl.ds(r, 1)],
                                      sem.at[slot]).start(priority=prio)
            return c

        lax.fori_loop(0, tm // 2, body, 0)

    @pl.when((t == 0) & (n_used > 0))
    def _():
        start(0, 0)

    @pl.when(t + 1 < n_used)
    def _():
        start(t + 1, (t + 1) % 2)

    @pl.when(t < n_used)
    def _():
        slot = t % 2
        pltpu.make_async_copy(x_hbm.at[pl.ds(0, tm)], buf.at[slot], sem.at[slot]).wait()
        o_ref[...] = buf[slot]

    @pl.when(t >= n_used)
    def _():
        o_ref[...] = jnp.zeros_like(o_ref)


def _dispatch(x_packed, row_tok, n_used, tm, n_rows, n_tiles):
    w = x_packed.shape[1]
    return pl.pallas_call(
        functools.partial(_dispatch_kernel, tm=tm),
        out_shape=jax.ShapeDtypeStruct((n_rows, w), x_packed.dtype),
        grid_spec=pltpu.PrefetchScalarGridSpec(
            num_scalar_prefetch=2,
            grid=(n_tiles,),
            in_specs=[pl.BlockSpec(memory_space=pl.ANY)],
            out_specs=pl.BlockSpec((tm, w), lambda t, rt, nu: (t, 0)),
            scratch_shapes=[pltpu.VMEM((2, tm, w), x_packed.dtype),
                            pltpu.SemaphoreType.DMA((2,))]),
        compiler_params=_params(("arbitrary",)),
        name="moe_dispatch_gather",
    )(row_tok, n_used, x_packed)


def _group_start(te_ref, t):
    return (t == 0) | (te_ref[t] != te_ref[jnp.maximum(t - 1, 0)])


def _moe_up_kernel(te_ref, ne_ref, tg_ref, nu_ref, x_ref, w_hbm, b_ref, o_ref,
                   stage, w_bf, sem, *, layer, ff):
    t = pl.program_id(0)
    n_used = nu_ref[0]

    def weight_copy(e):
        return pltpu.make_async_copy(w_hbm.at[layer, e], stage, sem.at[0])

    @pl.when((t == 0) & (n_used > 0))
    def _():
        weight_copy(te_ref[0]).start()

    @pl.when((t < n_used) & _group_start(te_ref, t))
    def _():
        weight_copy(te_ref[t]).wait()
        w_bf[...] = stage[...].astype(BF16)
        nxt = ne_ref[t]

        @pl.when(nxt >= 0)
        def _():
            weight_copy(nxt).start()

    @pl.when(t < n_used)
    def _():
        x_lo, x_hi = _unpack_bf16_pair(x_ref[...])
        half = x_lo.shape[1]
        h = (jnp.dot(x_lo.astype(BF16), w_bf[:half, :], preferred_element_type=F32)
             + jnp.dot(x_hi.astype(BF16), w_bf[half:, :], preferred_element_type=F32)
             + b_ref[...])
        h_glu = h[:, :ff]
        h_lin = h[:, ff:]
        h_glu = jnp.minimum(h_glu, SWIGLU_LIMIT)
        h_lin = jnp.clip(h_lin, -SWIGLU_LIMIT, SWIGLU_LIMIT)
        act = h_glu * jax.nn.sigmoid(SWIGLU_ALPHA * h_glu) * (h_lin + 1.0)
        o_ref[...] = act.astype(o_ref.dtype)

    @pl.when(t >= n_used)
    def _():
        o_ref[...] = jnp.zeros_like(o_ref)


def _moe_up(xs, w_gu, b_gu, layer, sched, tm, n_tiles):
    n_rows, half_d = xs.shape
    n_l, n_e, d, ff2 = w_gu.shape
    ff = ff2 // 2

    def row_map(t, te, ne, tg, nu):
        return (jnp.maximum(jnp.minimum(t, nu[0] - 1), 0), 0)

    return pl.pallas_call(
        functools.partial(_moe_up_kernel, layer=layer, ff=ff),
        out_shape=jax.ShapeDtypeStruct((n_rows, ff), BF16),
        grid_spec=pltpu.PrefetchScalarGridSpec(
            num_scalar_prefetch=4,
            grid=(n_tiles,),
            in_specs=[pl.BlockSpec((tm, half_d), row_map),
                      pl.BlockSpec(memory_space=pl.ANY),
                      pl.BlockSpec((None, None, 1, ff2),
                                   lambda t, te, ne, tg, nu: (layer, te[t], 0, 0))],
            out_specs=pl.BlockSpec((tm, ff), lambda t, te, ne, tg, nu: (t, 0)),
            scratch_shapes=[pltpu.VMEM((d, ff2), F32),
                            pltpu.VMEM((d, ff2), BF16),
                            pltpu.SemaphoreType.DMA((1,))]),
        compiler_params=_params(("arbitrary",)),
        name="moe_gate_up",
    )(*sched, xs, w_gu, b_gu.reshape(n_l, n_e, 1, ff2))


def _moe_down_kernel(te_ref, ne_ref, tg_ref, nu_ref, a_ref, w_hbm, b_ref, o_ref,
                     stage, w_bf, sem, *, layer):
    t = pl.program_id(0)
    n_used = nu_ref[0]

    def weight_copy(e, slot):
        return pltpu.make_async_copy(w_hbm.at[layer, e], stage.at[slot], sem.at[slot])

    @pl.when((t < n_used) & _group_start(te_ref, t))
    def _():
        e = te_ref[t]
        slot = tg_ref[t] % 2

        @pl.when(t == 0)
        def _():
            weight_copy(e, slot).start()

        nxt = ne_ref[t]

        @pl.when(nxt >= 0)
        def _():
            weight_copy(nxt, 1 - slot).start()

        weight_copy(e, slot).wait()
        w_bf[...] = stage[slot].astype(BF16)

    @pl.when(t < n_used)
    def _():
        y = jnp.dot(a_ref[...], w_bf[...], preferred_element_type=F32) + b_ref[...]
        half = y.shape[1] // 2
        o_ref[...] = _pack_bf16_pair(y[:, :half], y[:, half:])

    @pl.when(t >= n_used)
    def _():
        o_ref[...] = jnp.zeros_like(o_ref)


def _moe_down(act, w_dn, b_dn, layer, sched, tm, n_tiles):
    n_rows, ff = act.shape
    n_l, n_e, _, d = w_dn.shape
    return pl.pallas_call(
        functools.partial(_moe_down_kernel, layer=layer),
        out_shape=jax.ShapeDtypeStruct((n_rows, d // 2), jnp.int32),
        grid_spec=pltpu.PrefetchScalarGridSpec(
            num_scalar_prefetch=4,
            grid=(n_tiles,),
            in_specs=[pl.BlockSpec((tm, ff), lambda t, te, ne, tg, nu:
                                   (jnp.maximum(jnp.minimum(t, nu[0] - 1), 0), 0)),
                      pl.BlockSpec(memory_space=pl.ANY),
                      pl.BlockSpec((None, None, 1, d),
                                   lambda t, te, ne, tg, nu: (layer, te[t], 0, 0))],
            out_specs=pl.BlockSpec((tm, d // 2), lambda t, te, ne, tg, nu: (t, 0)),
            scratch_shapes=[pltpu.VMEM((2, ff, d), F32),
                            pltpu.VMEM((ff, d), BF16),
                            pltpu.SemaphoreType.DMA((2,))]),
        compiler_params=_params(("arbitrary",)),
        name="moe_down",
    )(*sched, act, w_dn, b_dn.reshape(n_l, n_e, 1, d))


def _combine_kernel(dest_ref, y_hbm, gate_ref, x_ref, g_ref, b_ref, of_ref, ob_ref, buf, sem,
                    *, tc, n_tiles):
    t = pl.program_id(0)

    def start(tile, slot):
        base = tile * tc * TOP_K

        def body(r, c):
            for kk in range(TOP_K):
                row = dest_ref[base + r * TOP_K + kk]
                pltpu.make_async_copy(y_hbm.at[pl.ds(row, 1)], buf.at[slot, kk, pl.ds(r, 1)],
                                      sem.at[slot]).start(priority=kk % 2)
            return c

        lax.fori_loop(0, tc, body, 0)

    @pl.when(t == 0)
    def _():
        start(0, 0)

    @pl.when(t + 1 < n_tiles)
    def _():
        start(t + 1, (t + 1) % 2)

    slot = t % 2
    for kk in range(TOP_K):
        pltpu.make_async_copy(y_hbm.at[pl.ds(0, tc)], buf.at[slot, kk], sem.at[slot]).wait()
    gates = gate_ref[...]
    lo, hi = _unpack_bf16_pair(buf[slot, 0])
    ffn_lo = gates[:, 0:1] * lo
    ffn_hi = gates[:, 0:1] * hi
    for kk in range(1, TOP_K):
        lo, hi = _unpack_bf16_pair(buf[slot, kk])
        ffn_lo = ffn_lo + gates[:, kk:kk + 1] * lo
        ffn_hi = ffn_hi + gates[:, kk:kk + 1] * hi
    ffn = jnp.concatenate([ffn_lo, ffn_hi], axis=1)
    out = _layer_norm_rows(DN_ALPHA * x_ref[...] + ffn, g_ref[...], b_ref[...])
    of_ref[...] = out
    ob_ref[...] = out.astype(ob_ref.dtype)


def _combine_ln(y, dest, gates, xres, g, b):
    t, d = xres.shape
    tc = min(COMBINE_TC, t)
    n_tiles = t // tc
    return pl.pallas_call(
        functools.partial(_combine_kernel, tc=tc, n_tiles=n_tiles),
        out_shape=(jax.ShapeDtypeStruct((t, d), F32), jax.ShapeDtypeStruct((t, d), BF16)),
        grid_spec=pltpu.PrefetchScalarGridSpec(
            num_scalar_prefetch=1,
            grid=(n_tiles,),
            in_specs=[pl.BlockSpec(memory_space=pl.ANY),
                      pl.BlockSpec((tc, TOP_K), lambda i, ds: (i, 0)),
                      pl.BlockSpec((tc, d), lambda i, ds: (i, 0)),
                      pl.BlockSpec((1, d), lambda i, ds: (0, 0)),
                      pl.BlockSpec((1, d), lambda i, ds: (0, 0))],
            out_specs=(pl.BlockSpec((tc, d), lambda i, ds: (i, 0)),
                       pl.BlockSpec((tc, d), lambda i, ds: (i, 0))),
            scratch_shapes=[pltpu.VMEM((2, TOP_K, tc, d // 2), jnp.int32),
                            pltpu.SemaphoreType.DMA((2,))]),
        compiler_params=_params(("arbitrary",)),
        name="moe_combine_deepnorm",
    )(dest.reshape(-1), y, gates, xres, g.reshape(1, d), b.reshape(1, d))


def _moe_block(x, x_packed, w_r, b_r, w_gu, b_gu, w_dn, b_dn, layer, g, b):
    tm = MOE_TM
    idx, gates, rank, counts = _router(x, w_r, b_r)
    dest, row_tok, sched, n_rows, n_tiles = _route_tables(idx, rank, counts[0], tm)
    xs = _dispatch(x_packed, row_tok, sched[-1], tm, n_rows, n_tiles)
    act = _moe_up(xs, w_gu, b_gu, layer, sched, tm, n_tiles)
    y = _moe_down(act, w_dn, b_dn, layer, sched, tm, n_tiles)
    return _combine_ln(y, dest, gates, x, g, b)


def kernel(x, gla_w_in, gla_w_gate2, gla_b_gate2, gla_norm_g, gla_w_out, sb_w_q, sb_w_out,
           shared_w_kv, router_w, router_b, moe_w_gate_up, moe_b_gate_up, moe_w_down, moe_b_down,
           ln1_g, ln1_b, ln2_g, ln2_b):
    bsz, seq, d = x.shape
    n_a = DEPTH // 2
    xf = x.reshape(bsz * seq, d)
    xb = xf.astype(BF16)
    kvp = None
    for layer in range(DEPTH):
        if layer < n_a:
            mix = _gla_mixer(xb, gla_w_in[layer], gla_w_gate2[layer], gla_b_gate2[layer],
                             gla_norm_g[layer], bsz, seq)
            w_out = gla_w_out[layer]
        else:
            if layer == n_a:
                kvp = _matmul_f32w(xb, shared_w_kv, shared_w_kv.shape[1], BF16)
            j = layer - n_a
            qp = _matmul_f32w(xb, sb_w_q[j], d, BF16)
            mix = _sb_attention(qp, kvp, bsz, seq)
            w_out = sb_w_out[j]
        xf, xp = _matmul_ln(mix, w_out.astype(BF16), xf, ln1_g[layer], ln1_b[layer])
        xf, xb = _moe_block(xf, xp, router_w[layer], router_b[layer], moe_w_gate_up,
                            moe_b_gate_up, moe_w_down, moe_b_down, layer, ln2_g[layer],
                            ln2_b[layer])
    return xf.reshape(bsz, seq, d)
```

```python
import functools

import jax
import jax.numpy as jnp
from jax import lax
from jax.experimental import pallas as pl
from jax.experimental.pallas import tpu as pltpu

F32 = jnp.float32
BF16 = jnp.bfloat16

DEPTH = 2
CHUNK = 64
GLA_HEADS = 8
GLA_GATE_RANK = 16
GLA_TAU = 16.0
SB_HEADS = 32
TOP_K = 4
SWIGLU_LIMIT = 7.0
SWIGLU_ALPHA = 1.702
DN_ALPHA = (2.0 * DEPTH) ** 0.25
LN_EPS = 1e-5
RMS_EPS = 1e-6
LOG2_E = 1.4426950408889634

V7X_LANES = 128
V7X_VMEM_LIMIT_BYTES = 56 * 1024 * 1024

MM_BM = 512
MMW_BN = 1024
LN_BM = 512
LN_BK = 512
GLA_TS = 256
GLA_HB = 4
SB_TQ = 256
SB_HB = 8
SB_STRIP = 32
ROUTER_BM = 512
MOE_TM = 256
DISPATCH_UNROLL = 4
COMBINE_TC = 128


def _params(semantics):
    return pltpu.CompilerParams(dimension_semantics=semantics,
                                vmem_limit_bytes=V7X_VMEM_LIMIT_BYTES)


def _mm_f32w_kernel(a_ref, w_ref, o_ref, w_bf, *, w_is_nk):
    @pl.when(pl.program_id(1) == 0)
    def _():
        w_bf[...] = w_ref[...].astype(BF16)

    contract = (((1,), (1 if w_is_nk else 0,)), ((), ()))
    o_ref[...] = lax.dot_general(a_ref[...], w_bf[...], contract,
                                 preferred_element_type=F32).astype(o_ref.dtype)


def _matmul_f32w(a, w, n, out_dtype, w_is_nk=False):
    m, k = a.shape
    bm = min(MM_BM, m)
    bn = min(MMW_BN, n)
    assert m % bm == 0 and n % bn == 0, (m, n, bm, bn)
    w_spec = (pl.BlockSpec((bn, k), lambda j, i: (j, 0)) if w_is_nk
              else pl.BlockSpec((k, bn), lambda j, i: (0, j)))
    return pl.pallas_call(
        functools.partial(_mm_f32w_kernel, w_is_nk=w_is_nk),
        out_shape=jax.ShapeDtypeStruct((m, n), out_dtype),
        grid=(n // bn, m // bm),
        in_specs=[pl.BlockSpec((bm, k), lambda j, i: (i, 0)), w_spec],
        out_specs=pl.BlockSpec((bm, bn), lambda j, i: (i, j)),
        scratch_shapes=[pltpu.VMEM((bn, k) if w_is_nk else (k, bn), BF16)],
        compiler_params=_params(("arbitrary", "arbitrary")),
        name="dense_matmul_f32w",
    )(a, w)


def _pack_bf16_pair(lo, hi):
    lo_bits = lax.bitcast_convert_type(lo.astype(BF16).astype(F32), jnp.int32)
    hi_bits = lax.bitcast_convert_type(hi.astype(BF16).astype(F32), jnp.int32)
    return hi_bits | lax.shift_right_logical(lo_bits, 16)


def _unpack_bf16_pair(packed):
    lo = lax.bitcast_convert_type(lax.shift_left(packed, 16), F32)
    hi = lax.bitcast_convert_type(packed & jnp.int32(-65536), F32)
    return lo, hi


def _layer_norm_rows(y, g, b):
    mu = jnp.mean(y, axis=-1, keepdims=True)
    yc = y - mu
    var = jnp.mean(yc * yc, axis=-1, keepdims=True)
    return yc * lax.rsqrt(var + LN_EPS) * g + b


def _mm_ln_kernel(a_ref, w_ref, x_ref, g_ref, b_ref, o_ref, *, nk):
    k = pl.program_id(1)

    @pl.when(k == 0)
    def _():
        o_ref[...] = DN_ALPHA * x_ref[...]

    o_ref[...] += jnp.dot(a_ref[...], w_ref[...], preferred_element_type=F32)

    @pl.when(k == nk - 1)
    def _():
        o_ref[...] = _layer_norm_rows(o_ref[...], g_ref[...], b_ref[...])


def _matmul_ln(a, w, xres, g, b):
    m, k = a.shape
    n = w.shape[1]
    bm = min(LN_BM, m)
    bk = min(LN_BK, k)
    nk = k // bk
    return pl.pallas_call(
        functools.partial(_mm_ln_kernel, nk=nk),
        out_shape=jax.ShapeDtypeStruct((m, n), F32),
        grid=(m // bm, nk),
        in_specs=[pl.BlockSpec((bm, bk), lambda i, kk: (i, kk)),
                  pl.BlockSpec((bk, n), lambda i, kk: (kk, 0)),
                  pl.BlockSpec((bm, n), lambda i, kk: (i, 0)),
                  pl.BlockSpec((1, n), lambda i, kk: (0, 0)),
                  pl.BlockSpec((1, n), lambda i, kk: (0, 0))],
        out_specs=pl.BlockSpec((bm, n), lambda i, kk: (i, 0)),
        compiler_params=_params(("parallel", "arbitrary")),
        name="matmul_deepnorm",
    )(a, w, xres, g.reshape(1, n), b.reshape(1, n))


def _gla_kernel(q_ref, k_ref, v_ref, r_ref, gl_ref, wg_ref, bg_ref, ng_ref, o_ref, state_ref,
                *, ts, dk, dv, hb):
    s = pl.program_id(2)

    @pl.when(s == 0)
    def _():
        state_ref[...] = jnp.zeros_like(state_ref)

    row = lax.broadcasted_iota(jnp.int32, (ts, ts), 0)
    col = lax.broadcasted_iota(jnp.int32, (ts, ts), 1)
    shift = CHUNK.bit_length() - 1
    later = ((row >> shift) == (col >> shift)) & (col > row)
    later = jnp.where(later, 1.0, 0.0).astype(BF16)
    lanes = min(dv, V7X_LANES)
    ones = jnp.ones((CHUNK, lanes), BF16)
    tn = (((0,), (0,)), ((), ()))
    code = gl_ref[...].astype(BF16)

    las, k_dec = [], []
    for h in range(hb):
        kc = slice(h * dk, (h + 1) * dk)
        z = jnp.dot(code, wg_ref[:, kc], preferred_element_type=F32) + bg_ref[:, kc]
        la = -(jnp.maximum(-z, 0.0) + jnp.log(1.0 + jnp.exp(-jnp.abs(z)))) * (1.0 / GLA_TAU)
        la = la.astype(BF16)
        to_end = jnp.dot(later, la, preferred_element_type=F32)
        las.append(la)
        k_dec.append((k_ref[:, kc].astype(F32) * jnp.exp(to_end)).astype(BF16))

    for c in range(ts // CHUNK):
        sl = slice(c * CHUNK, (c + 1) * CHUNK)
        states = []
        for h in range(hb):
            vc = slice(h * dv, (h + 1) * dv)
            chunk_log = lax.dot_general(las[h][sl], ones, tn, preferred_element_type=F32)
            decay = jnp.exp(chunk_log)
            kv = lax.dot_general(k_dec[h][sl], v_ref[sl, vc], tn, preferred_element_type=F32)
            state = jnp.concatenate([decay] * (dv // lanes), axis=1) * state_ref[h] + kv
            state_ref[h] = state
            states.append(state.astype(BF16))
        for h in range(hb):
            kc = slice(h * dk, (h + 1) * dk)
            vc = slice(h * dv, (h + 1) * dv)
            o = jnp.dot(q_ref[sl, kc], states[h], preferred_element_type=F32) * (dk ** -0.5)
            o = o * lax.rsqrt(jnp.mean(o * o, axis=-1, keepdims=True) + RMS_EPS) * ng_ref[...]
            r = r_ref[sl, vc].astype(F32)
            o_ref[sl, vc] = (o * (r * jax.nn.sigmoid(r))).astype(o_ref.dtype)


def _gla_core(proj, glow, wg2, bg2, norm_g, bsz, seq):
    t = proj.shape[0]
    qk = wg2.shape[1]
    nh = GLA_HEADS
    dk = qk // nh
    dv = norm_g.shape[0]
    hb = GLA_HB
    ng = nh // hb
    ts = min(GLA_TS, seq)
    ns = seq // ts
    v_blk0 = 2 * qk // (hb * dv)
    r_blk0 = v_blk0 + ng
    gw = glow.shape[1]
    return pl.pallas_call(
        functools.partial(_gla_kernel, ts=ts, dk=dk, dv=dv, hb=hb),
        out_shape=jax.ShapeDtypeStruct((t, nh * dv), BF16),
        grid=(bsz, ng, ns),
        in_specs=[pl.BlockSpec((ts, hb * dk), lambda b, g, s: (b * ns + s, g)),
                  pl.BlockSpec((ts, hb * dk), lambda b, g, s: (b * ns + s, ng + g)),
                  pl.BlockSpec((ts, hb * dv), lambda b, g, s: (b * ns + s, v_blk0 + g)),
                  pl.BlockSpec((ts, hb * dv), lambda b, g, s: (b * ns + s, r_blk0 + g)),
                  pl.BlockSpec((ts, gw), lambda b, g, s: (b * ns + s, 0)),
                  pl.BlockSpec((gw, hb * dk), lambda b, g, s: (0, g)),
                  pl.BlockSpec((1, hb * dk), lambda b, g, s: (0, g)),
                  pl.BlockSpec((1, dv), lambda b, g, s: (0, 0))],
        out_specs=pl.BlockSpec((ts, hb * dv), lambda b, g, s: (b * ns + s, g)),
        scratch_shapes=[pltpu.VMEM((hb, dk, dv), F32)],
        compiler_params=_params(("parallel", "parallel", "arbitrary")),
        name="gla_chunk_scan",
    )(proj, proj, proj, proj, glow, wg2, bg2.reshape(1, qk), norm_g.reshape(1, dv))


def _gla_mixer(x_bf, w_in, w_gate2, b_gate2, norm_g, bsz, seq):
    qk = w_gate2.shape[1]
    n_main = w_in.shape[1] - GLA_GATE_RANK
    w_in_t = w_in.T
    w_code_t = jnp.pad(w_in_t[n_main:], ((0, V7X_LANES - GLA_GATE_RANK), (0, 0)))
    wg2 = jnp.pad(w_gate2, ((0, V7X_LANES - GLA_GATE_RANK), (0, 0))).astype(BF16)
    proj = _matmul_f32w(x_bf, w_in_t, n_main, BF16, w_is_nk=True)
    glow = _matmul_f32w(x_bf, w_code_t, V7X_LANES, F32, w_is_nk=True)
    return _gla_core(proj, glow, wg2, b_gate2, norm_g, bsz, seq)


def _sb_kernel(q_ref, k_ref, v_ref, o_ref, z_scr, sp_scr, tail_scr, w_scr, right_scr, acc_scr,
               *, tq, dh, hb, rs):
    i = pl.program_id(2)
    scale = dh ** -0.5
    row = lax.broadcasted_iota(jnp.int32, (tq, tq), 0)
    col = lax.broadcasted_iota(jnp.int32, (tq, tq), 1)
    suffix = jnp.where(row >= col, 1.0, 0.0).astype(BF16)
    nt = (((1,), (1,)), ((), ()))
    heads = [slice(h * dh, (h + 1) * dh) for h in range(hb)]
    strips = [slice(r0, r0 + rs) for r0 in range(0, tq, rs)]

    def causal(rows):
        r = lax.broadcasted_iota(jnp.int32, (rs, tq), 0) + rows.start
        c = lax.broadcasted_iota(jnp.int32, (rs, tq), 1)
        return c < r

    right_scr[...] = jnp.zeros_like(right_scr)
    acc_scr[...] = jnp.zeros_like(acc_scr)

    def block(jb, masked):
        start = pl.multiple_of(jb * tq, tq)
        for h, c in enumerate(heads):
            z_scr[h] = lax.dot_general(q_ref[:, c], k_ref[pl.ds(start, tq), c], nt,
                                       preferred_element_type=F32) * (scale * LOG2_E)
        for h in range(hb):
            for rows in strips:
                z = z_scr[h, rows, :]
                sp = jnp.maximum(z, 0.0) + jnp.log2(1.0 + jnp.exp2(-jnp.abs(z)))
                if masked:
                    sp = jnp.where(causal(rows), sp, 0.0)
                sp_scr[h, rows, :] = sp.astype(BF16)
            tail_scr[h] = jnp.dot(sp_scr[h], suffix, preferred_element_type=F32)
        for h, c in enumerate(heads):
            for rows in strips:
                w = jnp.exp2(z_scr[h, rows, :] - tail_scr[h, rows, :] - right_scr[h, rows, :])
                if masked:
                    w = jnp.where(causal(rows), w, 0.0)
                w_scr[h, rows, :] = w.astype(BF16)
            acc_scr[h] += jnp.dot(w_scr[h], v_ref[pl.ds(start, tq), c],
                                  preferred_element_type=F32)
            right_scr[h] += tail_scr[h, :, :1]

    block(i, True)

    def body(n, carry):
        block(i - 1 - n, False)
        return carry

    lax.fori_loop(0, i, body, 0)
    for h, c in enumerate(heads):
        o_ref[:, c] = acc_scr[h].astype(o_ref.dtype)


def _sb_attention(qp, kvp, bsz, seq):
    t, d = qp.shape
    nh = SB_HEADS
    dh = d // nh
    hb = SB_HB
    ng = nh // hb
    tq = min(SB_TQ, seq)
    nq = seq // tq
    rs = min(SB_STRIP, tq)
    return pl.pallas_call(
        functools.partial(_sb_kernel, tq=tq, dh=dh, hb=hb, rs=rs),
        out_shape=jax.ShapeDtypeStruct((t, d), BF16),
        grid=(bsz, ng, nq),
        in_specs=[pl.BlockSpec((tq, hb * dh), lambda b, g, i: (b * nq + i, g)),
                  pl.BlockSpec((seq, hb * dh), lambda b, g, i: (b, g)),
                  pl.BlockSpec((seq, hb * dh), lambda b, g, i: (b, ng + g))],
        out_specs=pl.BlockSpec((tq, hb * dh), lambda b, g, i: (b * nq + i, g)),
        scratch_shapes=[pltpu.VMEM((hb, tq, tq), F32),
                        pltpu.VMEM((hb, tq, tq), BF16),
                        pltpu.VMEM((hb, tq, tq), F32),
                        pltpu.VMEM((hb, tq, tq), BF16),
                        pltpu.VMEM((hb, tq, 1), F32),
                        pltpu.VMEM((hb, tq, dh), F32)],
        compiler_params=_params(("parallel", "parallel", "arbitrary")),
        name="stick_breaking_attention",
    )(qp, kvp, kvp)


def _router_kernel(x_ref, w_ref, b_ref, idx_ref, gate_ref, rank_ref, cnt_ref, run_ref):
    i = pl.program_id(0)

    @pl.when(i == 0)
    def _():
        run_ref[...] = jnp.zeros_like(run_ref)

    x = x_ref[...]
    w = w_ref[...]
    x_hi = x.astype(BF16)
    x_lo = (x - x_hi.astype(F32)).astype(BF16)
    w_hi = w.astype(BF16)
    w_lo = (w - w_hi.astype(F32)).astype(BF16)
    logits = (jnp.dot(x_hi, w_hi, preferred_element_type=F32)
              + jnp.dot(x_lo, w_hi, preferred_element_type=F32)
              + jnp.dot(x_hi, w_lo, preferred_element_type=F32)) + b_ref[...]
    bm, n_e = logits.shape
    lane = lax.broadcasted_iota(jnp.int32, logits.shape, 1)
    vals = logits
    top_v, top_i = [], []
    for _ in range(TOP_K):
        m = jnp.max(vals, axis=-1, keepdims=True)
        sel = jnp.min(jnp.where(vals == m, lane, n_e), axis=-1, keepdims=True)
        top_v.append(m)
        top_i.append(sel)
        vals = jnp.where(lane == sel, -jnp.inf, vals)
    ex = [jnp.exp(v - top_v[0]) for v in top_v]
    denom = ex[0] + ex[1] + ex[2] + ex[3]

    chosen = lane == top_i[0]
    for kk in range(1, TOP_K):
        chosen = chosen | (lane == top_i[kk])
    hit = jnp.where(chosen, 1.0, 0.0)
    row = lax.broadcasted_iota(jnp.int32, (bm, bm), 0)
    col = lax.broadcasted_iota(jnp.int32, (bm, bm), 1)
    earlier = jnp.where(col < row, 1.0, 0.0).astype(BF16)
    before = jnp.dot(earlier, hit.astype(BF16), preferred_element_type=F32) + run_ref[...]
    for kk in range(TOP_K):
        idx_ref[:, kk:kk + 1] = top_i[kk]
        gate_ref[:, kk:kk + 1] = ex[kk] / denom
        rank = jnp.sum(jnp.where(lane == top_i[kk], before, 0.0), axis=-1, keepdims=True)
        rank_ref[:, kk:kk + 1] = rank.astype(jnp.int32)
    run_ref[...] += jnp.sum(hit, axis=0, keepdims=True)
    cnt_ref[...] = run_ref[...].astype(jnp.int32)


def _router(x, w_r, b_r):
    t, d = x.shape
    n_e = w_r.shape[1]
    bm = min(ROUTER_BM, t)
    return pl.pallas_call(
        _router_kernel,
        out_shape=(jax.ShapeDtypeStruct((t, TOP_K), jnp.int32),
                   jax.ShapeDtypeStruct((t, TOP_K), F32),
                   jax.ShapeDtypeStruct((t, TOP_K), jnp.int32),
                   jax.ShapeDtypeStruct((1, n_e), jnp.int32)),
        grid=(t // bm,),
        in_specs=[pl.BlockSpec((bm, d), lambda i: (i, 0)),
                  pl.BlockSpec((d, n_e), lambda i: (0, 0)),
                  pl.BlockSpec((1, n_e), lambda i: (0, 0))],
        out_specs=(pl.BlockSpec((bm, TOP_K), lambda i: (i, 0)),
                   pl.BlockSpec((bm, TOP_K), lambda i: (i, 0)),
                   pl.BlockSpec((bm, TOP_K), lambda i: (i, 0)),
                   pl.BlockSpec((1, n_e), lambda i: (0, 0))),
        scratch_shapes=[pltpu.VMEM((1, n_e), F32)],
        compiler_params=_params(("arbitrary",)),
        name="router_topk",
    )(x, w_r, b_r.reshape(1, n_e))


def _route_tables(idx, rank, counts, tm):
    t = idx.shape[0]
    n_e = counts.shape[0]
    padded = ((counts + tm - 1) // tm) * tm
    pend = jnp.cumsum(padded)
    pstart = pend - padded
    dest = (pstart[idx] + rank).astype(jnp.int32)
    n_rows = t * TOP_K + n_e * tm
    n_tiles = n_rows // tm
    tok = jnp.repeat(jnp.arange(t, dtype=jnp.int32), TOP_K)
    row_tok = (jnp.arange(n_rows, dtype=jnp.int32) % t).at[dest.reshape(-1)].set(tok)
    tile_end = pend // tm
    tile_id = jnp.arange(n_tiles, dtype=jnp.int32)
    tile_expert = jnp.minimum(jnp.sum(tile_end[None, :] <= tile_id[:, None], axis=1),
                              n_e - 1).astype(jnp.int32)
    n_used = tile_end[-1:].astype(jnp.int32)
    nonempty = (counts > 0).astype(jnp.int32)
    group_of_expert = jnp.cumsum(nonempty) - nonempty
    tile_group = group_of_expert[tile_expert].astype(jnp.int32)
    next_tile = tile_end[tile_expert]
    next_expert = jnp.where(next_tile < n_used[0],
                            tile_expert[jnp.minimum(next_tile, n_tiles - 1)], -1).astype(jnp.int32)
    sched = (tile_expert, next_expert, tile_group, n_used)
    return dest, row_tok, sched, n_rows, n_tiles


def _dispatch_kernel(rt_ref, nu_ref, x_hbm, o_ref, buf, sem, *, tm):
    t = pl.program_id(0)
    n_used = nu_ref[0]

    def start(tile, slot):
        base = tile * tm

        def body(group, c):
            for u in range(DISPATCH_UNROLL):
                r = group * DISPATCH_UNROLL + u
                tok = rt_ref[base + r]
                pltpu.make_async_copy(x_hbm.at[pl.ds(tok, 1)], buf.at[slot, pl.ds(r, 1)],
                                      sem.at[slot]).start(priority=u % 2)
            return c

        lax.fori_loop(0, tm // DISPATCH_UNROLL, body, 0)

    @pl.when((t == 0) & (n_used > 0))
    def _():
        start(0, 0)

    @pl.when(t + 1 < n_used)
    def _():
        start(t + 1, (t + 1) % 2)

    @pl.when(t < n_used)
    def _():
        slot = t % 2
        pltpu.make_async_copy(x_hbm.at[pl.ds(0, tm)], buf.at[slot], sem.at[slot]).wait()
        o_ref[...] = buf[slot].astype(o_ref.dtype)

    @pl.when(t >= n_used)
    def _():
        o_ref[...] = jnp.zeros_like(o_ref)


def _dispatch(x, row_tok, n_used, tm, n_rows, n_tiles):
    d = x.shape[1]
    return pl.pallas_call(
        functools.partial(_dispatch_kernel, tm=tm),
        out_shape=jax.ShapeDtypeStruct((n_rows, d), BF16),
        grid_spec=pltpu.PrefetchScalarGridSpec(
            num_scalar_prefetch=2,
            grid=(n_tiles,),
            in_specs=[pl.BlockSpec(memory_space=pl.ANY)],
            out_specs=pl.BlockSpec((tm, d), lambda t, rt, nu: (t, 0)),
            scratch_shapes=[pltpu.VMEM((2, tm, d), F32), pltpu.SemaphoreType.DMA((2,))]),
        compiler_params=_params(("arbitrary",)),
        name="moe_dispatch_gather",
    )(row_tok, n_used, x)


def _group_start(te_ref, t):
    return (t == 0) | (te_ref[t] != te_ref[jnp.maximum(t - 1, 0)])


def _moe_up_kernel(te_ref, ne_ref, tg_ref, nu_ref, x_ref, w_hbm, b_ref, o_ref,
                   stage, w_bf, sem, *, layer, ff):
    t = pl.program_id(0)
    n_used = nu_ref[0]

    def weight_copy(e):
        return pltpu.make_async_copy(w_hbm.at[layer, e], stage, sem.at[0])

    @pl.when((t == 0) & (n_used > 0))
    def _():
        weight_copy(te_ref[0]).start()

    @pl.when((t < n_used) & _group_start(te_ref, t))
    def _():
        weight_copy(te_ref[t]).wait()
        w_bf[...] = stage[...].astype(BF16)
        nxt = ne_ref[t]

        @pl.when(nxt >= 0)
        def _():
            weight_copy(nxt).start()

    @pl.when(t < n_used)
    def _():
        h = jnp.dot(x_ref[...], w_bf[...], preferred_element_type=F32) + b_ref[...]
        h_glu = h[:, :ff]
        h_lin = h[:, ff:]
        h_glu = jnp.minimum(h_glu, SWIGLU_LIMIT)
        h_lin = jnp.clip(h_lin, -SWIGLU_LIMIT, SWIGLU_LIMIT)
        act = h_glu * jax.nn.sigmoid(SWIGLU_ALPHA * h_glu) * (h_lin + 1.0)
        o_ref[...] = act.astype(o_ref.dtype)

    @pl.when(t >= n_used)
    def _():
        o_ref[...] = jnp.zeros_like(o_ref)


def _moe_up(xs, w_gu, b_gu, layer, sched, tm, n_tiles):
    n_rows, d = xs.shape
    n_l, n_e, _, ff2 = w_gu.shape
    ff = ff2 // 2

    def row_map(t, te, ne, tg, nu):
        return (jnp.maximum(jnp.minimum(t, nu[0] - 1), 0), 0)

    return pl.pallas_call(
        functools.partial(_moe_up_kernel, layer=layer, ff=ff),
        out_shape=jax.ShapeDtypeStruct((n_rows, ff), BF16),
        grid_spec=pltpu.PrefetchScalarGridSpec(
            num_scalar_prefetch=4,
            grid=(n_tiles,),
            in_specs=[pl.BlockSpec((tm, d), row_map),
                      pl.BlockSpec(memory_space=pl.ANY),
                      pl.BlockSpec((None, None, 1, ff2),
                                   lambda t, te, ne, tg, nu: (layer, te[t], 0, 0))],
            out_specs=pl.BlockSpec((tm, ff), lambda t, te, ne, tg, nu: (t, 0)),
            scratch_shapes=[pltpu.VMEM((d, ff2), F32),
                            pltpu.VMEM((d, ff2), BF16),
                            pltpu.SemaphoreType.DMA((1,))]),
        compiler_params=_params(("arbitrary",)),
        name="moe_gate_up",
    )(*sched, xs, w_gu, b_gu.reshape(n_l, n_e, 1, ff2))


def _moe_down_kernel(te_ref, ne_ref, tg_ref, nu_ref, a_ref, w_hbm, b_ref, o_ref,
                     stage, w_bf, sem, *, layer):
    t = pl.program_id(0)
    n_used = nu_ref[0]

    def weight_copy(e, slot):
        return pltpu.make_async_copy(w_hbm.at[layer, e], stage.at[slot], sem.at[slot])

    @pl.when((t < n_used) & _group_start(te_ref, t))
    def _():
        e = te_ref[t]
        slot = tg_ref[t] % 2

        @pl.when(t == 0)
        def _():
            weight_copy(e, slot).start()

        nxt = ne_ref[t]

        @pl.when(nxt >= 0)
        def _():
            weight_copy(nxt, 1 - slot).start()

        weight_copy(e, slot).wait()
        w_bf[...] = stage[slot].astype(BF16)

    @pl.when(t < n_used)
    def _():
        y = jnp.dot(a_ref[...], w_bf[...], preferred_element_type=F32) + b_ref[...]
        half = y.shape[1] // 2
        o_ref[...] = _pack_bf16_pair(y[:, :half], y[:, half:])

    @pl.when(t >= n_used)
    def _():
        o_ref[...] = jnp.zeros_like(o_ref)


def _moe_down(act, w_dn, b_dn, layer, sched, tm, n_tiles):
    n_rows, ff = act.shape
    n_l, n_e, _, d = w_dn.shape
    return pl.pallas_call(
        functools.partial(_moe_down_kernel, layer=layer),
        out_shape=jax.ShapeDtypeStruct((n_rows, d // 2), jnp.int32),
        grid_spec=pltpu.PrefetchScalarGridSpec(
            num_scalar_prefetch=4,
            grid=(n_tiles,),
            in_specs=[pl.BlockSpec((tm, ff), lambda t, te, ne, tg, nu:
                                   (jnp.maximum(jnp.minimum(t, nu[0] - 1), 0), 0)),
                      pl.BlockSpec(memory_space=pl.ANY),
                      pl.BlockSpec((None, None, 1, d),
                                   lambda t, te, ne, tg, nu: (layer, te[t], 0, 0))],
            out_specs=pl.BlockSpec((tm, d // 2), lambda t, te, ne, tg, nu: (t, 0)),
            scratch_shapes=[pltpu.VMEM((2, ff, d), F32),
                            pltpu.VMEM((ff, d), BF16),
                            pltpu.SemaphoreType.DMA((2,))]),
        compiler_params=_params(("arbitrary",)),
        name="moe_down",
    )(*sched, act, w_dn, b_dn.reshape(n_l, n_e, 1, d))


def _combine_kernel(dest_ref, y_hbm, gate_ref, x_ref, g_ref, b_ref, of_ref, ob_ref, buf, sem,
                    *, tc, n_tiles):
    t = pl.program_id(0)

    def start(tile, slot):
        base = tile * tc * TOP_K

        def body(r, c):
            for kk in range(TOP_K):
                row = dest_ref[base + r * TOP_K + kk]
                pltpu.make_async_copy(y_hbm.at[pl.ds(row, 1)], buf.at[slot, kk, pl.ds(r, 1)],
                                      sem.at[slot]).start(priority=kk % 2)
            return c

        lax.fori_loop(0, tc, body, 0)

    @pl.when(t == 0)
    def _():
        start(0, 0)

    @pl.when(t + 1 < n_tiles)
    def _():
        start(t + 1, (t + 1) % 2)

    slot = t % 2
    for kk in range(TOP_K):
        pltpu.make_async_copy(y_hbm.at[pl.ds(0, tc)], buf.at[slot, kk], sem.at[slot]).wait()
    gates = gate_ref[...]
    lo, hi = _unpack_bf16_pair(buf[slot, 0])
    ffn_lo = gates[:, 0:1] * lo
    ffn_hi = gates[:, 0:1] * hi
    for kk in range(1, TOP_K):
        lo, hi = _unpack_bf16_pair(buf[slot, kk])
        ffn_lo = ffn_lo + gates[:, kk:kk + 1] * lo
        ffn_hi = ffn_hi + gates[:, kk:kk + 1] * hi
    ffn = jnp.concatenate([ffn_lo, ffn_hi], axis=1)
    out = _layer_norm_rows(DN_ALPHA * x_ref[...] + ffn, g_ref[...], b_ref[...])
    of_ref[...] = out
    ob_ref[...] = out.astype(ob_ref.dtype)


def _combine_ln(y, dest, gates, xres, g, b):
    t, d = xres.shape
    tc = min(COMBINE_TC, t)
    n_tiles = t // tc
    return pl.pallas_call(
        functools.partial(_combine_kernel, tc=tc, n_tiles=n_tiles),
        out_shape=(jax.ShapeDtypeStruct((t, d), F32), jax.ShapeDtypeStruct((t, d), BF16)),
        grid_spec=pltpu.PrefetchScalarGridSpec(
            num_scalar_prefetch=1,
            grid=(n_tiles,),
            in_specs=[pl.BlockSpec(memory_space=pl.ANY),
                      pl.BlockSpec((tc, TOP_K), lambda i, ds: (i, 0)),
                      pl.BlockSpec((tc, d), lambda i, ds: (i, 0)),
                      pl.BlockSpec((1, d), lambda i, ds: (0, 0)),
                      pl.BlockSpec((1, d), lambda i, ds: (0, 0))],
            out_specs=(pl.BlockSpec((tc, d), lambda i, ds: (i, 0)),
                       pl.BlockSpec((tc, d), lambda i, ds: (i, 0))),
            scratch_shapes=[pltpu.VMEM((2, TOP_K, tc, d // 2), jnp.int32),
                            pltpu.SemaphoreType.DMA((2,))]),
        compiler_params=_params(("arbitrary",)),
        name="moe_combine_deepnorm",
    )(dest.reshape(-1), y, gates, xres, g.reshape(1, d), b.reshape(1, d))


def _moe_block(x, w_r, b_r, w_gu, b_gu, w_dn, b_dn, layer, g, b):
    tm = MOE_TM
    idx, gates, rank, counts = _router(x, w_r, b_r)
    dest, row_tok, sched, n_rows, n_tiles = _route_tables(idx, rank, counts[0], tm)
    xs = _dispatch(x, row_tok, sched[-1], tm, n_rows, n_tiles)
    act = _moe_up(xs, w_gu, b_gu, layer, sched, tm, n_tiles)
    y = _moe_down(act, w_dn, b_dn, layer, sched, tm, n_tiles)
    return _combine_ln(y, dest, gates, x, g, b)


def kernel(x, gla_w_in, gla_w_gate2, gla_b_gate2, gla_norm_g, gla_w_out, sb_w_q, sb_w_out,
           shared_w_kv, router_w, router_b, moe_w_gate_up, moe_b_gate_up, moe_w_down, moe_b_down,
           ln1_g, ln1_b, ln2_g, ln2_b):
    bsz, seq, d = x.shape
    n_a = DEPTH // 2
    xf = x.reshape(bsz * seq, d)
    xb = xf.astype(BF16)
    kvp = None
    for layer in range(DEPTH):
        if layer < n_a:
            mix = _gla_mixer(xb, gla_w_in[layer], gla_w_gate2[layer], gla_b_gate2[layer],
                             gla_norm_g[layer], bsz, seq)
            w_out = gla_w_out[layer]
        else:
            if layer == n_a:
                kvp = _matmul_f32w(xb, shared_w_kv, shared_w_kv.shape[1], BF16)
            j = layer - n_a
            qp = _matmul_f32w(xb, sb_w_q[j], d, BF16)
            mix = _sb_attention(qp, kvp, bsz, seq)
            w_out = sb_w_out[j]
        xf = _matmul_ln(mix, w_out.astype(BF16), xf, ln1_g[layer], ln1_b[layer])
        xf, xb = _moe_block(xf, router_w[layer], router_b[layer], moe_w_gate_up, moe_b_gate_up,
                            moe_w_down, moe_b_down, layer, ln2_g[layer], ln2_b[layer])
    return xf.reshape(bsz, seq, d)
```

```python
import functools

import jax
import jax.numpy as jnp
from jax import lax
from jax.experimental import pallas as pl
from jax.experimental.pallas import tpu as pltpu

F32 = jnp.float32
BF16 = jnp.bfloat16

DEPTH = 2
CHUNK = 64
GLA_HEADS = 8
GLA_GATE_RANK = 16
GLA_TAU = 16.0
SB_HEADS = 32
TOP_K = 4
SWIGLU_LIMIT = 7.0
SWIGLU_ALPHA = 1.702
DN_ALPHA = (2.0 * DEPTH) ** 0.25
LN_EPS = 1e-5
RMS_EPS = 1e-6
LOG2_E = 1.4426950408889634

V7X_LANES = 128
V7X_VMEM_LIMIT_BYTES = 56 * 1024 * 1024

MM_BM = 512
MMW_BN = 1024
LN_BM = 512
LN_BK = 512
GLA_TS = 256
GLA_HB = 4
SB_TQ = 256
SB_HB = 8
SB_STRIP = 32
ROUTER_BM = 512
MOE_TM = 256
DISPATCH_UNROLL = 8
COMBINE_TC = 128


def _params(semantics):
    return pltpu.CompilerParams(dimension_semantics=semantics,
                                vmem_limit_bytes=V7X_VMEM_LIMIT_BYTES)


def _mm_f32w_kernel(a_ref, w_ref, o_ref, w_bf, *, w_is_nk):
    @pl.when(pl.program_id(1) == 0)
    def _():
        w_bf[...] = w_ref[...].astype(BF16)

    contract = (((1,), (1 if w_is_nk else 0,)), ((), ()))
    o_ref[...] = lax.dot_general(a_ref[...], w_bf[...], contract,
                                 preferred_element_type=F32).astype(o_ref.dtype)


def _matmul_f32w(a, w, n, out_dtype, w_is_nk=False):
    m, k = a.shape
    bm = min(MM_BM, m)
    bn = min(MMW_BN, n)
    assert m % bm == 0 and n % bn == 0, (m, n, bm, bn)
    w_spec = (pl.BlockSpec((bn, k), lambda j, i: (j, 0)) if w_is_nk
              else pl.BlockSpec((k, bn), lambda j, i: (0, j)))
    return pl.pallas_call(
        functools.partial(_mm_f32w_kernel, w_is_nk=w_is_nk),
        out_shape=jax.ShapeDtypeStruct((m, n), out_dtype),
        grid=(n // bn, m // bm),
        in_specs=[pl.BlockSpec((bm, k), lambda j, i: (i, 0)), w_spec],
        out_specs=pl.BlockSpec((bm, bn), lambda j, i: (i, j)),
        scratch_shapes=[pltpu.VMEM((bn, k) if w_is_nk else (k, bn), BF16)],
        compiler_params=_params(("arbitrary", "arbitrary")),
        name="dense_matmul_f32w",
    )(a, w)


def _pack_bf16_pair(lo, hi):
    lo_bits = lax.bitcast_convert_type(lo.astype(BF16).astype(F32), jnp.int32)
    hi_bits = lax.bitcast_convert_type(hi.astype(BF16).astype(F32), jnp.int32)
    return hi_bits | lax.shift_right_logical(lo_bits, 16)


def _unpack_bf16_pair(packed):
    lo = lax.bitcast_convert_type(lax.shift_left(packed, 16), F32)
    hi = lax.bitcast_convert_type(packed & jnp.int32(-65536), F32)
    return lo, hi


def _layer_norm_rows(y, g, b):
    mu = jnp.mean(y, axis=-1, keepdims=True)
    yc = y - mu
    var = jnp.mean(yc * yc, axis=-1, keepdims=True)
    return yc * lax.rsqrt(var + LN_EPS) * g + b


def _mm_ln_kernel(a_ref, w_ref, x_ref, g_ref, b_ref, o_ref, *, nk):
    k = pl.program_id(1)

    @pl.when(k == 0)
    def _():
        o_ref[...] = DN_ALPHA * x_ref[...]

    o_ref[...] += jnp.dot(a_ref[...], w_ref[...], preferred_element_type=F32)

    @pl.when(k == nk - 1)
    def _():
        o_ref[...] = _layer_norm_rows(o_ref[...], g_ref[...], b_ref[...])


def _matmul_ln(a, w, xres, g, b):
    m, k = a.shape
    n = w.shape[1]
    bm = min(LN_BM, m)
    bk = min(LN_BK, k)
    nk = k // bk
    return pl.pallas_call(
        functools.partial(_mm_ln_kernel, nk=nk),
        out_shape=jax.ShapeDtypeStruct((m, n), F32),
        grid=(m // bm, nk),
        in_specs=[pl.BlockSpec((bm, bk), lambda i, kk: (i, kk)),
                  pl.BlockSpec((bk, n), lambda i, kk: (kk, 0)),
                  pl.BlockSpec((bm, n), lambda i, kk: (i, 0)),
                  pl.BlockSpec((1, n), lambda i, kk: (0, 0)),
                  pl.BlockSpec((1, n), lambda i, kk: (0, 0))],
        out_specs=pl.BlockSpec((bm, n), lambda i, kk: (i, 0)),
        compiler_params=_params(("parallel", "arbitrary")),
        name="matmul_deepnorm",
    )(a, w, xres, g.reshape(1, n), b.reshape(1, n))


def _gla_kernel(q_ref, k_ref, v_ref, r_ref, gl_ref, wg_ref, bg_ref, ng_ref, o_ref, state_ref,
                *, ts, dk, dv, hb):
    s = pl.program_id(2)

    @pl.when(s == 0)
    def _():
        state_ref[...] = jnp.zeros_like(state_ref)

    row = lax.broadcasted_iota(jnp.int32, (ts, ts), 0)
    col = lax.broadcasted_iota(jnp.int32, (ts, ts), 1)
    shift = CHUNK.bit_length() - 1
    later = ((row >> shift) == (col >> shift)) & (col > row)
    later = jnp.where(later, 1.0, 0.0).astype(BF16)
    lanes = min(dv, V7X_LANES)
    ones = jnp.ones((CHUNK, lanes), BF16)
    tn = (((0,), (0,)), ((), ()))
    code = gl_ref[...].astype(BF16)

    las, k_dec = [], []
    for h in range(hb):
        kc = slice(h * dk, (h + 1) * dk)
        z = jnp.dot(code, wg_ref[:, kc], preferred_element_type=F32) + bg_ref[:, kc]
        la = -(jnp.maximum(-z, 0.0) + jnp.log(1.0 + jnp.exp(-jnp.abs(z)))) * (1.0 / GLA_TAU)
        la = la.astype(BF16)
        to_end = jnp.dot(later, la, preferred_element_type=F32)
        las.append(la)
        k_dec.append((k_ref[:, kc].astype(F32) * jnp.exp(to_end)).astype(BF16))

    for c in range(ts // CHUNK):
        sl = slice(c * CHUNK, (c + 1) * CHUNK)
        states = []
        for h in range(hb):
            vc = slice(h * dv, (h + 1) * dv)
            chunk_log = lax.dot_general(las[h][sl], ones, tn, preferred_element_type=F32)
            decay = jnp.exp(chunk_log)
            kv = lax.dot_general(k_dec[h][sl], v_ref[sl, vc], tn, preferred_element_type=F32)
            state = jnp.concatenate([decay] * (dv // lanes), axis=1) * state_ref[h] + kv
            state_ref[h] = state
            states.append(state.astype(BF16))
        for h in range(hb):
            kc = slice(h * dk, (h + 1) * dk)
            vc = slice(h * dv, (h + 1) * dv)
            o = jnp.dot(q_ref[sl, kc], states[h], preferred_element_type=F32) * (dk ** -0.5)
            o = o * lax.rsqrt(jnp.mean(o * o, axis=-1, keepdims=True) + RMS_EPS) * ng_ref[...]
            r = r_ref[sl, vc].astype(F32)
            o_ref[sl, vc] = (o * (r * jax.nn.sigmoid(r))).astype(o_ref.dtype)


def _gla_core(proj, glow, wg2, bg2, norm_g, bsz, seq):
    t = proj.shape[0]
    qk = wg2.shape[1]
    nh = GLA_HEADS
    dk = qk // nh
    dv = norm_g.shape[0]
    hb = GLA_HB
    ng = nh // hb
    ts = min(GLA_TS, seq)
    ns = seq // ts
    v_blk0 = 2 * qk // (hb * dv)
    r_blk0 = v_blk0 + ng
    gw = glow.shape[1]
    return pl.pallas_call(
        functools.partial(_gla_kernel, ts=ts, dk=dk, dv=dv, hb=hb),
        out_shape=jax.ShapeDtypeStruct((t, nh * dv), BF16),
        grid=(bsz, ng, ns),
        in_specs=[pl.BlockSpec((ts, hb * dk), lambda b, g, s: (b * ns + s, g)),
                  pl.BlockSpec((ts, hb * dk), lambda b, g, s: (b * ns + s, ng + g)),
                  pl.BlockSpec((ts, hb * dv), lambda b, g, s: (b * ns + s, v_blk0 + g)),
                  pl.BlockSpec((ts, hb * dv), lambda b, g, s: (b * ns + s, r_blk0 + g)),
                  pl.BlockSpec((ts, gw), lambda b, g, s: (b * ns + s, 0)),
                  pl.BlockSpec((gw, hb * dk), lambda b, g, s: (0, g)),
                  pl.BlockSpec((1, hb * dk), lambda b, g, s: (0, g)),
                  pl.BlockSpec((1, dv), lambda b, g, s: (0, 0))],
        out_specs=pl.BlockSpec((ts, hb * dv), lambda b, g, s: (b * ns + s, g)),
        scratch_shapes=[pltpu.VMEM((hb, dk, dv), F32)],
        compiler_params=_params(("parallel", "parallel", "arbitrary")),
        name="gla_chunk_scan",
    )(proj, proj, proj, proj, glow, wg2, bg2.reshape(1, qk), norm_g.reshape(1, dv))


def _gla_mixer(x_bf, w_in, w_gate2, b_gate2, norm_g, bsz, seq):
    qk = w_gate2.shape[1]
    n_main = w_in.shape[1] - GLA_GATE_RANK
    w_in_t = w_in.T
    w_code_t = jnp.pad(w_in_t[n_main:], ((0, V7X_LANES - GLA_GATE_RANK), (0, 0)))
    wg2 = jnp.pad(w_gate2, ((0, V7X_LANES - GLA_GATE_RANK), (0, 0))).astype(BF16)
    proj = _matmul_f32w(x_bf, w_in_t, n_main, BF16, w_is_nk=True)
    glow = _matmul_f32w(x_bf, w_code_t, V7X_LANES, F32, w_is_nk=True)
    return _gla_core(proj, glow, wg2, b_gate2, norm_g, bsz, seq)


def _sb_kernel(q_ref, k_ref, v_ref, o_ref, z_scr, sp_scr, tail_scr, w_scr, right_scr, acc_scr,
               *, tq, dh, hb, rs):
    i = pl.program_id(2)
    scale = dh ** -0.5
    row = lax.broadcasted_iota(jnp.int32, (tq, tq), 0)
    col = lax.broadcasted_iota(jnp.int32, (tq, tq), 1)
    suffix = jnp.where(row >= col, 1.0, 0.0).astype(BF16)
    nt = (((1,), (1,)), ((), ()))
    heads = [slice(h * dh, (h + 1) * dh) for h in range(hb)]
    strips = [slice(r0, r0 + rs) for r0 in range(0, tq, rs)]

    def causal(rows):
        r = lax.broadcasted_iota(jnp.int32, (rs, tq), 0) + rows.start
        c = lax.broadcasted_iota(jnp.int32, (rs, tq), 1)
        return c < r

    right_scr[...] = jnp.zeros_like(right_scr)
    acc_scr[...] = jnp.zeros_like(acc_scr)

    def block(jb, masked):
        start = pl.multiple_of(jb * tq, tq)
        for h, c in enumerate(heads):
            z_scr[h] = lax.dot_general(q_ref[:, c], k_ref[pl.ds(start, tq), c], nt,
                                       preferred_element_type=F32) * (scale * LOG2_E)
        for h in range(hb):
            for rows in strips:
                z = z_scr[h, rows, :]
                sp = jnp.maximum(z, 0.0) + jnp.log2(1.0 + jnp.exp2(-jnp.abs(z)))
                if masked:
                    sp = jnp.where(causal(rows), sp, 0.0)
                sp_scr[h, rows, :] = sp.astype(BF16)
            tail_scr[h] = jnp.dot(sp_scr[h], suffix, preferred_element_type=F32)
        for h, c in enumerate(heads):
            for rows in strips:
                w = jnp.exp2(z_scr[h, rows, :] - tail_scr[h, rows, :] - right_scr[h, rows, :])
                if masked:
                    w = jnp.where(causal(rows), w, 0.0)
                w_scr[h, rows, :] = w.astype(BF16)
            acc_scr[h] += jnp.dot(w_scr[h], v_ref[pl.ds(start, tq), c],
                                  preferred_element_type=F32)
            right_scr[h] += tail_scr[h, :, :1]

    block(i, True)

    def body(n, carry):
        block(i - 1 - n, False)
        return carry

    lax.fori_loop(0, i, body, 0)
    for h, c in enumerate(heads):
        o_ref[:, c] = acc_scr[h].astype(o_ref.dtype)


def _sb_attention(qp, kvp, bsz, seq):
    t, d = qp.shape
    nh = SB_HEADS
    dh = d // nh
    hb = SB_HB
    ng = nh // hb
    tq = min(SB_TQ, seq)
    nq = seq // tq
    rs = min(SB_STRIP, tq)
    return pl.pallas_call(
        functools.partial(_sb_kernel, tq=tq, dh=dh, hb=hb, rs=rs),
        out_shape=jax.ShapeDtypeStruct((t, d), BF16),
        grid=(bsz, ng, nq),
        in_specs=[pl.BlockSpec((tq, hb * dh), lambda b, g, i: (b * nq + i, g)),
                  pl.BlockSpec((seq, hb * dh), lambda b, g, i: (b, g)),
                  pl.BlockSpec((seq, hb * dh), lambda b, g, i: (b, ng + g))],
        out_specs=pl.BlockSpec((tq, hb * dh), lambda b, g, i: (b * nq + i, g)),
        scratch_shapes=[pltpu.VMEM((hb, tq, tq), F32),
                        pltpu.VMEM((hb, tq, tq), BF16),
                        pltpu.VMEM((hb, tq, tq), F32),
                        pltpu.VMEM((hb, tq, tq), BF16),
                        pltpu.VMEM((hb, tq, 1), F32),
                        pltpu.VMEM((hb, tq, dh), F32)],
        compiler_params=_params(("parallel", "parallel", "arbitrary")),
        name="stick_breaking_attention",
    )(qp, kvp, kvp)


def _router_kernel(x_ref, w_ref, b_ref, idx_ref, gate_ref, rank_ref, cnt_ref, run_ref):
    i = pl.program_id(0)

    @pl.when(i == 0)
    def _():
        run_ref[...] = jnp.zeros_like(run_ref)

    x = x_ref[...]
    w = w_ref[...]
    x_hi = x.astype(BF16)
    x_lo = (x - x_hi.astype(F32)).astype(BF16)
    w_hi = w.astype(BF16)
    w_lo = (w - w_hi.astype(F32)).astype(BF16)
    logits = (jnp.dot(x_hi, w_hi, preferred_element_type=F32)
              + jnp.dot(x_lo, w_hi, preferred_element_type=F32)
              + jnp.dot(x_hi, w_lo, preferred_element_type=F32)) + b_ref[...]
    bm, n_e = logits.shape
    lane = lax.broadcasted_iota(jnp.int32, logits.shape, 1)
    vals = logits
    top_v, top_i = [], []
    for _ in range(TOP_K):
        m = jnp.max(vals, axis=-1, keepdims=True)
        sel = jnp.min(jnp.where(vals == m, lane, n_e), axis=-1, keepdims=True)
        top_v.append(m)
        top_i.append(sel)
        vals = jnp.where(lane == sel, -jnp.inf, vals)
    ex = [jnp.exp(v - top_v[0]) for v in top_v]
    denom = ex[0] + ex[1] + ex[2] + ex[3]

    chosen = lane == top_i[0]
    for kk in range(1, TOP_K):
        chosen = chosen | (lane == top_i[kk])
    hit = jnp.where(chosen, 1.0, 0.0)
    row = lax.broadcasted_iota(jnp.int32, (bm, bm), 0)
    col = lax.broadcasted_iota(jnp.int32, (bm, bm), 1)
    earlier = jnp.where(col < row, 1.0, 0.0).astype(BF16)
    before = jnp.dot(earlier, hit.astype(BF16), preferred_element_type=F32) + run_ref[...]
    for kk in range(TOP_K):
        idx_ref[:, kk:kk + 1] = top_i[kk]
        gate_ref[:, kk:kk + 1] = ex[kk] / denom
        rank = jnp.sum(jnp.where(lane == top_i[kk], before, 0.0), axis=-1, keepdims=True)
        rank_ref[:, kk:kk + 1] = rank.astype(jnp.int32)
    run_ref[...] += jnp.sum(hit, axis=0, keepdims=True)
    cnt_ref[...] = run_ref[...].astype(jnp.int32)


def _router(x, w_r, b_r):
    t, d = x.shape
    n_e = w_r.shape[1]
    bm = min(ROUTER_BM, t)
    return pl.pallas_call(
        _router_kernel,
        out_shape=(jax.ShapeDtypeStruct((t, TOP_K), jnp.int32),
                   jax.ShapeDtypeStruct((t, TOP_K), F32),
                   jax.ShapeDtypeStruct((t, TOP_K), jnp.int32),
                   jax.ShapeDtypeStruct((1, n_e), jnp.int32)),
        grid=(t // bm,),
        in_specs=[pl.BlockSpec((bm, d), lambda i: (i, 0)),
                  pl.BlockSpec((d, n_e), lambda i: (0, 0)),
                  pl.BlockSpec((1, n_e), lambda i: (0, 0))],
        out_specs=(pl.BlockSpec((bm, TOP_K), lambda i: (i, 0)),
                   pl.BlockSpec((bm, TOP_K), lambda i: (i, 0)),
                   pl.BlockSpec((bm, TOP_K), lambda i: (i, 0)),
                   pl.BlockSpec((1, n_e), lambda i: (0, 0))),
        scratch_shapes=[pltpu.VMEM((1, n_e), F32)],
        compiler_params=_params(("arbitrary",)),
        name="router_topk",
    )(x, w_r, b_r.reshape(1, n_e))


def _route_tables(idx, rank, counts, tm):
    t = idx.shape[0]
    n_e = counts.shape[0]
    padded = ((counts + tm - 1) // tm) * tm
    pend = jnp.cumsum(padded)
    pstart = pend - padded
    dest = (pstart[idx] + rank).astype(jnp.int32)
    n_rows = t * TOP_K + n_e * tm
    n_tiles = n_rows // tm
    tok = jnp.repeat(jnp.arange(t, dtype=jnp.int32), TOP_K)
    row_tok = (jnp.arange(n_rows, dtype=jnp.int32) % t).at[dest.reshape(-1)].set(tok)
    tile_end = pend // tm
    tile_id = jnp.arange(n_tiles, dtype=jnp.int32)
    tile_expert = jnp.minimum(jnp.sum(tile_end[None, :] <= tile_id[:, None], axis=1),
                              n_e - 1).astype(jnp.int32)
    n_used = tile_end[-1:].astype(jnp.int32)
    nonempty = (counts > 0).astype(jnp.int32)
    group_of_expert = jnp.cumsum(nonempty) - nonempty
    tile_group = group_of_expert[tile_expert].astype(jnp.int32)
    next_tile = tile_end[tile_expert]
    next_expert = jnp.where(next_tile < n_used[0],
                            tile_expert[jnp.minimum(next_tile, n_tiles - 1)], -1).astype(jnp.int32)
    sched = (tile_expert, next_expert, tile_group, n_used)
    first_tile = tile_end - padded // tm
    real_rows = jnp.clip(counts[tile_expert] - (tile_id - first_tile[tile_expert]) * tm, 0, tm)
    real_rows = jnp.where(tile_id < n_used[0], real_rows, 0)
    tile_groups = ((real_rows + DISPATCH_UNROLL - 1) // DISPATCH_UNROLL).astype(jnp.int32)
    return dest, row_tok, tile_groups, sched, n_rows, n_tiles


def _dispatch_kernel(rt_ref, ng_ref, nu_ref, x_hbm, o_ref, buf, sem, *, tm):
    t = pl.program_id(0)
    n_used = nu_ref[0]

    def start(tile, slot):
        base = tile * tm

        def body(group, c):
            for u in range(DISPATCH_UNROLL):
                r = group * DISPATCH_UNROLL + u
                tok = rt_ref[base + r]
                pltpu.make_async_copy(x_hbm.at[pl.ds(tok, 1)], buf.at[slot, pl.ds(r, 1)],
                                      sem.at[slot]).start(priority=u % 2)
            return c

        lax.fori_loop(0, ng_ref[tile], body, 0)

    @pl.when((t == 0) & (n_used > 0))
    def _():
        buf[...] = jnp.zeros_like(buf)
        start(0, 0)

    @pl.when(t + 1 < n_used)
    def _():
        start(t + 1, (t + 1) % 2)

    @pl.when(t < n_used)
    def _():
        slot = t % 2
        rows = pl.multiple_of(ng_ref[t] * DISPATCH_UNROLL, DISPATCH_UNROLL)
        pltpu.make_async_copy(x_hbm.at[pl.ds(0, rows)], buf.at[slot, pl.ds(0, rows)],
                              sem.at[slot]).wait()
        o_ref[...] = buf[slot].astype(o_ref.dtype)

    @pl.when(t >= n_used)
    def _():
        o_ref[...] = jnp.zeros_like(o_ref)


def _dispatch(x, row_tok, tile_groups, n_used, tm, n_rows, n_tiles):
    d = x.shape[1]
    return pl.pallas_call(
        functools.partial(_dispatch_kernel, tm=tm),
        out_shape=jax.ShapeDtypeStruct((n_rows, d), BF16),
        grid_spec=pltpu.PrefetchScalarGridSpec(
            num_scalar_prefetch=3,
            grid=(n_tiles,),
            in_specs=[pl.BlockSpec(memory_space=pl.ANY)],
            out_specs=pl.BlockSpec((tm, d), lambda t, rt, ng, nu: (t, 0)),
            scratch_shapes=[pltpu.VMEM((2, tm, d), F32), pltpu.SemaphoreType.DMA((2,))]),
        compiler_params=_params(("arbitrary",)),
        name="moe_dispatch_gather",
    )(row_tok, tile_groups, n_used, x)


def _group_start(te_ref, t):
    return (t == 0) | (te_ref[t] != te_ref[jnp.maximum(t - 1, 0)])


def _moe_up_kernel(te_ref, ne_ref, tg_ref, nu_ref, x_ref, w_hbm, b_ref, o_ref,
                   stage, w_bf, sem, *, layer, ff):
    t = pl.program_id(0)
    n_used = nu_ref[0]

    def weight_copy(e):
        return pltpu.make_async_copy(w_hbm.at[layer, e], stage, sem.at[0])

    @pl.when((t == 0) & (n_used > 0))
    def _():
        weight_copy(te_ref[0]).start()

    @pl.when((t < n_used) & _group_start(te_ref, t))
    def _():
        weight_copy(te_ref[t]).wait()
        w_bf[...] = stage[...].astype(BF16)
        nxt = ne_ref[t]

        @pl.when(nxt >= 0)
        def _():
            weight_copy(nxt).start()

    @pl.when(t < n_used)
    def _():
        h = jnp.dot(x_ref[...], w_bf[...], preferred_element_type=F32) + b_ref[...]
        h_glu = h[:, :ff]
        h_lin = h[:, ff:]
        h_glu = jnp.minimum(h_glu, SWIGLU_LIMIT)
        h_lin = jnp.clip(h_lin, -SWIGLU_LIMIT, SWIGLU_LIMIT)
        act = h_glu * jax.nn.sigmoid(SWIGLU_ALPHA * h_glu) * (h_lin + 1.0)
        o_ref[...] = act.astype(o_ref.dtype)

    @pl.when(t >= n_used)
    def _():
        o_ref[...] = jnp.zeros_like(o_ref)


def _moe_up(xs, w_gu, b_gu, layer, sched, tm, n_tiles):
    n_rows, d = xs.shape
    n_l, n_e, _, ff2 = w_gu.shape
    ff = ff2 // 2

    def row_map(t, te, ne, tg, nu):
        return (jnp.maximum(jnp.minimum(t, nu[0] - 1), 0), 0)

    return pl.pallas_call(
        functools.partial(_moe_up_kernel, layer=layer, ff=ff),
        out_shape=jax.ShapeDtypeStruct((n_rows, ff), BF16),
        grid_spec=pltpu.PrefetchScalarGridSpec(
            num_scalar_prefetch=4,
            grid=(n_tiles,),
            in_specs=[pl.BlockSpec((tm, d), row_map),
                      pl.BlockSpec(memory_space=pl.ANY),
                      pl.BlockSpec((None, None, 1, ff2),
                                   lambda t, te, ne, tg, nu: (layer, te[t], 0, 0))],
            out_specs=pl.BlockSpec((tm, ff), lambda t, te, ne, tg, nu: (t, 0)),
            scratch_shapes=[pltpu.VMEM((d, ff2), F32),
                            pltpu.VMEM((d, ff2), BF16),
                            pltpu.SemaphoreType.DMA((1,))]),
        compiler_params=_params(("arbitrary",)),
        name="moe_gate_up",
    )(*sched, xs, w_gu, b_gu.reshape(n_l, n_e, 1, ff2))


def _moe_down_kernel(te_ref, ne_ref, tg_ref, nu_ref, a_ref, w_hbm, b_ref, o_ref,
                     stage, w_bf, sem, *, layer):
    t = pl.program_id(0)
    n_used = nu_ref[0]

    def weight_copy(e, slot):
        return pltpu.make_async_copy(w_hbm.at[layer, e], stage.at[slot], sem.at[slot])

    @pl.when((t < n_used) & _group_start(te_ref, t))
    def _():
        e = te_ref[t]
        slot = tg_ref[t] % 2

        @pl.when(t == 0)
        def _():
            weight_copy(e, slot).start()

        nxt = ne_ref[t]

        @pl.when(nxt >= 0)
        def _():
            weight_copy(nxt, 1 - slot).start()

        weight_copy(e, slot).wait()
        w_bf[...] = stage[slot].astype(BF16)

    @pl.when(t < n_used)
    def _():
        y = jnp.dot(a_ref[...], w_bf[...], preferred_element_type=F32) + b_ref[...]
        half = y.shape[1] // 2
        o_ref[...] = _pack_bf16_pair(y[:, :half], y[:, half:])

    @pl.when(t >= n_used)
    def _():
        o_ref[...] = jnp.zeros_like(o_ref)


def _moe_down(act, w_dn, b_dn, layer, sched, tm, n_tiles):
    n_rows, ff = act.shape
    n_l, n_e, _, d = w_dn.shape
    return pl.pallas_call(
        functools.partial(_moe_down_kernel, layer=layer),
        out_shape=jax.ShapeDtypeStruct((n_rows, d // 2), jnp.int32),
        grid_spec=pltpu.PrefetchScalarGridSpec(
            num_scalar_prefetch=4,
            grid=(n_tiles,),
            in_specs=[pl.BlockSpec((tm, ff), lambda t, te, ne, tg, nu:
                                   (jnp.maximum(jnp.minimum(t, nu[0] - 1), 0), 0)),
                      pl.BlockSpec(memory_space=pl.ANY),
                      pl.BlockSpec((None, None, 1, d),
                                   lambda t, te, ne, tg, nu: (layer, te[t], 0, 0))],
            out_specs=pl.BlockSpec((tm, d // 2), lambda t, te, ne, tg, nu: (t, 0)),
            scratch_shapes=[pltpu.VMEM((2, ff, d), F32),
                            pltpu.VMEM((ff, d), BF16),
                            pltpu.SemaphoreType.DMA((2,))]),
        compiler_params=_params(("arbitrary",)),
        name="moe_down",
    )(*sched, act, w_dn, b_dn.reshape(n_l, n_e, 1, d))


def _combine_kernel(dest_ref, y_hbm, gate_ref, x_ref, g_ref, b_ref, of_ref, ob_ref, buf, sem,
                    *, tc, n_tiles):
    t = pl.program_id(0)

    def start(tile, slot):
        base = tile * tc * TOP_K

        def body(r, c):
            for kk in range(TOP_K):
                row = dest_ref[base + r * TOP_K + kk]
                pltpu.make_async_copy(y_hbm.at[pl.ds(row, 1)], buf.at[slot, kk, pl.ds(r, 1)],
                                      sem.at[slot]).start(priority=kk % 2)
            return c

        lax.fori_loop(0, tc, body, 0)

    @pl.when(t == 0)
    def _():
        start(0, 0)

    @pl.when(t + 1 < n_tiles)
    def _():
        start(t + 1, (t + 1) % 2)

    slot = t % 2
    for kk in range(TOP_K):
        pltpu.make_async_copy(y_hbm.at[pl.ds(0, tc)], buf.at[slot, kk], sem.at[slot]).wait()
    gates = gate_ref[...]
    lo, hi = _unpack_bf16_pair(buf[slot, 0])
    ffn_lo = gates[:, 0:1] * lo
    ffn_hi = gates[:, 0:1] * hi
    for kk in range(1, TOP_K):
        lo, hi = _unpack_bf16_pair(buf[slot, kk])
        ffn_lo = ffn_lo + gates[:, kk:kk + 1] * lo
        ffn_hi = ffn_hi + gates[:, kk:kk + 1] * hi
    ffn = jnp.concatenate([ffn_lo, ffn_hi], axis=1)
    out = _layer_norm_rows(DN_ALPHA * x_ref[...] + ffn, g_ref[...], b_ref[...])
    of_ref[...] = out
    ob_ref[...] = out.astype(ob_ref.dtype)


def _combine_ln(y, dest, gates, xres, g, b):
    t, d = xres.shape
    tc = min(COMBINE_TC, t)
    n_tiles = t // tc
    return pl.pallas_call(
        functools.partial(_combine_kernel, tc=tc, n_tiles=n_tiles),
        out_shape=(jax.ShapeDtypeStruct((t, d), F32), jax.ShapeDtypeStruct((t, d), BF16)),
        grid_spec=pltpu.PrefetchScalarGridSpec(
            num_scalar_prefetch=1,
            grid=(n_tiles,),
            in_specs=[pl.BlockSpec(memory_space=pl.ANY),
                      pl.BlockSpec((tc, TOP_K), lambda i, ds: (i, 0)),
                      pl.BlockSpec((tc, d), lambda i, ds: (i, 0)),
                      pl.BlockSpec((1, d), lambda i, ds: (0, 0)),
                      pl.BlockSpec((1, d), lambda i, ds: (0, 0))],
            out_specs=(pl.BlockSpec((tc, d), lambda i, ds: (i, 0)),
                       pl.BlockSpec((tc, d), lambda i, ds: (i, 0))),
            scratch_shapes=[pltpu.VMEM((2, TOP_K, tc, d // 2), jnp.int32),
                            pltpu.SemaphoreType.DMA((2,))]),
        compiler_params=_params(("arbitrary",)),
        name="moe_combine_deepnorm",
    )(dest.reshape(-1), y, gates, xres, g.reshape(1, d), b.reshape(1, d))


def _moe_block(x, w_r, b_r, w_gu, b_gu, w_dn, b_dn, layer, g, b):
    tm = MOE_TM
    idx, gates, rank, counts = _router(x, w_r, b_r)
    dest, row_tok, tile_groups, sched, n_rows, n_tiles = _route_tables(idx, rank, counts[0], tm)
    xs = _dispatch(x, row_tok, tile_groups, sched[-1], tm, n_rows, n_tiles)
    act = _moe_up(xs, w_gu, b_gu, layer, sched, tm, n_tiles)
    y = _moe_down(act, w_dn, b_dn, layer, sched, tm, n_tiles)
    return _combine_ln(y, dest, gates, x, g, b)


def kernel(x, gla_w_in, gla_w_gate2, gla_b_gate2, gla_norm_g, gla_w_out, sb_w_q, sb_w_out,
           shared_w_kv, router_w, router_b, moe_w_gate_up, moe_b_gate_up, moe_w_down, moe_b_down,
           ln1_g, ln1_b, ln2_g, ln2_b):
    bsz, seq, d = x.shape
    n_a = DEPTH // 2
    xf = x.reshape(bsz * seq, d)
    xb = xf.astype(BF16)
    kvp = None
    for layer in range(DEPTH):
        if layer < n_a:
            mix = _gla_mixer(xb, gla_w_in[layer], gla_w_gate2[layer], gla_b_gate2[layer],
                             gla_norm_g[layer], bsz, seq)
            w_out = gla_w_out[layer]
        else:
            if layer == n_a:
                kvp = _matmul_f32w(xb, shared_w_kv, shared_w_kv.shape[1], BF16)
            j = layer - n_a
            qp = _matmul_f32w(xb, sb_w_q[j], d, BF16)
            mix = _sb_attention(qp, kvp, bsz, seq)
            w_out = sb_w_out[j]
        xf = _matmul_ln(mix, w_out.astype(BF16), xf, ln1_g[layer], ln1_b[layer])
        xf, xb = _moe_block(xf, router_w[layer], router_b[layer], moe_w_gate_up, moe_b_gate_up,
                            moe_w_down, moe_b_down, layer, ln2_g[layer], ln2_b[layer])
    return xf.reshape(bsz, seq, d)
```

```python
import functools

import jax
import jax.numpy as jnp
from jax import lax
from jax.experimental import pallas as pl
from jax.experimental.pallas import tpu as pltpu

F32 = jnp.float32
BF16 = jnp.bfloat16

DEPTH = 2
CHUNK = 64
GLA_HEADS = 8
GLA_GATE_RANK = 16
GLA_TAU = 16.0
SB_HEADS = 32
TOP_K = 4
SWIGLU_LIMIT = 7.0
SWIGLU_ALPHA = 1.702
DN_ALPHA = (2.0 * DEPTH) ** 0.25
LN_EPS = 1e-5
RMS_EPS = 1e-6
LOG2_E = 1.4426950408889634

V7X_LANES = 128
V7X_VMEM_LIMIT_BYTES = 56 * 1024 * 1024

MM_BM = 512
MMW_BN = 1024
LN_BM = 512
LN_BK = 512
GLA_TS = 256
GLA_HB = 4
SB_TQ = 256
SB_HB = 8
SB_STRIP = 32
ROUTER_BM = 512
MOE_TM = 256
DISPATCH_UNROLL = 4
COMBINE_TC = 128


def _params(semantics):
    return pltpu.CompilerParams(dimension_semantics=semantics,
                                vmem_limit_bytes=V7X_VMEM_LIMIT_BYTES)


def _mm_f32w_kernel(a_ref, w_ref, o_ref, w_bf, *, w_is_nk):
    @pl.when(pl.program_id(1) == 0)
    def _():
        w_bf[...] = w_ref[...].astype(BF16)

    contract = (((1,), (1 if w_is_nk else 0,)), ((), ()))
    o_ref[...] = lax.dot_general(a_ref[...], w_bf[...], contract,
                                 preferred_element_type=F32).astype(o_ref.dtype)


def _matmul_f32w(a, w, n, out_dtype, w_is_nk=False):
    m, k = a.shape
    bm = min(MM_BM, m)
    bn = min(MMW_BN, n)
    assert m % bm == 0 and n % bn == 0, (m, n, bm, bn)
    w_spec = (pl.BlockSpec((bn, k), lambda j, i: (j, 0)) if w_is_nk
              else pl.BlockSpec((k, bn), lambda j, i: (0, j)))
    return pl.pallas_call(
        functools.partial(_mm_f32w_kernel, w_is_nk=w_is_nk),
        out_shape=jax.ShapeDtypeStruct((m, n), out_dtype),
        grid=(n // bn, m // bm),
        in_specs=[pl.BlockSpec((bm, k), lambda j, i: (i, 0)), w_spec],
        out_specs=pl.BlockSpec((bm, bn), lambda j, i: (i, j)),
        scratch_shapes=[pltpu.VMEM((bn, k) if w_is_nk else (k, bn), BF16)],
        compiler_params=_params(("arbitrary", "arbitrary")),
        name="dense_matmul_f32w",
    )(a, w)


def _pack_bf16_pair(lo, hi):
    lo_bits = lax.bitcast_convert_type(lo.astype(BF16).astype(F32), jnp.int32)
    hi_bits = lax.bitcast_convert_type(hi.astype(BF16).astype(F32), jnp.int32)
    return hi_bits | lax.shift_right_logical(lo_bits, 16)


def _unpack_bf16_pair(packed):
    lo = lax.bitcast_convert_type(lax.shift_left(packed, 16), F32)
    hi = lax.bitcast_convert_type(packed & jnp.int32(-65536), F32)
    return lo, hi


def _layer_norm_rows(y, g, b):
    mu = jnp.mean(y, axis=-1, keepdims=True)
    yc = y - mu
    var = jnp.mean(yc * yc, axis=-1, keepdims=True)
    return yc * lax.rsqrt(var + LN_EPS) * g + b


def _mm_ln_kernel(a_ref, w_ref, x_ref, g_ref, b_ref, o_ref, *, nk):
    k = pl.program_id(1)

    @pl.when(k == 0)
    def _():
        o_ref[...] = DN_ALPHA * x_ref[...]

    o_ref[...] += jnp.dot(a_ref[...], w_ref[...], preferred_element_type=F32)

    @pl.when(k == nk - 1)
    def _():
        o_ref[...] = _layer_norm_rows(o_ref[...], g_ref[...], b_ref[...])


def _matmul_ln(a, w, xres, g, b):
    m, k = a.shape
    n = w.shape[1]
    bm = min(LN_BM, m)
    bk = min(LN_BK, k)
    nk = k // bk
    return pl.pallas_call(
        functools.partial(_mm_ln_kernel, nk=nk),
        out_shape=jax.ShapeDtypeStruct((m, n), F32),
        grid=(m // bm, nk),
        in_specs=[pl.BlockSpec((bm, bk), lambda i, kk: (i, kk)),
                  pl.BlockSpec((bk, n), lambda i, kk: (kk, 0)),
                  pl.BlockSpec((bm, n), lambda i, kk: (i, 0)),
                  pl.BlockSpec((1, n), lambda i, kk: (0, 0)),
                  pl.BlockSpec((1, n), lambda i, kk: (0, 0))],
        out_specs=pl.BlockSpec((bm, n), lambda i, kk: (i, 0)),
        compiler_params=_params(("parallel", "arbitrary")),
        name="matmul_deepnorm",
    )(a, w, xres, g.reshape(1, n), b.reshape(1, n))


def _gla_kernel(q_ref, k_ref, v_ref, r_ref, gl_ref, wg_ref, bg_ref, ng_ref, o_ref, state_ref,
                *, ts, dk, dv, hb):
    s = pl.program_id(2)

    @pl.when(s == 0)
    def _():
        state_ref[...] = jnp.zeros_like(state_ref)

    row = lax.broadcasted_iota(jnp.int32, (ts, ts), 0)
    col = lax.broadcasted_iota(jnp.int32, (ts, ts), 1)
    shift = CHUNK.bit_length() - 1
    later = ((row >> shift) == (col >> shift)) & (col > row)
    later = jnp.where(later, 1.0, 0.0).astype(BF16)
    lanes = min(dv, V7X_LANES)
    ones = jnp.ones((CHUNK, lanes), BF16)
    tn = (((0,), (0,)), ((), ()))
    code = gl_ref[...].astype(BF16)

    las, k_dec = [], []
    for h in range(hb):
        kc = slice(h * dk, (h + 1) * dk)
        z = jnp.dot(code, wg_ref[:, kc], preferred_element_type=F32) + bg_ref[:, kc]
        la = -(jnp.maximum(-z, 0.0) + jnp.log(1.0 + jnp.exp(-jnp.abs(z)))) * (1.0 / GLA_TAU)
        la = la.astype(BF16)
        to_end = jnp.dot(later, la, preferred_element_type=F32)
        las.append(la)
        k_dec.append((k_ref[:, kc].astype(F32) * jnp.exp(to_end)).astype(BF16))

    for c in range(ts // CHUNK):
        sl = slice(c * CHUNK, (c + 1) * CHUNK)
        states = []
        for h in range(hb):
            vc = slice(h * dv, (h + 1) * dv)
            chunk_log = lax.dot_general(las[h][sl], ones, tn, preferred_element_type=F32)
            decay = jnp.exp(chunk_log)
            kv = lax.dot_general(k_dec[h][sl], v_ref[sl, vc], tn, preferred_element_type=F32)
            state = jnp.concatenate([decay] * (dv // lanes), axis=1) * state_ref[h] + kv
            state_ref[h] = state
            states.append(state.astype(BF16))
        for h in range(hb):
            kc = slice(h * dk, (h + 1) * dk)
            vc = slice(h * dv, (h + 1) * dv)
            o = jnp.dot(q_ref[sl, kc], states[h], preferred_element_type=F32) * (dk ** -0.5)
            o = o * lax.rsqrt(jnp.mean(o * o, axis=-1, keepdims=True) + RMS_EPS) * ng_ref[...]
            r = r_ref[sl, vc].astype(F32)
            o_ref[sl, vc] = (o * (r * jax.nn.sigmoid(r))).astype(o_ref.dtype)


def _gla_core(proj, glow, wg2, bg2, norm_g, bsz, seq):
    t = proj.shape[0]
    qk = wg2.shape[1]
    nh = GLA_HEADS
    dk = qk // nh
    dv = norm_g.shape[0]
    hb = GLA_HB
    ng = nh // hb
    ts = min(GLA_TS, seq)
    ns = seq // ts
    v_blk0 = 2 * qk // (hb * dv)
    r_blk0 = v_blk0 + ng
    gw = glow.shape[1]
    return pl.pallas_call(
        functools.partial(_gla_kernel, ts=ts, dk=dk, dv=dv, hb=hb),
        out_shape=jax.ShapeDtypeStruct((t, nh * dv), BF16),
        grid=(bsz, ng, ns),
        in_specs=[pl.BlockSpec((ts, hb * dk), lambda b, g, s: (b * ns + s, g)),
                  pl.BlockSpec((ts, hb * dk), lambda b, g, s: (b * ns + s, ng + g)),
                  pl.BlockSpec((ts, hb * dv), lambda b, g, s: (b * ns + s, v_blk0 + g)),
                  pl.BlockSpec((ts, hb * dv), lambda b, g, s: (b * ns + s, r_blk0 + g)),
                  pl.BlockSpec((ts, gw), lambda b, g, s: (b * ns + s, 0)),
                  pl.BlockSpec((gw, hb * dk), lambda b, g, s: (0, g)),
                  pl.BlockSpec((1, hb * dk), lambda b, g, s: (0, g)),
                  pl.BlockSpec((1, dv), lambda b, g, s: (0, 0))],
        out_specs=pl.BlockSpec((ts, hb * dv), lambda b, g, s: (b * ns + s, g)),
        scratch_shapes=[pltpu.VMEM((hb, dk, dv), F32)],
        compiler_params=_params(("parallel", "parallel", "arbitrary")),
        name="gla_chunk_scan",
    )(proj, proj, proj, proj, glow, wg2, bg2.reshape(1, qk), norm_g.reshape(1, dv))


def _gla_mixer(x_bf, w_in, w_gate2, b_gate2, norm_g, bsz, seq):
    qk = w_gate2.shape[1]
    n_main = w_in.shape[1] - GLA_GATE_RANK
    w_in_t = w_in.T
    w_code_t = jnp.pad(w_in_t[n_main:], ((0, V7X_LANES - GLA_GATE_RANK), (0, 0)))
    wg2 = jnp.pad(w_gate2, ((0, V7X_LANES - GLA_GATE_RANK), (0, 0))).astype(BF16)
    proj = _matmul_f32w(x_bf, w_in_t, n_main, BF16, w_is_nk=True)
    glow = _matmul_f32w(x_bf, w_code_t, V7X_LANES, F32, w_is_nk=True)
    return _gla_core(proj, glow, wg2, b_gate2, norm_g, bsz, seq)


def _sb_kernel(q_ref, k_ref, v_ref, o_ref, z_scr, sp_scr, tail_scr, w_scr, right_scr, acc_scr,
               *, tq, dh, hb, rs):
    i = pl.program_id(2)
    scale = dh ** -0.5
    row = lax.broadcasted_iota(jnp.int32, (tq, tq), 0)
    col = lax.broadcasted_iota(jnp.int32, (tq, tq), 1)
    suffix = jnp.where(row >= col, 1.0, 0.0).astype(BF16)
    nt = (((1,), (1,)), ((), ()))
    heads = [slice(h * dh, (h + 1) * dh) for h in range(hb)]
    strips = [slice(r0, r0 + rs) for r0 in range(0, tq, rs)]

    def causal(rows):
        r = lax.broadcasted_iota(jnp.int32, (rs, tq), 0) + rows.start
        c = lax.broadcasted_iota(jnp.int32, (rs, tq), 1)
        return c < r

    right_scr[...] = jnp.zeros_like(right_scr)
    acc_scr[...] = jnp.zeros_like(acc_scr)

    def block(jb, masked):
        start = pl.multiple_of(jb * tq, tq)
        for h, c in enumerate(heads):
            z_scr[h] = lax.dot_general(q_ref[:, c], k_ref[pl.ds(start, tq), c], nt,
                                       preferred_element_type=F32) * (scale * LOG2_E)
        for h in range(hb):
            for rows in strips:
                z = z_scr[h, rows, :]
                sp = jnp.maximum(z, 0.0) + jnp.log2(1.0 + jnp.exp2(-jnp.abs(z)))
                if masked:
                    sp = jnp.where(causal(rows), sp, 0.0)
                sp_scr[h, rows, :] = sp.astype(BF16)
            tail_scr[h] = jnp.dot(sp_scr[h], suffix, preferred_element_type=F32)
        for h, c in enumerate(heads):
            for rows in strips:
                w = jnp.exp2(z_scr[h, rows, :] - tail_scr[h, rows, :] - right_scr[h, rows, :])
                if masked:
                    w = jnp.where(causal(rows), w, 0.0)
                w_scr[h, rows, :] = w.astype(BF16)
            acc_scr[h] += jnp.dot(w_scr[h], v_ref[pl.ds(start, tq), c],
                                  preferred_element_type=F32)
            right_scr[h] += tail_scr[h, :, :1]

    block(i, True)

    def body(n, carry):
        block(i - 1 - n, False)
        return carry

    lax.fori_loop(0, i, body, 0)
    for h, c in enumerate(heads):
        o_ref[:, c] = acc_scr[h].astype(o_ref.dtype)


def _sb_attention(qp, kvp, bsz, seq):
    t, d = qp.shape
    nh = SB_HEADS
    dh = d // nh
    hb = SB_HB
    ng = nh // hb
    tq = min(SB_TQ, seq)
    nq = seq // tq
    rs = min(SB_STRIP, tq)
    return pl.pallas_call(
        functools.partial(_sb_kernel, tq=tq, dh=dh, hb=hb, rs=rs),
        out_shape=jax.ShapeDtypeStruct((t, d), BF16),
        grid=(bsz, ng, nq),
        in_specs=[pl.BlockSpec((tq, hb * dh), lambda b, g, i: (b * nq + i, g)),
                  pl.BlockSpec((seq, hb * dh), lambda b, g, i: (b, g)),
                  pl.BlockSpec((seq, hb * dh), lambda b, g, i: (b, ng + g))],
        out_specs=pl.BlockSpec((tq, hb * dh), lambda b, g, i: (b * nq + i, g)),
        scratch_shapes=[pltpu.VMEM((hb, tq, tq), F32),
                        pltpu.VMEM((hb, tq, tq), BF16),
                        pltpu.VMEM((hb, tq, tq), F32),
                        pltpu.VMEM((hb, tq, tq), BF16),
                        pltpu.VMEM((hb, tq, 1), F32),
                        pltpu.VMEM((hb, tq, dh), F32)],
        compiler_params=_params(("parallel", "parallel", "arbitrary")),
        name="stick_breaking_attention",
    )(qp, kvp, kvp)


def _router_kernel(x_ref, w_ref, b_ref, idx_ref, gate_ref, rank_ref, cnt_ref, run_ref):
    i = pl.program_id(0)

    @pl.when(i == 0)
    def _():
        run_ref[...] = jnp.zeros_like(run_ref)

    x = x_ref[...]
    w = w_ref[...]
    x_hi = x.astype(BF16)
    x_lo = (x - x_hi.astype(F32)).astype(BF16)
    w_hi = w.astype(BF16)
    w_lo = (w - w_hi.astype(F32)).astype(BF16)
    logits = (jnp.dot(x_hi, w_hi, preferred_element_type=F32)
              + jnp.dot(x_lo, w_hi, preferred_element_type=F32)
              + jnp.dot(x_hi, w_lo, preferred_element_type=F32)) + b_ref[...]
    bm, n_e = logits.shape
    lane = lax.broadcasted_iota(jnp.int32, logits.shape, 1)
    vals = logits
    top_v, top_i = [], []
    for _ in range(TOP_K):
        m = jnp.max(vals, axis=-1, keepdims=True)
        sel = jnp.min(jnp.where(vals == m, lane, n_e), axis=-1, keepdims=True)
        top_v.append(m)
        top_i.append(sel)
        vals = jnp.where(lane == sel, -jnp.inf, vals)
    ex = [jnp.exp(v - top_v[0]) for v in top_v]
    denom = ex[0] + ex[1] + ex[2] + ex[3]

    chosen = lane == top_i[0]
    for kk in range(1, TOP_K):
        chosen = chosen | (lane == top_i[kk])
    hit = jnp.where(chosen, 1.0, 0.0)
    row = lax.broadcasted_iota(jnp.int32, (bm, bm), 0)
    col = lax.broadcasted_iota(jnp.int32, (bm, bm), 1)
    earlier = jnp.where(col < row, 1.0, 0.0).astype(BF16)
    before = jnp.dot(earlier, hit.astype(BF16), preferred_element_type=F32) + run_ref[...]
    for kk in range(TOP_K):
        idx_ref[:, kk:kk + 1] = top_i[kk]
        gate_ref[:, kk:kk + 1] = ex[kk] / denom
        rank = jnp.sum(jnp.where(lane == top_i[kk], before, 0.0), axis=-1, keepdims=True)
        rank_ref[:, kk:kk + 1] = rank.astype(jnp.int32)
    run_ref[...] += jnp.sum(hit, axis=0, keepdims=True)
    cnt_ref[...] = run_ref[...].astype(jnp.int32)


def _router(x, w_r, b_r):
    t, d = x.shape
    n_e = w_r.shape[1]
    bm = min(ROUTER_BM, t)
    return pl.pallas_call(
        _router_kernel,
        out_shape=(jax.ShapeDtypeStruct((t, TOP_K), jnp.int32),
                   jax.ShapeDtypeStruct((t, TOP_K), F32),
                   jax.ShapeDtypeStruct((t, TOP_K), jnp.int32),
                   jax.ShapeDtypeStruct((1, n_e), jnp.int32)),
        grid=(t // bm,),
        in_specs=[pl.BlockSpec((bm, d), lambda i: (i, 0)),
                  pl.BlockSpec((d, n_e), lambda i: (0, 0)),
                  pl.BlockSpec((1, n_e), lambda i: (0, 0))],
        out_specs=(pl.BlockSpec((bm, TOP_K), lambda i: (i, 0)),
                   pl.BlockSpec((bm, TOP_K), lambda i: (i, 0)),
                   pl.BlockSpec((bm, TOP_K), lambda i: (i, 0)),
                   pl.BlockSpec((1, n_e), lambda i: (0, 0))),
        scratch_shapes=[pltpu.VMEM((1, n_e), F32)],
        compiler_params=_params(("arbitrary",)),
        name="router_topk",
    )(x, w_r, b_r.reshape(1, n_e))


def _route_tables(idx, rank, counts, tm):
    t = idx.shape[0]
    n_e = counts.shape[0]
    padded = ((counts + tm - 1) // tm) * tm
    pend = jnp.cumsum(padded)
    pstart = pend - padded
    dest = (pstart[idx.reshape(-1, V7X_LANES)] + rank.reshape(-1, V7X_LANES))
    dest = dest.astype(jnp.int32).reshape(-1)
    n_rows = t * TOP_K + n_e * tm
    n_tiles = n_rows // tm
    tok = jnp.repeat(jnp.arange(t, dtype=jnp.int32), TOP_K)
    row_tok = (jnp.arange(n_rows, dtype=jnp.int32) % t).at[dest].set(tok)
    tile_end = pend // tm
    tile_id = jnp.arange(n_tiles, dtype=jnp.int32)
    expert_id = jnp.arange(n_e, dtype=jnp.int32)
    tile_expert = jnp.minimum(jnp.sum(tile_end[None, :] <= tile_id[:, None], axis=1),
                              n_e - 1).astype(jnp.int32)
    n_used = tile_end[-1:].astype(jnp.int32)
    nonempty = counts > 0
    group_of_expert = jnp.cumsum(nonempty.astype(jnp.int32)) - nonempty.astype(jnp.int32)
    at_or_after = lax.cummin(jnp.where(nonempty, expert_id, n_e)[::-1])[::-1]
    after = jnp.concatenate([at_or_after[1:], jnp.full((1,), n_e, jnp.int32)])
    per_expert = jnp.stack([group_of_expert, jnp.where(after < n_e, after, -1)], axis=1)
    onehot = (tile_expert[:, None] == expert_id[None, :]).astype(F32)
    per_tile = jnp.dot(onehot, per_expert.astype(F32),
                       precision=lax.Precision.HIGHEST).astype(jnp.int32)
    sched = (tile_expert, per_tile[:, 1], per_tile[:, 0], n_used)
    return dest, row_tok, sched, n_rows, n_tiles


def _dispatch_kernel(rt_ref, nu_ref, x_hbm, o_ref, buf, sem, *, tm):
    t = pl.program_id(0)
    n_used = nu_ref[0]

    def start(tile, slot):
        base = tile * tm

        def body(group, c):
            for u in range(DISPATCH_UNROLL):
                r = group * DISPATCH_UNROLL + u
                tok = rt_ref[base + r]
                pltpu.make_async_copy(x_hbm.at[pl.ds(tok, 1)], buf.at[slot, pl.ds(r, 1)],
                                      sem.at[slot]).start(priority=u % 2)
            return c

        lax.fori_loop(0, tm // DISPATCH_UNROLL, body, 0)

    @pl.when((t == 0) & (n_used > 0))
    def _():
        start(0, 0)

    @pl.when(t + 1 < n_used)
    def _():
        start(t + 1, (t + 1) % 2)

    @pl.when(t < n_used)
    def _():
        slot = t % 2
        pltpu.make_async_copy(x_hbm.at[pl.ds(0, tm)], buf.at[slot], sem.at[slot]).wait()
        o_ref[...] = buf[slot].astype(o_ref.dtype)

    @pl.when(t >= n_used)
    def _():
        o_ref[...] = jnp.zeros_like(o_ref)


def _dispatch(x, row_tok, n_used, tm, n_rows, n_tiles):
    d = x.shape[1]
    return pl.pallas_call(
        functools.partial(_dispatch_kernel, tm=tm),
        out_shape=jax.ShapeDtypeStruct((n_rows, d), BF16),
        grid_spec=pltpu.PrefetchScalarGridSpec(
            num_scalar_prefetch=2,
            grid=(n_tiles,),
            in_specs=[pl.BlockSpec(memory_space=pl.ANY)],
            out_specs=pl.BlockSpec((tm, d), lambda t, rt, nu: (t, 0)),
            scratch_shapes=[pltpu.VMEM((2, tm, d), F32), pltpu.SemaphoreType.DMA((2,))]),
        compiler_params=_params(("arbitrary",)),
        name="moe_dispatch_gather",
    )(row_tok, n_used, x)


def _group_start(te_ref, t):
    return (t == 0) | (te_ref[t] != te_ref[jnp.maximum(t - 1, 0)])


def _moe_up_kernel(te_ref, ne_ref, tg_ref, nu_ref, x_ref, w_hbm, b_ref, o_ref,
                   stage, w_bf, sem, *, layer, ff):
    t = pl.program_id(0)
    n_used = nu_ref[0]

    def weight_copy(e):
        return pltpu.make_async_copy(w_hbm.at[layer, e], stage, sem.at[0])

    @pl.when((t == 0) & (n_used > 0))
    def _():
        weight_copy(te_ref[0]).start()

    @pl.when((t < n_used) & _group_start(te_ref, t))
    def _():
        weight_copy(te_ref[t]).wait()
        w_bf[...] = stage[...].astype(BF16)
        nxt = ne_ref[t]

        @pl.when(nxt >= 0)
        def _():
            weight_copy(nxt).start()

    @pl.when(t < n_used)
    def _():
        h = jnp.dot(x_ref[...], w_bf[...], preferred_element_type=F32) + b_ref[...]
        h_glu = h[:, :ff]
        h_lin = h[:, ff:]
        h_glu = jnp.minimum(h_glu, SWIGLU_LIMIT)
        h_lin = jnp.clip(h_lin, -SWIGLU_LIMIT, SWIGLU_LIMIT)
        act = h_glu * jax.nn.sigmoid(SWIGLU_ALPHA * h_glu) * (h_lin + 1.0)
        o_ref[...] = act.astype(o_ref.dtype)

    @pl.when(t >= n_used)
    def _():
        o_ref[...] = jnp.zeros_like(o_ref)


def _moe_up(xs, w_gu, b_gu, layer, sched, tm, n_tiles):
    n_rows, d = xs.shape
    n_l, n_e, _, ff2 = w_gu.shape
    ff = ff2 // 2

    def row_map(t, te, ne, tg, nu):
        return (jnp.maximum(jnp.minimum(t, nu[0] - 1), 0), 0)

    return pl.pallas_call(
        functools.partial(_moe_up_kernel, layer=layer, ff=ff),
        out_shape=jax.ShapeDtypeStruct((n_rows, ff), BF16),
        grid_spec=pltpu.PrefetchScalarGridSpec(
            num_scalar_prefetch=4,
            grid=(n_tiles,),
            in_specs=[pl.BlockSpec((tm, d), row_map),
                      pl.BlockSpec(memory_space=pl.ANY),
                      pl.BlockSpec((None, None, 1, ff2),
                                   lambda t, te, ne, tg, nu: (layer, te[t], 0, 0))],
            out_specs=pl.BlockSpec((tm, ff), lambda t, te, ne, tg, nu: (t, 0)),
            scratch_shapes=[pltpu.VMEM((d, ff2), F32),
                            pltpu.VMEM((d, ff2), BF16),
                            pltpu.SemaphoreType.DMA((1,))]),
        compiler_params=_params(("arbitrary",)),
        name="moe_gate_up",
    )(*sched, xs, w_gu, b_gu.reshape(n_l, n_e, 1, ff2))


def _moe_down_kernel(te_ref, ne_ref, tg_ref, nu_ref, a_ref, w_hbm, b_ref, o_ref,
                     stage, w_bf, sem, *, layer):
    t = pl.program_id(0)
    n_used = nu_ref[0]

    def weight_copy(e, slot):
        return pltpu.make_async_copy(w_hbm.at[layer, e], stage.at[slot], sem.at[slot])

    @pl.when((t < n_used) & _group_start(te_ref, t))
    def _():
        e = te_ref[t]
        slot = tg_ref[t] % 2

        @pl.when(t == 0)
        def _():
            weight_copy(e, slot).start()

        nxt = ne_ref[t]

        @pl.when(nxt >= 0)
        def _():
            weight_copy(nxt, 1 - slot).start()

        weight_copy(e, slot).wait()
        w_bf[...] = stage[slot].astype(BF16)

    @pl.when(t < n_used)
    def _():
        y = jnp.dot(a_ref[...], w_bf[...], preferred_element_type=F32) + b_ref[...]
        half = y.shape[1] // 2
        o_ref[...] = _pack_bf16_pair(y[:, :half], y[:, half:])

    @pl.when(t >= n_used)
    def _():
        o_ref[...] = jnp.zeros_like(o_ref)


def _moe_down(act, w_dn, b_dn, layer, sched, tm, n_tiles):
    n_rows, ff = act.shape
    n_l, n_e, _, d = w_dn.shape
    return pl.pallas_call(
        functools.partial(_moe_down_kernel, layer=layer),
        out_shape=jax.ShapeDtypeStruct((n_rows, d // 2), jnp.int32),
        grid_spec=pltpu.PrefetchScalarGridSpec(
            num_scalar_prefetch=4,
            grid=(n_tiles,),
            in_specs=[pl.BlockSpec((tm, ff), lambda t, te, ne, tg, nu:
                                   (jnp.maximum(jnp.minimum(t, nu[0] - 1), 0), 0)),
                      pl.BlockSpec(memory_space=pl.ANY),
                      pl.BlockSpec((None, None, 1, d),
                                   lambda t, te, ne, tg, nu: (layer, te[t], 0, 0))],
            out_specs=pl.BlockSpec((tm, d // 2), lambda t, te, ne, tg, nu: (t, 0)),
            scratch_shapes=[pltpu.VMEM((2, ff, d), F32),
                            pltpu.VMEM((ff, d), BF16),
                            pltpu.SemaphoreType.DMA((2,))]),
        compiler_params=_params(("arbitrary",)),
        name="moe_down",
    )(*sched, act, w_dn, b_dn.reshape(n_l, n_e, 1, d))


def _combine_kernel(dest_ref, y_hbm, gate_ref, x_ref, g_ref, b_ref, of_ref, ob_ref, buf, sem,
                    *, tc, n_tiles):
    t = pl.program_id(0)

    def start(tile, slot):
        base = tile * tc * TOP_K

        def body(r, c):
            for kk in range(TOP_K):
                row = dest_ref[base + r * TOP_K + kk]
                pltpu.make_async_copy(y_hbm.at[pl.ds(row, 1)], buf.at[slot, kk, pl.ds(r, 1)],
                                      sem.at[slot]).start(priority=kk % 2)
            return c

        lax.fori_loop(0, tc, body, 0)

    @pl.when(t == 0)
    def _():
        start(0, 0)

    @pl.when(t + 1 < n_tiles)
    def _():
        start(t + 1, (t + 1) % 2)

    slot = t % 2
    for kk in range(TOP_K):
        pltpu.make_async_copy(y_hbm.at[pl.ds(0, tc)], buf.at[slot, kk], sem.at[slot]).wait()
    gates = gate_ref[...]
    lo, hi = _unpack_bf16_pair(buf[slot, 0])
    ffn_lo = gates[:, 0:1] * lo
    ffn_hi = gates[:, 0:1] * hi
    for kk in range(1, TOP_K):
        lo, hi = _unpack_bf16_pair(buf[slot, kk])
        ffn_lo = ffn_lo + gates[:, kk:kk + 1] * lo
        ffn_hi = ffn_hi + gates[:, kk:kk + 1] * hi
    ffn = jnp.concatenate([ffn_lo, ffn_hi], axis=1)
    out = _layer_norm_rows(DN_ALPHA * x_ref[...] + ffn, g_ref[...], b_ref[...])
    of_ref[...] = out
    ob_ref[...] = out.astype(ob_ref.dtype)


def _combine_ln(y, dest, gates, xres, g, b):
    t, d = xres.shape
    tc = min(COMBINE_TC, t)
    n_tiles = t // tc
    return pl.pallas_call(
        functools.partial(_combine_kernel, tc=tc, n_tiles=n_tiles),
        out_shape=(jax.ShapeDtypeStruct((t, d), F32), jax.ShapeDtypeStruct((t, d), BF16)),
        grid_spec=pltpu.PrefetchScalarGridSpec(
            num_scalar_prefetch=1,
            grid=(n_tiles,),
            in_specs=[pl.BlockSpec(memory_space=pl.ANY),
                      pl.BlockSpec((tc, TOP_K), lambda i, ds: (i, 0)),
                      pl.BlockSpec((tc, d), lambda i, ds: (i, 0)),
                      pl.BlockSpec((1, d), lambda i, ds: (0, 0)),
                      pl.BlockSpec((1, d), lambda i, ds: (0, 0))],
            out_specs=(pl.BlockSpec((tc, d), lambda i, ds: (i, 0)),
                       pl.BlockSpec((tc, d), lambda i, ds: (i, 0))),
            scratch_shapes=[pltpu.VMEM((2, TOP_K, tc, d // 2), jnp.int32),
                            pltpu.SemaphoreType.DMA((2,))]),
        compiler_params=_params(("arbitrary",)),
        name="moe_combine_deepnorm",
    )(dest.reshape(-1), y, gates, xres, g.reshape(1, d), b.reshape(1, d))


def _moe_block(x, w_r, b_r, w_gu, b_gu, w_dn, b_dn, layer, g, b):
    tm = MOE_TM
    idx, gates, rank, counts = _router(x, w_r, b_r)
    dest, row_tok, sched, n_rows, n_tiles = _route_tables(idx, rank, counts[0], tm)
    xs = _dispatch(x, row_tok, sched[-1], tm, n_rows, n_tiles)
    act = _moe_up(xs, w_gu, b_gu, layer, sched, tm, n_tiles)
    y = _moe_down(act, w_dn, b_dn, layer, sched, tm, n_tiles)
    return _combine_ln(y, dest, gates, x, g, b)


def kernel(x, gla_w_in, gla_w_gate2, gla_b_gate2, gla_norm_g, gla_w_out, sb_w_q, sb_w_out,
           shared_w_kv, router_w, router_b, moe_w_gate_up, moe_b_gate_up, moe_w_down, moe_b_down,
           ln1_g, ln1_b, ln2_g, ln2_b):
    bsz, seq, d = x.shape
    n_a = DEPTH // 2
    xf = x.reshape(bsz * seq, d)
    xb = xf.astype(BF16)
    kvp = None
    for layer in range(DEPTH):
        if layer < n_a:
            mix = _gla_mixer(xb, gla_w_in[layer], gla_w_gate2[layer], gla_b_gate2[layer],
                             gla_norm_g[layer], bsz, seq)
            w_out = gla_w_out[layer]
        else:
            if layer == n_a:
                kvp = _matmul_f32w(xb, shared_w_kv, shared_w_kv.shape[1], BF16)
            j = layer - n_a
            qp = _matmul_f32w(xb, sb_w_q[j], d, BF16)
            mix = _sb_attention(qp, kvp, bsz, seq)
            w_out = sb_w_out[j]
        xf = _matmul_ln(mix, w_out.astype(BF16), xf, ln1_g[layer], ln1_b[layer])
        xf, xb = _moe_block(xf, router_w[layer], router_b[layer], moe_w_gate_up, moe_b_gate_up,
                            moe_w_down, moe_b_down, layer, ln2_g[layer], ln2_b[layer])
    return xf.reshape(bsz, seq, d)
```

```python
import functools

import jax
import jax.numpy as jnp
from jax import lax
from jax.experimental import pallas as pl
from jax.experimental.pallas import tpu as pltpu

F32 = jnp.float32
BF16 = jnp.bfloat16

DEPTH = 2
CHUNK = 64
GLA_HEADS = 8
GLA_GATE_RANK = 16
GLA_TAU = 16.0
SB_HEADS = 32
TOP_K = 4
SWIGLU_LIMIT = 7.0
SWIGLU_ALPHA = 1.702
DN_ALPHA = (2.0 * DEPTH) ** 0.25
LN_EPS = 1e-5
RMS_EPS = 1e-6
LOG2_E = 1.4426950408889634

V7X_LANES = 128
V7X_VMEM_LIMIT_BYTES = 56 * 1024 * 1024

MM_BM = 512
MMW_BN = 1024
LN_BM = 512
LN_BK = 512
GLA_TS = 256
GLA_HB = 8
SB_TQ = 256
SB_HB = 8
SB_STRIP = 32
ROUTER_BM = 512
MOE_TM = 256
DISPATCH_UNROLL = 4
COMBINE_TC = 256


def _params(semantics):
    return pltpu.CompilerParams(dimension_semantics=semantics,
                                vmem_limit_bytes=V7X_VMEM_LIMIT_BYTES)


def _mm_f32w_kernel(a_ref, w_ref, o_ref, w_bf, *, w_is_nk):
    @pl.when(pl.program_id(1) == 0)
    def _():
        w_bf[...] = w_ref[...].astype(BF16)

    contract = (((1,), (1 if w_is_nk else 0,)), ((), ()))
    o_ref[...] = lax.dot_general(a_ref[...], w_bf[...], contract,
                                 preferred_element_type=F32).astype(o_ref.dtype)


def _matmul_f32w(a, w, n, out_dtype, w_is_nk=False):
    m, k = a.shape
    bm = min(MM_BM, m)
    bn = min(MMW_BN, n)
    assert m % bm == 0 and n % bn == 0, (m, n, bm, bn)
    w_spec = (pl.BlockSpec((bn, k), lambda j, i: (j, 0)) if w_is_nk
              else pl.BlockSpec((k, bn), lambda j, i: (0, j)))
    return pl.pallas_call(
        functools.partial(_mm_f32w_kernel, w_is_nk=w_is_nk),
        out_shape=jax.ShapeDtypeStruct((m, n), out_dtype),
        grid=(n // bn, m // bm),
        in_specs=[pl.BlockSpec((bm, k), lambda j, i: (i, 0)), w_spec],
        out_specs=pl.BlockSpec((bm, bn), lambda j, i: (i, j)),
        scratch_shapes=[pltpu.VMEM((bn, k) if w_is_nk else (k, bn), BF16)],
        compiler_params=_params(("arbitrary", "arbitrary")),
        name="dense_matmul_f32w",
    )(a, w)


def _pack_bf16_pair(lo, hi):
    lo_bits = lax.bitcast_convert_type(lo.astype(BF16).astype(F32), jnp.int32)
    hi_bits = lax.bitcast_convert_type(hi.astype(BF16).astype(F32), jnp.int32)
    return hi_bits | lax.shift_right_logical(lo_bits, 16)


def _unpack_bf16_pair(packed):
    lo = lax.bitcast_convert_type(lax.shift_left(packed, 16), F32)
    hi = lax.bitcast_convert_type(packed & jnp.int32(-65536), F32)
    return lo, hi


def _layer_norm_rows(y, g, b):
    mu = jnp.mean(y, axis=-1, keepdims=True)
    yc = y - mu
    var = jnp.mean(yc * yc, axis=-1, keepdims=True)
    return yc * lax.rsqrt(var + LN_EPS) * g + b


def _mm_ln_kernel(a_ref, w_ref, x_ref, g_ref, b_ref, o_ref, *, nk):
    k = pl.program_id(1)

    @pl.when(k == 0)
    def _():
        o_ref[...] = DN_ALPHA * x_ref[...]

    o_ref[...] += jnp.dot(a_ref[...], w_ref[...], preferred_element_type=F32)

    @pl.when(k == nk - 1)
    def _():
        o_ref[...] = _layer_norm_rows(o_ref[...], g_ref[...], b_ref[...])


def _matmul_ln(a, w, xres, g, b):
    m, k = a.shape
    n = w.shape[1]
    bm = min(LN_BM, m)
    bk = min(LN_BK, k)
    nk = k // bk
    return pl.pallas_call(
        functools.partial(_mm_ln_kernel, nk=nk),
        out_shape=jax.ShapeDtypeStruct((m, n), F32),
        grid=(m // bm, nk),
        in_specs=[pl.BlockSpec((bm, bk), lambda i, kk: (i, kk)),
                  pl.BlockSpec((bk, n), lambda i, kk: (kk, 0)),
                  pl.BlockSpec((bm, n), lambda i, kk: (i, 0)),
                  pl.BlockSpec((1, n), lambda i, kk: (0, 0)),
                  pl.BlockSpec((1, n), lambda i, kk: (0, 0))],
        out_specs=pl.BlockSpec((bm, n), lambda i, kk: (i, 0)),
        compiler_params=_params(("parallel", "arbitrary")),
        name="matmul_deepnorm",
    )(a, w, xres, g.reshape(1, n), b.reshape(1, n))


def _gla_kernel(q_ref, k_ref, v_ref, r_ref, gl_ref, wg_ref, bg_ref, ng_ref, o_ref, state_ref,
                *, ts, dk, dv, hb):
    s = pl.program_id(2)

    @pl.when(s == 0)
    def _():
        state_ref[...] = jnp.zeros_like(state_ref)

    row = lax.broadcasted_iota(jnp.int32, (ts, ts), 0)
    col = lax.broadcasted_iota(jnp.int32, (ts, ts), 1)
    shift = CHUNK.bit_length() - 1
    later = ((row >> shift) == (col >> shift)) & (col > row)
    later = jnp.where(later, 1.0, 0.0).astype(BF16)
    lanes = min(dv, V7X_LANES)
    ones = jnp.ones((CHUNK, lanes), BF16)
    tn = (((0,), (0,)), ((), ()))
    code = gl_ref[...].astype(BF16)

    las, k_dec = [], []
    for h in range(hb):
        kc = slice(h * dk, (h + 1) * dk)
        z = jnp.dot(code, wg_ref[:, kc], preferred_element_type=F32) + bg_ref[:, kc]
        la = -(jnp.maximum(-z, 0.0) + jnp.log(1.0 + jnp.exp(-jnp.abs(z)))) * (1.0 / GLA_TAU)
        la = la.astype(BF16)
        to_end = jnp.dot(later, la, preferred_element_type=F32)
        las.append(la)
        k_dec.append((k_ref[:, kc].astype(F32) * jnp.exp(to_end)).astype(BF16))

    for c in range(ts // CHUNK):
        sl = slice(c * CHUNK, (c + 1) * CHUNK)
        states = []
        for h in range(hb):
            vc = slice(h * dv, (h + 1) * dv)
            chunk_log = lax.dot_general(las[h][sl], ones, tn, preferred_element_type=F32)
            decay = jnp.exp(chunk_log)
            kv = lax.dot_general(k_dec[h][sl], v_ref[sl, vc], tn, preferred_element_type=F32)
            state = jnp.concatenate([decay] * (dv // lanes), axis=1) * state_ref[h] + kv
            state_ref[h] = state
            states.append(state.astype(BF16))
        for h in range(hb):
            kc = slice(h * dk, (h + 1) * dk)
            vc = slice(h * dv, (h + 1) * dv)
            o = jnp.dot(q_ref[sl, kc], states[h], preferred_element_type=F32) * (dk ** -0.5)
            o = o * lax.rsqrt(jnp.mean(o * o, axis=-1, keepdims=True) + RMS_EPS) * ng_ref[...]
            r = r_ref[sl, vc].astype(F32)
            o_ref[sl, vc] = (o * (r * jax.nn.sigmoid(r))).astype(o_ref.dtype)


def _gla_core(proj, glow, wg2, bg2, norm_g, bsz, seq):
    t = proj.shape[0]
    qk = wg2.shape[1]
    nh = GLA_HEADS
    dk = qk // nh
    dv = norm_g.shape[0]
    hb = GLA_HB
    ng = nh // hb
    ts = min(GLA_TS, seq)
    ns = seq // ts
    v_blk0 = 2 * qk // (hb * dv)
    r_blk0 = v_blk0 + ng
    gw = glow.shape[1]
    return pl.pallas_call(
        functools.partial(_gla_kernel, ts=ts, dk=dk, dv=dv, hb=hb),
        out_shape=jax.ShapeDtypeStruct((t, nh * dv), BF16),
        grid=(bsz, ng, ns),
        in_specs=[pl.BlockSpec((ts, hb * dk), lambda b, g, s: (b * ns + s, g)),
                  pl.BlockSpec((ts, hb * dk), lambda b, g, s: (b * ns + s, ng + g)),
                  pl.BlockSpec((ts, hb * dv), lambda b, g, s: (b * ns + s, v_blk0 + g)),
                  pl.BlockSpec((ts, hb * dv), lambda b, g, s: (b * ns + s, r_blk0 + g)),
                  pl.BlockSpec((ts, gw), lambda b, g, s: (b * ns + s, 0)),
                  pl.BlockSpec((gw, hb * dk), lambda b, g, s: (0, g)),
                  pl.BlockSpec((1, hb * dk), lambda b, g, s: (0, g)),
                  pl.BlockSpec((1, dv), lambda b, g, s: (0, 0))],
        out_specs=pl.BlockSpec((ts, hb * dv), lambda b, g, s: (b * ns + s, g)),
        scratch_shapes=[pltpu.VMEM((hb, dk, dv), F32)],
        compiler_params=_params(("parallel", "parallel", "arbitrary")),
        name="gla_chunk_scan",
    )(proj, proj, proj, proj, glow, wg2, bg2.reshape(1, qk), norm_g.reshape(1, dv))


def _gla_mixer(x_bf, w_in, w_gate2, b_gate2, norm_g, bsz, seq):
    qk = w_gate2.shape[1]
    n_main = w_in.shape[1] - GLA_GATE_RANK
    w_in_t = w_in.T
    w_code_t = jnp.pad(w_in_t[n_main:], ((0, V7X_LANES - GLA_GATE_RANK), (0, 0)))
    wg2 = jnp.pad(w_gate2, ((0, V7X_LANES - GLA_GATE_RANK), (0, 0))).astype(BF16)
    proj = _matmul_f32w(x_bf, w_in_t, n_main, BF16, w_is_nk=True)
    glow = _matmul_f32w(x_bf, w_code_t, V7X_LANES, F32, w_is_nk=True)
    return _gla_core(proj, glow, wg2, b_gate2, norm_g, bsz, seq)


def _sb_kernel(q_ref, k_ref, v_ref, o_ref, z_scr, sp_scr, tail_scr, w_scr, right_scr, acc_scr,
               *, tq, dh, hb, rs):
    i = pl.program_id(2)
    scale = dh ** -0.5
    row = lax.broadcasted_iota(jnp.int32, (tq, tq), 0)
    col = lax.broadcasted_iota(jnp.int32, (tq, tq), 1)
    suffix = jnp.where(row >= col, 1.0, 0.0).astype(BF16)
    nt = (((1,), (1,)), ((), ()))
    heads = [slice(h * dh, (h + 1) * dh) for h in range(hb)]
    strips = [slice(r0, r0 + rs) for r0 in range(0, tq, rs)]

    def causal(rows):
        r = lax.broadcasted_iota(jnp.int32, (rs, tq), 0) + rows.start
        c = lax.broadcasted_iota(jnp.int32, (rs, tq), 1)
        return c < r

    right_scr[...] = jnp.zeros_like(right_scr)
    acc_scr[...] = jnp.zeros_like(acc_scr)

    def block(jb, masked):
        start = pl.multiple_of(jb * tq, tq)
        for h, c in enumerate(heads):
            z_scr[h] = lax.dot_general(q_ref[:, c], k_ref[pl.ds(start, tq), c], nt,
                                       preferred_element_type=F32) * (scale * LOG2_E)
        for h in range(hb):
            for rows in strips:
                z = z_scr[h, rows, :]
                sp = jnp.maximum(z, 0.0) + jnp.log2(1.0 + jnp.exp2(-jnp.abs(z)))
                if masked:
                    sp = jnp.where(causal(rows), sp, 0.0)
                sp_scr[h, rows, :] = sp.astype(BF16)
            tail_scr[h] = jnp.dot(sp_scr[h], suffix, preferred_element_type=F32)
        for h, c in enumerate(heads):
            for rows in strips:
                w = jnp.exp2(z_scr[h, rows, :] - tail_scr[h, rows, :] - right_scr[h, rows, :])
                if masked:
                    w = jnp.where(causal(rows), w, 0.0)
                w_scr[h, rows, :] = w.astype(BF16)
            acc_scr[h] += jnp.dot(w_scr[h], v_ref[pl.ds(start, tq), c],
                                  preferred_element_type=F32)
            right_scr[h] += tail_scr[h, :, :1]

    block(i, True)

    def body(n, carry):
        block(i - 1 - n, False)
        return carry

    lax.fori_loop(0, i, body, 0)
    for h, c in enumerate(heads):
        o_ref[:, c] = acc_scr[h].astype(o_ref.dtype)


def _sb_attention(qp, kvp, bsz, seq):
    t, d = qp.shape
    nh = SB_HEADS
    dh = d // nh
    hb = SB_HB
    ng = nh // hb
    tq = min(SB_TQ, seq)
    nq = seq // tq
    rs = min(SB_STRIP, tq)
    return pl.pallas_call(
        functools.partial(_sb_kernel, tq=tq, dh=dh, hb=hb, rs=rs),
        out_shape=jax.ShapeDtypeStruct((t, d), BF16),
        grid=(bsz, ng, nq),
        in_specs=[pl.BlockSpec((tq, hb * dh), lambda b, g, i: (b * nq + i, g)),
                  pl.BlockSpec((seq, hb * dh), lambda b, g, i: (b, g)),
                  pl.BlockSpec((seq, hb * dh), lambda b, g, i: (b, ng + g))],
        out_specs=pl.BlockSpec((tq, hb * dh), lambda b, g, i: (b * nq + i, g)),
        scratch_shapes=[pltpu.VMEM((hb, tq, tq), F32),
                        pltpu.VMEM((hb, tq, tq), BF16),
                        pltpu.VMEM((hb, tq, tq), F32),
                        pltpu.VMEM((hb, tq, tq), BF16),
                        pltpu.VMEM((hb, tq, 1), F32),
                        pltpu.VMEM((hb, tq, dh), F32)],
        compiler_params=_params(("parallel", "parallel", "arbitrary")),
        name="stick_breaking_attention",
    )(qp, kvp, kvp)


def _router_kernel(x_ref, w_ref, b_ref, idx_ref, gate_ref, rank_ref, cnt_ref, run_ref):
    i = pl.program_id(0)

    @pl.when(i == 0)
    def _():
        run_ref[...] = jnp.zeros_like(run_ref)

    x = x_ref[...]
    w = w_ref[...]
    x_hi = x.astype(BF16)
    x_lo = (x - x_hi.astype(F32)).astype(BF16)
    w_hi = w.astype(BF16)
    w_lo = (w - w_hi.astype(F32)).astype(BF16)
    logits = (jnp.dot(x_hi, w_hi, preferred_element_type=F32)
              + jnp.dot(x_lo, w_hi, preferred_element_type=F32)
              + jnp.dot(x_hi, w_lo, preferred_element_type=F32)) + b_ref[...]
    bm, n_e = logits.shape
    lane = lax.broadcasted_iota(jnp.int32, logits.shape, 1)
    vals = logits
    top_v, top_i = [], []
    for _ in range(TOP_K):
        m = jnp.max(vals, axis=-1, keepdims=True)
        sel = jnp.min(jnp.where(vals == m, lane, n_e), axis=-1, keepdims=True)
        top_v.append(m)
        top_i.append(sel)
        vals = jnp.where(lane == sel, -jnp.inf, vals)
    ex = [jnp.exp(v - top_v[0]) for v in top_v]
    denom = ex[0] + ex[1] + ex[2] + ex[3]

    chosen = lane == top_i[0]
    for kk in range(1, TOP_K):
        chosen = chosen | (lane == top_i[kk])
    hit = jnp.where(chosen, 1.0, 0.0)
    row = lax.broadcasted_iota(jnp.int32, (bm, bm), 0)
    col = lax.broadcasted_iota(jnp.int32, (bm, bm), 1)
    earlier = jnp.where(col < row, 1.0, 0.0).astype(BF16)
    before = jnp.dot(earlier, hit.astype(BF16), preferred_element_type=F32) + run_ref[...]
    for kk in range(TOP_K):
        idx_ref[:, kk:kk + 1] = top_i[kk]
        gate_ref[:, kk:kk + 1] = ex[kk] / denom
        rank = jnp.sum(jnp.where(lane == top_i[kk], before, 0.0), axis=-1, keepdims=True)
        rank_ref[:, kk:kk + 1] = rank.astype(jnp.int32)
    run_ref[...] += jnp.sum(hit, axis=0, keepdims=True)
    cnt_ref[...] = run_ref[...].astype(jnp.int32)


def _router(x, w_r, b_r):
    t, d = x.shape
    n_e = w_r.shape[1]
    bm = min(ROUTER_BM, t)
    return pl.pallas_call(
        _router_kernel,
        out_shape=(jax.ShapeDtypeStruct((t, TOP_K), jnp.int32),
                   jax.ShapeDtypeStruct((t, TOP_K), F32),
                   jax.ShapeDtypeStruct((t, TOP_K), jnp.int32),
                   jax.ShapeDtypeStruct((1, n_e), jnp.int32)),
        grid=(t // bm,),
        in_specs=[pl.BlockSpec((bm, d), lambda i: (i, 0)),
                  pl.BlockSpec((d, n_e), lambda i: (0, 0)),
                  pl.BlockSpec((1, n_e), lambda i: (0, 0))],
        out_specs=(pl.BlockSpec((bm, TOP_K), lambda i: (i, 0)),
                   pl.BlockSpec((bm, TOP_K), lambda i: (i, 0)),
                   pl.BlockSpec((bm, TOP_K), lambda i: (i, 0)),
                   pl.BlockSpec((1, n_e), lambda i: (0, 0))),
        scratch_shapes=[pltpu.VMEM((1, n_e), F32)],
        compiler_params=_params(("arbitrary",)),
        name="router_topk",
    )(x, w_r, b_r.reshape(1, n_e))


def _route_tables(idx, rank, counts, tm):
    t = idx.shape[0]
    n_e = counts.shape[0]
    padded = ((counts + tm - 1) // tm) * tm
    pend = jnp.cumsum(padded)
    pstart = pend - padded
    dest = (pstart[idx.reshape(-1, V7X_LANES)] + rank.reshape(-1, V7X_LANES))
    dest = dest.astype(jnp.int32).reshape(-1)
    n_rows = t * TOP_K + n_e * tm
    n_tiles = n_rows // tm
    tok = jnp.repeat(jnp.arange(t, dtype=jnp.int32), TOP_K)
    row_tok = (jnp.arange(n_rows, dtype=jnp.int32) % t).at[dest].set(tok)
    tile_end = pend // tm
    tile_id = jnp.arange(n_tiles, dtype=jnp.int32)
    expert_id = jnp.arange(n_e, dtype=jnp.int32)
    tile_expert = jnp.minimum(jnp.sum(tile_end[None, :] <= tile_id[:, None], axis=1),
                              n_e - 1).astype(jnp.int32)
    n_used = tile_end[-1:].astype(jnp.int32)
    nonempty = counts > 0
    group_of_expert = jnp.cumsum(nonempty.astype(jnp.int32)) - nonempty.astype(jnp.int32)
    at_or_after = lax.cummin(jnp.where(nonempty, expert_id, n_e)[::-1])[::-1]
    after = jnp.concatenate([at_or_after[1:], jnp.full((1,), n_e, jnp.int32)])
    per_expert = jnp.stack([group_of_expert, jnp.where(after < n_e, after, -1)], axis=1)
    onehot = (tile_expert[:, None] == expert_id[None, :]).astype(F32)
    per_tile = jnp.dot(onehot, per_expert.astype(F32),
                       precision=lax.Precision.HIGHEST).astype(jnp.int32)
    sched = (tile_expert, per_tile[:, 1], per_tile[:, 0], n_used)
    return dest, row_tok, sched, n_rows, n_tiles


def _dispatch_kernel(rt_ref, nu_ref, x_hbm, o_ref, buf, sem, *, tm):
    t = pl.program_id(0)
    n_used = nu_ref[0]

    def start(tile, slot):
        base = tile * tm

        def body(group, c):
            for u in range(DISPATCH_UNROLL):
                r = group * DISPATCH_UNROLL + u
                tok = rt_ref[base + r]
                pltpu.make_async_copy(x_hbm.at[pl.ds(tok, 1)], buf.at[slot, pl.ds(r, 1)],
                                      sem.at[slot]).start(priority=u % 2)
            return c

        lax.fori_loop(0, tm // DISPATCH_UNROLL, body, 0)

    @pl.when((t == 0) & (n_used > 0))
    def _():
        start(0, 0)

    @pl.when(t + 1 < n_used)
    def _():
        start(t + 1, (t + 1) % 2)

    @pl.when(t < n_used)
    def _():
        slot = t % 2
        pltpu.make_async_copy(x_hbm.at[pl.ds(0, tm)], buf.at[slot], sem.at[slot]).wait()
        o_ref[...] = buf[slot].astype(o_ref.dtype)

    @pl.when(t >= n_used)
    def _():
        o_ref[...] = jnp.zeros_like(o_ref)


def _dispatch(x, row_tok, n_used, tm, n_rows, n_tiles):
    d = x.shape[1]
    return pl.pallas_call(
        functools.partial(_dispatch_kernel, tm=tm),
        out_shape=jax.ShapeDtypeStruct((n_rows, d), BF16),
        grid_spec=pltpu.PrefetchScalarGridSpec(
            num_scalar_prefetch=2,
            grid=(n_tiles,),
            in_specs=[pl.BlockSpec(memory_space=pl.ANY)],
            out_specs=pl.BlockSpec((tm, d), lambda t, rt, nu: (t, 0)),
            scratch_shapes=[pltpu.VMEM((2, tm, d), F32), pltpu.SemaphoreType.DMA((2,))]),
        compiler_params=_params(("arbitrary",)),
        name="moe_dispatch_gather",
    )(row_tok, n_used, x)


def _group_start(te_ref, t):
    return (t == 0) | (te_ref[t] != te_ref[jnp.maximum(t - 1, 0)])


def _moe_up_kernel(te_ref, ne_ref, tg_ref, nu_ref, x_ref, w_hbm, b_ref, o_ref,
                   stage, w_bf, sem, *, layer, ff):
    t = pl.program_id(0)
    n_used = nu_ref[0]

    def weight_copy(e):
        return pltpu.make_async_copy(w_hbm.at[layer, e], stage, sem.at[0])

    @pl.when((t == 0) & (n_used > 0))
    def _():
        weight_copy(te_ref[0]).start()

    @pl.when((t < n_used) & _group_start(te_ref, t))
    def _():
        weight_copy(te_ref[t]).wait()
        w_bf[...] = stage[...].astype(BF16)
        nxt = ne_ref[t]

        @pl.when(nxt >= 0)
        def _():
            weight_copy(nxt).start()

    @pl.when(t < n_used)
    def _():
        h = jnp.dot(x_ref[...], w_bf[...], preferred_element_type=F32) + b_ref[...]
        h_glu = h[:, :ff]
        h_lin = h[:, ff:]
        h_glu = jnp.minimum(h_glu, SWIGLU_LIMIT)
        h_lin = jnp.clip(h_lin, -SWIGLU_LIMIT, SWIGLU_LIMIT)
        act = h_glu * jax.nn.sigmoid(SWIGLU_ALPHA * h_glu) * (h_lin + 1.0)
        o_ref[...] = act.astype(o_ref.dtype)

    @pl.when(t >= n_used)
    def _():
        o_ref[...] = jnp.zeros_like(o_ref)


def _moe_up(xs, w_gu, b_gu, layer, sched, tm, n_tiles):
    n_rows, d = xs.shape
    n_l, n_e, _, ff2 = w_gu.shape
    ff = ff2 // 2

    def row_map(t, te, ne, tg, nu):
        return (jnp.maximum(jnp.minimum(t, nu[0] - 1), 0), 0)

    return pl.pallas_call(
        functools.partial(_moe_up_kernel, layer=layer, ff=ff),
        out_shape=jax.ShapeDtypeStruct((n_rows, ff), BF16),
        grid_spec=pltpu.PrefetchScalarGridSpec(
            num_scalar_prefetch=4,
            grid=(n_tiles,),
            in_specs=[pl.BlockSpec((tm, d), row_map),
                      pl.BlockSpec(memory_space=pl.ANY),
                      pl.BlockSpec((None, None, 1, ff2),
                                   lambda t, te, ne, tg, nu: (layer, te[t], 0, 0))],
            out_specs=pl.BlockSpec((tm, ff), lambda t, te, ne, tg, nu: (t, 0)),
            scratch_shapes=[pltpu.VMEM((d, ff2), F32),
                            pltpu.VMEM((d, ff2), BF16),
                            pltpu.SemaphoreType.DMA((1,))]),
        compiler_params=_params(("arbitrary",)),
        name="moe_gate_up",
    )(*sched, xs, w_gu, b_gu.reshape(n_l, n_e, 1, ff2))


def _moe_down_kernel(te_ref, ne_ref, tg_ref, nu_ref, a_ref, w_hbm, b_ref, o_ref,
                     stage, w_bf, sem, *, layer):
    t = pl.program_id(0)
    n_used = nu_ref[0]

    def weight_copy(e, slot):
        return pltpu.make_async_copy(w_hbm.at[layer, e], stage.at[slot], sem.at[slot])

    @pl.when((t < n_used) & _group_start(te_ref, t))
    def _():
        e = te_ref[t]
        slot = tg_ref[t] % 2

        @pl.when(t == 0)
        def _():
            weight_copy(e, slot).start()

        nxt = ne_ref[t]

        @pl.when(nxt >= 0)
        def _():
            weight_copy(nxt, 1 - slot).start()

        weight_copy(e, slot).wait()
        w_bf[...] = stage[slot].astype(BF16)

    @pl.when(t < n_used)
    def _():
        y = jnp.dot(a_ref[...], w_bf[...], preferred_element_type=F32) + b_ref[...]
        half = y.shape[1] // 2
        o_ref[...] = _pack_bf16_pair(y[:, :half], y[:, half:])

    @pl.when(t >= n_used)
    def _():
        o_ref[...] = jnp.zeros_like(o_ref)


def _moe_down(act, w_dn, b_dn, layer, sched, tm, n_tiles):
    n_rows, ff = act.shape
    n_l, n_e, _, d = w_dn.shape
    return pl.pallas_call(
        functools.partial(_moe_down_kernel, layer=layer),
        out_shape=jax.ShapeDtypeStruct((n_rows, d // 2), jnp.int32),
        grid_spec=pltpu.PrefetchScalarGridSpec(
            num_scalar_prefetch=4,
            grid=(n_tiles,),
            in_specs=[pl.BlockSpec((tm, ff), lambda t, te, ne, tg, nu:
                                   (jnp.maximum(jnp.minimum(t, nu[0] - 1), 0), 0)),
                      pl.BlockSpec(memory_space=pl.ANY),
                      pl.BlockSpec((None, None, 1, d),
                                   lambda t, te, ne, tg, nu: (layer, te[t], 0, 0))],
            out_specs=pl.BlockSpec((tm, d // 2), lambda t, te, ne, tg, nu: (t, 0)),
            scratch_shapes=[pltpu.VMEM((2, ff, d), F32),
                            pltpu.VMEM((ff, d), BF16),
                            pltpu.SemaphoreType.DMA((2,))]),
        compiler_params=_params(("arbitrary",)),
        name="moe_down",
    )(*sched, act, w_dn, b_dn.reshape(n_l, n_e, 1, d))


def _combine_kernel(dest_ref, y_hbm, gate_ref, x_ref, g_ref, b_ref, of_ref, ob_ref, buf, sem,
                    *, tc, n_tiles):
    t = pl.program_id(0)

    def start(tile, slot):
        base = tile * tc * TOP_K

        def body(r, c):
            for kk in range(TOP_K):
                row = dest_ref[base + r * TOP_K + kk]
                pltpu.make_async_copy(y_hbm.at[pl.ds(row, 1)], buf.at[slot, kk, pl.ds(r, 1)],
                                      sem.at[slot]).start(priority=kk % 2)
            return c

        lax.fori_loop(0, tc, body, 0)

    @pl.when(t == 0)
    def _():
        start(0, 0)

    @pl.when(t + 1 < n_tiles)
    def _():
        start(t + 1, (t + 1) % 2)

    slot = t % 2
    for kk in range(TOP_K):
        pltpu.make_async_copy(y_hbm.at[pl.ds(0, tc)], buf.at[slot, kk], sem.at[slot]).wait()
    gates = gate_ref[...]
    lo, hi = _unpack_bf16_pair(buf[slot, 0])
    ffn_lo = gates[:, 0:1] * lo
    ffn_hi = gates[:, 0:1] * hi
    for kk in range(1, TOP_K):
        lo, hi = _unpack_bf16_pair(buf[slot, kk])
        ffn_lo = ffn_lo + gates[:, kk:kk + 1] * lo
        ffn_hi = ffn_hi + gates[:, kk:kk + 1] * hi
    ffn = jnp.concatenate([ffn_lo, ffn_hi], axis=1)
    out = _layer_norm_rows(DN_ALPHA * x_ref[...] + ffn, g_ref[...], b_ref[...])
    of_ref[...] = out
    ob_ref[...] = out.astype(ob_ref.dtype)


def _combine_ln(y, dest, gates, xres, g, b):
    t, d = xres.shape
    tc = min(COMBINE_TC, t)
    n_tiles = t // tc
    return pl.pallas_call(
        functools.partial(_combine_kernel, tc=tc, n_tiles=n_tiles),
        out_shape=(jax.ShapeDtypeStruct((t, d), F32), jax.ShapeDtypeStruct((t, d), BF16)),
        grid_spec=pltpu.PrefetchScalarGridSpec(
            num_scalar_prefetch=1,
            grid=(n_tiles,),
            in_specs=[pl.BlockSpec(memory_space=pl.ANY),
                      pl.BlockSpec((tc, TOP_K), lambda i, ds: (i, 0)),
                      pl.BlockSpec((tc, d), lambda i, ds: (i, 0)),
                      pl.BlockSpec((1, d), lambda i, ds: (0, 0)),
                      pl.BlockSpec((1, d), lambda i, ds: (0, 0))],
            out_specs=(pl.BlockSpec((tc, d), lambda i, ds: (i, 0)),
                       pl.BlockSpec((tc, d), lambda i, ds: (i, 0))),
            scratch_shapes=[pltpu.VMEM((2, TOP_K, tc, d // 2), jnp.int32),
                            pltpu.SemaphoreType.DMA((2,))]),
        compiler_params=_params(("arbitrary",)),
        name="moe_combine_deepnorm",
    )(dest.reshape(-1), y, gates, xres, g.reshape(1, d), b.reshape(1, d))


def _moe_block(x, w_r, b_r, w_gu, b_gu, w_dn, b_dn, layer, g, b):
    tm = MOE_TM
    idx, gates, rank, counts = _router(x, w_r, b_r)
    dest, row_tok, sched, n_rows, n_tiles = _route_tables(idx, rank, counts[0], tm)
    xs = _dispatch(x, row_tok, sched[-1], tm, n_rows, n_tiles)
    act = _moe_up(xs, w_gu, b_gu, layer, sched, tm, n_tiles)
    y = _moe_down(act, w_dn, b_dn, layer, sched, tm, n_tiles)
    return _combine_ln(y, dest, gates, x, g, b)


def kernel(x, gla_w_in, gla_w_gate2, gla_b_gate2, gla_norm_g, gla_w_out, sb_w_q, sb_w_out,
           shared_w_kv, router_w, router_b, moe_w_gate_up, moe_b_gate_up, moe_w_down, moe_b_down,
           ln1_g, ln1_b, ln2_g, ln2_b):
    bsz, seq, d = x.shape
    n_a = DEPTH // 2
    xf = x.reshape(bsz * seq, d)
    xb = xf.astype(BF16)
    kvp = None
    for layer in range(DEPTH):
        if layer < n_a:
            mix = _gla_mixer(xb, gla_w_in[layer], gla_w_gate2[layer], gla_b_gate2[layer],
                             gla_norm_g[layer], bsz, seq)
            w_out = gla_w_out[layer]
        else:
            if layer == n_a:
                kvp = _matmul_f32w(xb, shared_w_kv, shared_w_kv.shape[1], BF16)
            j = layer - n_a
            qp = _matmul_f32w(xb, sb_w_q[j], d, BF16)
            mix = _sb_attention(qp, kvp, bsz, seq)
            w_out = sb_w_out[j]
        xf = _matmul_ln(mix, w_out.astype(BF16), xf, ln1_g[layer], ln1_b[layer])
        xf, xb = _moe_block(xf, router_w[layer], router_b[layer], moe_w_gate_up, moe_b_gate_up,
                            moe_w_down, moe_b_down, layer, ln2_g[layer], ln2_b[layer])
    return xf.reshape(bsz, seq, d)
```

```python
import functools

import jax
import jax.numpy as jnp
from jax import lax
from jax.experimental import pallas as pl
from jax.experimental.pallas import tpu as pltpu

F32 = jnp.float32
BF16 = jnp.bfloat16

DEPTH = 2
CHUNK = 64
GLA_HEADS = 8
GLA_GATE_RANK = 16
GLA_TAU = 16.0
SB_HEADS = 32
TOP_K = 4
SWIGLU_LIMIT = 7.0
SWIGLU_ALPHA = 1.702
DN_ALPHA = (2.0 * DEPTH) ** 0.25
LN_EPS = 1e-5
RMS_EPS = 1e-6
LOG2_E = 1.4426950408889634

V7X_LANES = 128
V7X_VMEM_LIMIT_BYTES = 56 * 1024 * 1024

MM_BM = 512
MMW_BN = 1024
LN_BM = 128
GLA_TS = 256
GLA_HB = 4
SB_TQ = 256
SB_HB = 8
SB_STRIP = 32
ROUTER_BM = 512
MOE_TM = 256
DISPATCH_UNROLL = 4
COMBINE_TC = 128


def _params(semantics):
    return pltpu.CompilerParams(dimension_semantics=semantics,
                                vmem_limit_bytes=V7X_VMEM_LIMIT_BYTES)


def _mm_f32w_kernel(a_ref, w_ref, o_ref, w_bf, *, w_is_nk):
    @pl.when(pl.program_id(1) == 0)
    def _():
        w_bf[...] = w_ref[...].astype(BF16)

    contract = (((1,), (1 if w_is_nk else 0,)), ((), ()))
    o_ref[...] = lax.dot_general(a_ref[...], w_bf[...], contract,
                                 preferred_element_type=F32).astype(o_ref.dtype)


def _matmul_f32w(a, w, n, out_dtype, w_is_nk=False):
    m, k = a.shape
    bm = min(MM_BM, m)
    bn = min(MMW_BN, n)
    assert m % bm == 0 and n % bn == 0, (m, n, bm, bn)
    w_spec = (pl.BlockSpec((bn, k), lambda j, i: (j, 0)) if w_is_nk
              else pl.BlockSpec((k, bn), lambda j, i: (0, j)))
    return pl.pallas_call(
        functools.partial(_mm_f32w_kernel, w_is_nk=w_is_nk),
        out_shape=jax.ShapeDtypeStruct((m, n), out_dtype),
        grid=(n // bn, m // bm),
        in_specs=[pl.BlockSpec((bm, k), lambda j, i: (i, 0)), w_spec],
        out_specs=pl.BlockSpec((bm, bn), lambda j, i: (i, j)),
        scratch_shapes=[pltpu.VMEM((bn, k) if w_is_nk else (k, bn), BF16)],
        compiler_params=_params(("arbitrary", "arbitrary")),
        name="dense_matmul_f32w",
    )(a, w)


def _pack_bf16_pair(lo, hi):
    lo_bits = lax.bitcast_convert_type(lo.astype(BF16).astype(F32), jnp.int32)
    hi_bits = lax.bitcast_convert_type(hi.astype(BF16).astype(F32), jnp.int32)
    return hi_bits | lax.shift_right_logical(lo_bits, 16)


def _unpack_bf16_pair(packed):
    lo = lax.bitcast_convert_type(lax.shift_left(packed, 16), F32)
    hi = lax.bitcast_convert_type(packed & jnp.int32(-65536), F32)
    return lo, hi


def _layer_norm_rows(y, g, b):
    mu = jnp.mean(y, axis=-1, keepdims=True)
    yc = y - mu
    var = jnp.mean(yc * yc, axis=-1, keepdims=True)
    return yc * lax.rsqrt(var + LN_EPS) * g + b


def _mm_ln_kernel(a_ref, w_ref, x_ref, g_ref, b_ref, o_ref):
    y = DN_ALPHA * x_ref[...] + jnp.dot(a_ref[...], w_ref[...], preferred_element_type=F32)
    o_ref[...] = _layer_norm_rows(y, g_ref[...], b_ref[...])


def _matmul_ln(a, w, xres, g, b):
    m, k = a.shape
    n = w.shape[1]
    bm = min(LN_BM, m)
    return pl.pallas_call(
        _mm_ln_kernel,
        out_shape=jax.ShapeDtypeStruct((m, n), F32),
        grid=(m // bm,),
        in_specs=[pl.BlockSpec((bm, k), lambda i: (i, 0)),
                  pl.BlockSpec((k, n), lambda i: (0, 0), pipeline_mode=pl.Buffered(1)),
                  pl.BlockSpec((bm, n), lambda i: (i, 0)),
                  pl.BlockSpec((1, n), lambda i: (0, 0)),
                  pl.BlockSpec((1, n), lambda i: (0, 0))],
        out_specs=pl.BlockSpec((bm, n), lambda i: (i, 0)),
        compiler_params=_params(("arbitrary",)),
        name="matmul_deepnorm",
    )(a, w, xres, g.reshape(1, n), b.reshape(1, n))


def _gla_kernel(q_ref, k_ref, v_ref, r_ref, gl_ref, wg_ref, bg_ref, ng_ref, o_ref, state_ref,
                *, ts, dk, dv, hb):
    s = pl.program_id(2)

    @pl.when(s == 0)
    def _():
        state_ref[...] = jnp.zeros_like(state_ref)

    row = lax.broadcasted_iota(jnp.int32, (ts, ts), 0)
    col = lax.broadcasted_iota(jnp.int32, (ts, ts), 1)
    shift = CHUNK.bit_length() - 1
    later = ((row >> shift) == (col >> shift)) & (col > row)
    later = jnp.where(later, 1.0, 0.0).astype(BF16)
    lanes = min(dv, V7X_LANES)
    ones = jnp.ones((CHUNK, lanes), BF16)
    tn = (((0,), (0,)), ((), ()))
    code = gl_ref[...].astype(BF16)

    las, k_dec = [], []
    for h in range(hb):
        kc = slice(h * dk, (h + 1) * dk)
        z = jnp.dot(code, wg_ref[:, kc], preferred_element_type=F32) + bg_ref[:, kc]
        la = -(jnp.maximum(-z, 0.0) + jnp.log(1.0 + jnp.exp(-jnp.abs(z)))) * (1.0 / GLA_TAU)
        la = la.astype(BF16)
        to_end = jnp.dot(later, la, preferred_element_type=F32)
        las.append(la)
        k_dec.append((k_ref[:, kc].astype(F32) * jnp.exp(to_end)).astype(BF16))

    for c in range(ts // CHUNK):
        sl = slice(c * CHUNK, (c + 1) * CHUNK)
        states = []
        for h in range(hb):
            vc = slice(h * dv, (h + 1) * dv)
            chunk_log = lax.dot_general(las[h][sl], ones, tn, preferred_element_type=F32)
            decay = jnp.exp(chunk_log)
            kv = lax.dot_general(k_dec[h][sl], v_ref[sl, vc], tn, preferred_element_type=F32)
            state = jnp.concatenate([decay] * (dv // lanes), axis=1) * state_ref[h] + kv
            state_ref[h] = state
            states.append(state.astype(BF16))
        for h in range(hb):
            kc = slice(h * dk, (h + 1) * dk)
            vc = slice(h * dv, (h + 1) * dv)
            o = jnp.dot(q_ref[sl, kc], states[h], preferred_element_type=F32) * (dk ** -0.5)
            o = o * lax.rsqrt(jnp.mean(o * o, axis=-1, keepdims=True) + RMS_EPS) * ng_ref[...]
            r = r_ref[sl, vc].astype(F32)
            o_ref[sl, vc] = (o * (r * jax.nn.sigmoid(r))).astype(o_ref.dtype)


def _gla_core(proj, glow, wg2, bg2, norm_g, bsz, seq):
    t = proj.shape[0]
    qk = wg2.shape[1]
    nh = GLA_HEADS
    dk = qk // nh
    dv = norm_g.shape[0]
    hb = GLA_HB
    ng = nh // hb
    ts = min(GLA_TS, seq)
    ns = seq // ts
    v_blk0 = 2 * qk // (hb * dv)
    r_blk0 = v_blk0 + ng
    gw = glow.shape[1]
    return pl.pallas_call(
        functools.partial(_gla_kernel, ts=ts, dk=dk, dv=dv, hb=hb),
        out_shape=jax.ShapeDtypeStruct((t, nh * dv), BF16),
        grid=(bsz, ng, ns),
        in_specs=[pl.BlockSpec((ts, hb * dk), lambda b, g, s: (b * ns + s, g)),
                  pl.BlockSpec((ts, hb * dk), lambda b, g, s: (b * ns + s, ng + g)),
                  pl.BlockSpec((ts, hb * dv), lambda b, g, s: (b * ns + s, v_blk0 + g)),
                  pl.BlockSpec((ts, hb * dv), lambda b, g, s: (b * ns + s, r_blk0 + g)),
                  pl.BlockSpec((ts, gw), lambda b, g, s: (b * ns + s, 0)),
                  pl.BlockSpec((gw, hb * dk), lambda b, g, s: (0, g)),
                  pl.BlockSpec((1, hb * dk), lambda b, g, s: (0, g)),
                  pl.BlockSpec((1, dv), lambda b, g, s: (0, 0))],
        out_specs=pl.BlockSpec((ts, hb * dv), lambda b, g, s: (b * ns + s, g)),
        scratch_shapes=[pltpu.VMEM((hb, dk, dv), F32)],
        compiler_params=_params(("parallel", "parallel", "arbitrary")),
        name="gla_chunk_scan",
    )(proj, proj, proj, proj, glow, wg2, bg2.reshape(1, qk), norm_g.reshape(1, dv))


def _gla_mixer(x_bf, w_in, w_gate2, b_gate2, norm_g, bsz, seq):
    qk = w_gate2.shape[1]
    n_main = w_in.shape[1] - GLA_GATE_RANK
    w_in_t = w_in.T
    w_code_t = jnp.pad(w_in_t[n_main:], ((0, V7X_LANES - GLA_GATE_RANK), (0, 0)))
    wg2 = jnp.pad(w_gate2, ((0, V7X_LANES - GLA_GATE_RANK), (0, 0))).astype(BF16)
    proj = _matmul_f32w(x_bf, w_in_t, n_main, BF16, w_is_nk=True)
    glow = _matmul_f32w(x_bf, w_code_t, V7X_LANES, F32, w_is_nk=True)
    return _gla_core(proj, glow, wg2, b_gate2, norm_g, bsz, seq)


def _sb_kernel(q_ref, k_ref, v_ref, o_ref, z_scr, sp_scr, tail_scr, w_scr, right_scr, acc_scr,
               *, tq, dh, hb, rs):
    i = pl.program_id(2)
    scale = dh ** -0.5
    row = lax.broadcasted_iota(jnp.int32, (tq, tq), 0)
    col = lax.broadcasted_iota(jnp.int32, (tq, tq), 1)
    suffix = jnp.where(row >= col, 1.0, 0.0).astype(BF16)
    nt = (((1,), (1,)), ((), ()))
    heads = [slice(h * dh, (h + 1) * dh) for h in range(hb)]
    strips = [slice(r0, r0 + rs) for r0 in range(0, tq, rs)]

    def causal(rows):
        r = lax.broadcasted_iota(jnp.int32, (rs, tq), 0) + rows.start
        c = lax.broadcasted_iota(jnp.int32, (rs, tq), 1)
        return c < r

    right_scr[...] = jnp.zeros_like(right_scr)
    acc_scr[...] = jnp.zeros_like(acc_scr)

    def block(jb, masked):
        start = pl.multiple_of(jb * tq, tq)
        for h, c in enumerate(heads):
            z_scr[h] = lax.dot_general(q_ref[:, c], k_ref[pl.ds(start, tq), c], nt,
                                       preferred_element_type=F32) * (scale * LOG2_E)
        for h in range(hb):
            for rows in strips:
                z = z_scr[h, rows, :]
                sp = jnp.maximum(z, 0.0) + jnp.log2(1.0 + jnp.exp2(-jnp.abs(z)))
                if masked:
                    sp = jnp.where(causal(rows), sp, 0.0)
                sp_scr[h, rows, :] = sp.astype(BF16)
            tail_scr[h] = jnp.dot(sp_scr[h], suffix, preferred_element_type=F32)
        for h, c in enumerate(heads):
            for rows in strips:
                w = jnp.exp2(z_scr[h, rows, :] - tail_scr[h, rows, :] - right_scr[h, rows, :])
                if masked:
                    w = jnp.where(causal(rows), w, 0.0)
                w_scr[h, rows, :] = w.astype(BF16)
            acc_scr[h] += jnp.dot(w_scr[h], v_ref[pl.ds(start, tq), c],
                                  preferred_element_type=F32)
            right_scr[h] += tail_scr[h, :, :1]

    block(i, True)

    def body(n, carry):
        block(i - 1 - n, False)
        return carry

    lax.fori_loop(0, i, body, 0)
    for h, c in enumerate(heads):
        o_ref[:, c] = acc_scr[h].astype(o_ref.dtype)


def _sb_attention(qp, kvp, bsz, seq):
    t, d = qp.shape
    nh = SB_HEADS
    dh = d // nh
    hb = SB_HB
    ng = nh // hb
    tq = min(SB_TQ, seq)
    nq = seq // tq
    rs = min(SB_STRIP, tq)
    return pl.pallas_call(
        functools.partial(_sb_kernel, tq=tq, dh=dh, hb=hb, rs=rs),
        out_shape=jax.ShapeDtypeStruct((t, d), BF16),
        grid=(bsz, ng, nq),
        in_specs=[pl.BlockSpec((tq, hb * dh), lambda b, g, i: (b * nq + i, g)),
                  pl.BlockSpec((seq, hb * dh), lambda b, g, i: (b, g)),
                  pl.BlockSpec((seq, hb * dh), lambda b, g, i: (b, ng + g))],
        out_specs=pl.BlockSpec((tq, hb * dh), lambda b, g, i: (b * nq + i, g)),
        scratch_shapes=[pltpu.VMEM((hb, tq, tq), F32),
                        pltpu.VMEM((hb, tq, tq), BF16),
                        pltpu.VMEM((hb, tq, tq), F32),
                        pltpu.VMEM((hb, tq, tq), BF16),
                        pltpu.VMEM((hb, tq, 1), F32),
                        pltpu.VMEM((hb, tq, dh), F32)],
        compiler_params=_params(("parallel", "parallel", "arbitrary")),
        name="stick_breaking_attention",
    )(qp, kvp, kvp)


def _router_kernel(x_ref, w_ref, b_ref, idx_ref, gate_ref, rank_ref, cnt_ref, run_ref):
    i = pl.program_id(0)

    @pl.when(i == 0)
    def _():
        run_ref[...] = jnp.zeros_like(run_ref)

    x = x_ref[...]
    w = w_ref[...]
    x_hi = x.astype(BF16)
    x_lo = (x - x_hi.astype(F32)).astype(BF16)
    w_hi = w.astype(BF16)
    w_lo = (w - w_hi.astype(F32)).astype(BF16)
    logits = (jnp.dot(x_hi, w_hi, preferred_element_type=F32)
              + jnp.dot(x_lo, w_hi, preferred_element_type=F32)
              + jnp.dot(x_hi, w_lo, preferred_element_type=F32)) + b_ref[...]
    bm, n_e = logits.shape
    lane = lax.broadcasted_iota(jnp.int32, logits.shape, 1)
    vals = logits
    top_v, top_i = [], []
    for _ in range(TOP_K):
        m = jnp.max(vals, axis=-1, keepdims=True)
        sel = jnp.min(jnp.where(vals == m, lane, n_e), axis=-1, keepdims=True)
        top_v.append(m)
        top_i.append(sel)
        vals = jnp.where(lane == sel, -jnp.inf, vals)
    ex = [jnp.exp(v - top_v[0]) for v in top_v]
    denom = ex[0] + ex[1] + ex[2] + ex[3]

    chosen = lane == top_i[0]
    for kk in range(1, TOP_K):
        chosen = chosen | (lane == top_i[kk])
    hit = jnp.where(chosen, 1.0, 0.0)
    row = lax.broadcasted_iota(jnp.int32, (bm, bm), 0)
    col = lax.broadcasted_iota(jnp.int32, (bm, bm), 1)
    earlier = jnp.where(col < row, 1.0, 0.0).astype(BF16)
    before = jnp.dot(earlier, hit.astype(BF16), preferred_element_type=F32) + run_ref[...]
    for kk in range(TOP_K):
        idx_ref[:, kk:kk + 1] = top_i[kk]
        gate_ref[:, kk:kk + 1] = ex[kk] / denom
        rank = jnp.sum(jnp.where(lane == top_i[kk], before, 0.0), axis=-1, keepdims=True)
        rank_ref[:, kk:kk + 1] = rank.astype(jnp.int32)
    run_ref[...] += jnp.sum(hit, axis=0, keepdims=True)
    cnt_ref[...] = run_ref[...].astype(jnp.int32)


def _router(x, w_r, b_r):
    t, d = x.shape
    n_e = w_r.shape[1]
    bm = min(ROUTER_BM, t)
    return pl.pallas_call(
        _router_kernel,
        out_shape=(jax.ShapeDtypeStruct((t, TOP_K), jnp.int32),
                   jax.ShapeDtypeStruct((t, TOP_K), F32),
                   jax.ShapeDtypeStruct((t, TOP_K), jnp.int32),
                   jax.ShapeDtypeStruct((1, n_e), jnp.int32)),
        grid=(t // bm,),
        in_specs=[pl.BlockSpec((bm, d), lambda i: (i, 0)),
                  pl.BlockSpec((d, n_e), lambda i: (0, 0)),
                  pl.BlockSpec((1, n_e), lambda i: (0, 0))],
        out_specs=(pl.BlockSpec((bm, TOP_K), lambda i: (i, 0)),
                   pl.BlockSpec((bm, TOP_K), lambda i: (i, 0)),
                   pl.BlockSpec((bm, TOP_K), lambda i: (i, 0)),
                   pl.BlockSpec((1, n_e), lambda i: (0, 0))),
        scratch_shapes=[pltpu.VMEM((1, n_e), F32)],
        compiler_params=_params(("arbitrary",)),
        name="router_topk",
    )(x, w_r, b_r.reshape(1, n_e))


def _route_tables(idx, rank, counts, tm):
    t = idx.shape[0]
    n_e = counts.shape[0]
    padded = ((counts + tm - 1) // tm) * tm
    pend = jnp.cumsum(padded)
    pstart = pend - padded
    dest = (pstart[idx.reshape(-1, V7X_LANES)] + rank.reshape(-1, V7X_LANES))
    dest = dest.astype(jnp.int32).reshape(-1)
    n_rows = t * TOP_K + n_e * tm
    n_tiles = n_rows // tm
    tok = jnp.repeat(jnp.arange(t, dtype=jnp.int32), TOP_K)
    row_tok = (jnp.arange(n_rows, dtype=jnp.int32) % t).at[dest].set(tok)
    tile_end = pend // tm
    tile_id = jnp.arange(n_tiles, dtype=jnp.int32)
    expert_id = jnp.arange(n_e, dtype=jnp.int32)
    tile_expert = jnp.minimum(jnp.sum(tile_end[None, :] <= tile_id[:, None], axis=1),
                              n_e - 1).astype(jnp.int32)
    n_used = tile_end[-1:].astype(jnp.int32)
    nonempty = counts > 0
    group_of_expert = jnp.cumsum(nonempty.astype(jnp.int32)) - nonempty.astype(jnp.int32)
    at_or_after = lax.cummin(jnp.where(nonempty, expert_id, n_e)[::-1])[::-1]
    after = jnp.concatenate([at_or_after[1:], jnp.full((1,), n_e, jnp.int32)])
    per_expert = jnp.stack([group_of_expert, jnp.where(after < n_e, after, -1)], axis=1)
    onehot = (tile_expert[:, None] == expert_id[None, :]).astype(F32)
    per_tile = jnp.dot(onehot, per_expert.astype(F32),
                       precision=lax.Precision.HIGHEST).astype(jnp.int32)
    sched = (tile_expert, per_tile[:, 1], per_tile[:, 0], n_used)
    return dest, row_tok, sched, n_rows, n_tiles


def _dispatch_kernel(rt_ref, nu_ref, x_hbm, o_ref, buf, sem, *, tm):
    t = pl.program_id(0)
    n_used = nu_ref[0]

    def start(tile, slot):
        base = tile * tm

        def body(group, c):
            for u in range(DISPATCH_UNROLL):
                r = group * DISPATCH_UNROLL + u
                tok = rt_ref[base + r]
                pltpu.make_async_copy(x_hbm.at[pl.ds(tok, 1)], buf.at[slot, pl.ds(r, 1)],
                                      sem.at[slot]).start(priority=u % 2)
            return c

        lax.fori_loop(0, tm // DISPATCH_UNROLL, body, 0)

    @pl.when((t == 0) & (n_used > 0))
    def _():
        start(0, 0)

    @pl.when(t + 1 < n_used)
    def _():
        start(t + 1, (t + 1) % 2)

    @pl.when(t < n_used)
    def _():
        slot = t % 2
        pltpu.make_async_copy(x_hbm.at[pl.ds(0, tm)], buf.at[slot], sem.at[slot]).wait()
        o_ref[...] = buf[slot].astype(o_ref.dtype)

    @pl.when(t >= n_used)
    def _():
        o_ref[...] = jnp.zeros_like(o_ref)


def _dispatch(x, row_tok, n_used, tm, n_rows, n_tiles):
    d = x.shape[1]
    return pl.pallas_call(
        functools.partial(_dispatch_kernel, tm=tm),
        out_shape=jax.ShapeDtypeStruct((n_rows, d), BF16),
        grid_spec=pltpu.PrefetchScalarGridSpec(
            num_scalar_prefetch=2,
            grid=(n_tiles,),
            in_specs=[pl.BlockSpec(memory_space=pl.ANY)],
            out_specs=pl.BlockSpec((tm, d), lambda t, rt, nu: (t, 0)),
            scratch_shapes=[pltpu.VMEM((2, tm, d), F32), pltpu.SemaphoreType.DMA((2,))]),
        compiler_params=_params(("arbitrary",)),
        name="moe_dispatch_gather",
    )(row_tok, n_used, x)


def _group_start(te_ref, t):
    return (t == 0) | (te_ref[t] != te_ref[jnp.maximum(t - 1, 0)])


def _moe_up_kernel(te_ref, ne_ref, tg_ref, nu_ref, x_ref, w_hbm, b_ref, o_ref,
                   stage, w_bf, sem, *, layer, ff):
    t = pl.program_id(0)
    n_used = nu_ref[0]

    def weight_copy(e):
        return pltpu.make_async_copy(w_hbm.at[layer, e], stage, sem.at[0])

    @pl.when((t == 0) & (n_used > 0))
    def _():
        weight_copy(te_ref[0]).start()

    @pl.when((t < n_used) & _group_start(te_ref, t))
    def _():
        weight_copy(te_ref[t]).wait()
        w_bf[...] = stage[...].astype(BF16)
        nxt = ne_ref[t]

        @pl.when(nxt >= 0)
        def _():
            weight_copy(nxt).start()

    @pl.when(t < n_used)
    def _():
        h = jnp.dot(x_ref[...], w_bf[...], preferred_element_type=F32) + b_ref[...]
        h_glu = h[:, :ff]
        h_lin = h[:, ff:]
        h_glu = jnp.minimum(h_glu, SWIGLU_LIMIT)
        h_lin = jnp.clip(h_lin, -SWIGLU_LIMIT, SWIGLU_LIMIT)
        act = h_glu * jax.nn.sigmoid(SWIGLU_ALPHA * h_glu) * (h_lin + 1.0)
        o_ref[...] = act.astype(o_ref.dtype)

    @pl.when(t >= n_used)
    def _():
        o_ref[...] = jnp.zeros_like(o_ref)


def _moe_up(xs, w_gu, b_gu, layer, sched, tm, n_tiles):
    n_rows, d = xs.shape
    n_l, n_e, _, ff2 = w_gu.shape
    ff = ff2 // 2

    def row_map(t, te, ne, tg, nu):
        return (jnp.maximum(jnp.minimum(t, nu[0] - 1), 0), 0)

    return pl.pallas_call(
        functools.partial(_moe_up_kernel, layer=layer, ff=ff),
        out_shape=jax.ShapeDtypeStruct((n_rows, ff), BF16),
        grid_spec=pltpu.PrefetchScalarGridSpec(
            num_scalar_prefetch=4,
            grid=(n_tiles,),
            in_specs=[pl.BlockSpec((tm, d), row_map),
                      pl.BlockSpec(memory_space=pl.ANY),
                      pl.BlockSpec((None, None, 1, ff2),
                                   lambda t, te, ne, tg, nu: (layer, te[t], 0, 0))],
            out_specs=pl.BlockSpec((tm, ff), lambda t, te, ne, tg, nu: (t, 0)),
            scratch_shapes=[pltpu.VMEM((d, ff2), F32),
                            pltpu.VMEM((d, ff2), BF16),
                            pltpu.SemaphoreType.DMA((1,))]),
        compiler_params=_params(("arbitrary",)),
        name="moe_gate_up",
    )(*sched, xs, w_gu, b_gu.reshape(n_l, n_e, 1, ff2))


def _moe_down_kernel(te_ref, ne_ref, tg_ref, nu_ref, a_ref, w_hbm, b_ref, o_ref,
                     stage, w_bf, sem, *, layer):
    t = pl.program_id(0)
    n_used = nu_ref[0]

    def weight_copy(e, slot):
        return pltpu.make_async_copy(w_hbm.at[layer, e], stage.at[slot], sem.at[slot])

    @pl.when((t < n_used) & _group_start(te_ref, t))
    def _():
        e = te_ref[t]
        slot = tg_ref[t] % 2

        @pl.when(t == 0)
        def _():
            weight_copy(e, slot).start()

        nxt = ne_ref[t]

        @pl.when(nxt >= 0)
        def _():
            weight_copy(nxt, 1 - slot).start()

        weight_copy(e, slot).wait()
        w_bf[...] = stage[slot].astype(BF16)

    @pl.when(t < n_used)
    def _():
        y = jnp.dot(a_ref[...], w_bf[...], preferred_element_type=F32) + b_ref[...]
        half = y.shape[1] // 2
        o_ref[...] = _pack_bf16_pair(y[:, :half], y[:, half:])

    @pl.when(t >= n_used)
    def _():
        o_ref[...] = jnp.zeros_like(o_ref)


def _moe_down(act, w_dn, b_dn, layer, sched, tm, n_tiles):
    n_rows, ff = act.shape
    n_l, n_e, _, d = w_dn.shape
    return pl.pallas_call(
        functools.partial(_moe_down_kernel, layer=layer),
        out_shape=jax.ShapeDtypeStruct((n_rows, d // 2), jnp.int32),
        grid_spec=pltpu.PrefetchScalarGridSpec(
            num_scalar_prefetch=4,
            grid=(n_tiles,),
            in_specs=[pl.BlockSpec((tm, ff), lambda t, te, ne, tg, nu:
                                   (jnp.maximum(jnp.minimum(t, nu[0] - 1), 0), 0)),
                      pl.BlockSpec(memory_space=pl.ANY),
                      pl.BlockSpec((None, None, 1, d),
                                   lambda t, te, ne, tg, nu: (layer, te[t], 0, 0))],
            out_specs=pl.BlockSpec((tm, d // 2), lambda t, te, ne, tg, nu: (t, 0)),
            scratch_shapes=[pltpu.VMEM((2, ff, d), F32),
                            pltpu.VMEM((ff, d), BF16),
                            pltpu.SemaphoreType.DMA((2,))]),
        compiler_params=_params(("arbitrary",)),
        name="moe_down",
    )(*sched, act, w_dn, b_dn.reshape(n_l, n_e, 1, d))


def _combine_kernel(dest_ref, y_hbm, gate_ref, x_ref, g_ref, b_ref, of_ref, ob_ref, buf, sem,
                    *, tc, n_tiles):
    t = pl.program_id(0)

    def start(tile, slot):
        base = tile * tc * TOP_K

        def body(r, c):
            for kk in range(TOP_K):
                row = dest_ref[base + r * TOP_K + kk]
                pltpu.make_async_copy(y_hbm.at[pl.ds(row, 1)], buf.at[slot, kk, pl.ds(r, 1)],
                                      sem.at[slot]).start(priority=kk % 2)
            return c

        lax.fori_loop(0, tc, body, 0)

    @pl.when(t == 0)
    def _():
        start(0, 0)

    @pl.when(t + 1 < n_tiles)
    def _():
        start(t + 1, (t + 1) % 2)

    slot = t % 2
    for kk in range(TOP_K):
        pltpu.make_async_copy(y_hbm.at[pl.ds(0, tc)], buf.at[slot, kk], sem.at[slot]).wait()
    gates = gate_ref[...]
    lo, hi = _unpack_bf16_pair(buf[slot, 0])
    ffn_lo = gates[:, 0:1] * lo
    ffn_hi = gates[:, 0:1] * hi
    for kk in range(1, TOP_K):
        lo, hi = _unpack_bf16_pair(buf[slot, kk])
        ffn_lo = ffn_lo + gates[:, kk:kk + 1] * lo
        ffn_hi = ffn_hi + gates[:, kk:kk + 1] * hi
    ffn = jnp.concatenate([ffn_lo, ffn_hi], axis=1)
    out = _layer_norm_rows(DN_ALPHA * x_ref[...] + ffn, g_ref[...], b_ref[...])
    of_ref[...] = out
    ob_ref[...] = out.astype(ob_ref.dtype)


def _combine_ln(y, dest, gates, xres, g, b):
    t, d = xres.shape
    tc = min(COMBINE_TC, t)
    n_tiles = t // tc
    return pl.pallas_call(
        functools.partial(_combine_kernel, tc=tc, n_tiles=n_tiles),
        out_shape=(jax.ShapeDtypeStruct((t, d), F32), jax.ShapeDtypeStruct((t, d), BF16)),
        grid_spec=pltpu.PrefetchScalarGridSpec(
            num_scalar_prefetch=1,
            grid=(n_tiles,),
            in_specs=[pl.BlockSpec(memory_space=pl.ANY),
                      pl.BlockSpec((tc, TOP_K), lambda i, ds: (i, 0)),
                      pl.BlockSpec((tc, d), lambda i, ds: (i, 0)),
                      pl.BlockSpec((1, d), lambda i, ds: (0, 0)),
                      pl.BlockSpec((1, d), lambda i, ds: (0, 0))],
            out_specs=(pl.BlockSpec((tc, d), lambda i, ds: (i, 0)),
                       pl.BlockSpec((tc, d), lambda i, ds: (i, 0))),
            scratch_shapes=[pltpu.VMEM((2, TOP_K, tc, d // 2), jnp.int32),
                            pltpu.SemaphoreType.DMA((2,))]),
        compiler_params=_params(("arbitrary",)),
        name="moe_combine_deepnorm",
    )(dest.reshape(-1), y, gates, xres, g.reshape(1, d), b.reshape(1, d))


def _moe_block(x, w_r, b_r, w_gu, b_gu, w_dn, b_dn, layer, g, b):
    tm = MOE_TM
    idx, gates, rank, counts = _router(x, w_r, b_r)
    dest, row_tok, sched, n_rows, n_tiles = _route_tables(idx, rank, counts[0], tm)
    xs = _dispatch(x, row_tok, sched[-1], tm, n_rows, n_tiles)
    act = _moe_up(xs, w_gu, b_gu, layer, sched, tm, n_tiles)
    y = _moe_down(act, w_dn, b_dn, layer, sched, tm, n_tiles)
    return _combine_ln(y, dest, gates, x, g, b)


def kernel(x, gla_w_in, gla_w_gate2, gla_b_gate2, gla_norm_g, gla_w_out, sb_w_q, sb_w_out,
           shared_w_kv, router_w, router_b, moe_w_gate_up, moe_b_gate_up, moe_w_down, moe_b_down,
           ln1_g, ln1_b, ln2_g, ln2_b):
    bsz, seq, d = x.shape
    n_a = DEPTH // 2
    xf = x.reshape(bsz * seq, d)
    xb = xf.astype(BF16)
    kvp = None
    for layer in range(DEPTH):
        if layer < n_a:
            mix = _gla_mixer(xb, gla_w_in[layer], gla_w_gate2[layer], gla_b_gate2[layer],
                             gla_norm_g[layer], bsz, seq)
            w_out = gla_w_out[layer]
        else:
            if layer == n_a:
                kvp = _matmul_f32w(xb, shared_w_kv, shared_w_kv.shape[1], BF16)
            j = layer - n_a
            qp = _matmul_f32w(xb, sb_w_q[j], d, BF16)
            mix = _sb_attention(qp, kvp, bsz, seq)
            w_out = sb_w_out[j]
        xf = _matmul_ln(mix, w_out.astype(BF16), xf, ln1_g[layer], ln1_b[layer])
        xf, xb = _moe_block(xf, router_w[layer], router_b[layer], moe_w_gate_up, moe_b_gate_up,
                            moe_w_down, moe_b_down, layer, ln2_g[layer], ln2_b[layer])
    return xf.reshape(bsz, seq, d)
```
